```python
import jax, jax.numpy as jnp
from jax import lax
import numpy as np

D_MODEL = 1024
BATCH = 2
SEQ = 8192
DEPTH = 2

HEAD_DIM = 64
ROPE_THETA = 10000.0
EPS = 1e-6
Q_BLOCK = 128

A_HEADS = 4
MOBA_BLOCK = 256
MOBA_TOPK = 3
B_HEADS = 8
B_KV_HEADS = 2
WINDOW = 128
C_HEADS = 4
C_HALF = HEAD_DIM // 2

N_BRANCH = 3
D_FF = 2816
CONV_W = 3

A_W = A_HEADS * HEAD_DIM
B_QW = B_HEADS * HEAD_DIM
B_KVW = B_KV_HEADS * HEAD_DIM
C_W = C_HEADS * HEAD_DIM
IN_SIZES = (A_W, A_W, A_W, B_QW, B_KVW, B_KVW, C_W, C_W, C_W, N_BRANCH * D_MODEL)
IN_COLS = sum(IN_SIZES)
SPLIT_POINTS = tuple(np.cumsum(IN_SIZES)[:-1].tolist())

kernel_name = "hybrid_moba_swa_diff_convglu"


def rms_norm(x, g):
    x32 = x.astype(jnp.float32)
    y = x32 * lax.rsqrt(jnp.mean(x32 * x32, axis=-1, keepdims=True) + EPS)
    return (y * g.astype(jnp.float32)).astype(x.dtype)


def rope_tables(seq, dim):
    inv = 1.0 / (ROPE_THETA ** (jnp.arange(0, dim, 2, dtype=jnp.float32) / dim))
    ang = jnp.arange(seq, dtype=jnp.float32)[:, None] * inv[None, :]
    return jnp.cos(ang), jnp.sin(ang)


def apply_rope(x, cos, sin):
    half = x.shape[-1] // 2
    shape = (1, x.shape[1]) + (1,) * (x.ndim - 3) + (half,)
    c = cos.reshape(shape).astype(x.dtype)
    s = sin.reshape(shape).astype(x.dtype)
    x1, x2 = x[..., :half], x[..., half:]
    return jnp.concatenate([x1 * c - x2 * s, x2 * c + x1 * s], axis=-1)


def moba_attention(q, k, v):
    B, S, H, d = q.shape
    n_blk = -(-S // MOBA_BLOCK)
    pad = n_blk * MOBA_BLOCK - S
    padw = ((0, 0), (0, pad), (0, 0), (0, 0))
    kb = jnp.pad(k, padw).reshape(B, n_blk, MOBA_BLOCK, H, d).transpose(0, 3, 1, 2, 4)
    vb = jnp.pad(v, padw).reshape(B, n_blk, MOBA_BLOCK, H, d).transpose(0, 3, 1, 2, 4)
    k_mean = kb.mean(axis=3)
    topk = min(MOBA_TOPK, n_blk)
    n_chunks = S // Q_BLOCK
    qc = q.reshape(B, n_chunks, Q_BLOCK, H, d).transpose(1, 0, 3, 2, 4)
    b_idx = jnp.arange(B)[:, None, None, None]
    h_idx = jnp.arange(H)[None, :, None, None]
    scale = d ** -0.5
    n_sel = topk * MOBA_BLOCK

    def one_chunk(args):
        c, qh = args
        q_pos = c * Q_BLOCK + jnp.arange(Q_BLOCK)
        own = (c * Q_BLOCK) // MOBA_BLOCK
        gate = jnp.einsum('bhqd,bhnd->bhqn', qh, k_mean).astype(jnp.float32)
        gate = jnp.where(jnp.arange(n_blk) < own, gate, -jnp.inf)
        _, idx = lax.top_k(gate, topk)
        valid = jnp.arange(topk) < own
        k_sel = kb[b_idx, h_idx, idx]
        v_sel = vb[b_idx, h_idx, idx]
        s_sel = jnp.einsum('bhqd,bhqjld->bhqjl', qh, k_sel).astype(jnp.float32) * scale
        s_sel = jnp.where(valid[:, None], s_sel, -jnp.inf).reshape(B, H, Q_BLOCK, n_sel)
        k_own = lax.dynamic_index_in_dim(kb, own, axis=2, keepdims=False)
        v_own = lax.dynamic_index_in_dim(vb, own, axis=2, keepdims=False)
        s_own = jnp.einsum('bhqd,bhld->bhql', qh, k_own).astype(jnp.float32) * scale
        k_pos = own * MOBA_BLOCK + jnp.arange(MOBA_BLOCK)
        s_own = jnp.where(k_pos[None, :] <= q_pos[:, None], s_own, -jnp.inf)
        p = jax.nn.softmax(jnp.concatenate([s_sel, s_own], axis=-1), axis=-1).astype(v.dtype)
        p_sel = p[..., :n_sel].reshape(B, H, Q_BLOCK, topk, MOBA_BLOCK)
        p_own = p[..., n_sel:]
        return (jnp.einsum('bhqjl,bhqjld->bhqd', p_sel, v_sel)
                + jnp.einsum('bhql,bhld->bhqd', p_own, v_own))

    o = lax.map(one_chunk, (jnp.arange(n_chunks), qc))
    return o.transpose(1, 0, 3, 2, 4).reshape(B, S, H * d)


def sliding_window_sink_attention(q, k, v, sinks):
    B, S, Hq, d = q.shape
    Hkv = k.shape[2]
    G = Hq // Hkv
    nb = S // WINDOW
    qb = q.reshape(B, nb, WINDOW, Hkv, G, d)

    def band(t):
        tb = t.reshape(B, nb, WINDOW, Hkv, d)
        prev = jnp.concatenate([jnp.zeros_like(tb[:, :1]), tb[:, :-1]], axis=1)
        return jnp.concatenate([prev, tb], axis=2)

    kband, vband = band(k), band(v)
    s = jnp.einsum('bnqkgd,bnjkd->bnkgqj', qb, kband).astype(jnp.float32) * (d ** -0.5)
    rel = jnp.arange(WINDOW)[:, None] + WINDOW - jnp.arange(2 * WINDOW)[None, :]
    in_win = (rel >= 0) & (rel < WINDOW)
    k_abs = jnp.arange(nb)[:, None] * WINDOW - WINDOW + jnp.arange(2 * WINDOW)[None, :]
    mask = in_win[None, :, :] & (k_abs >= 0)[:, None, :]
    s = jnp.where(mask[None, :, None, None], s, -jnp.inf)
    sink = sinks.astype(jnp.float32).reshape(1, 1, Hkv, G, 1, 1)
    m = jnp.maximum(s.max(axis=-1, keepdims=True), sink)
    p = jnp.exp(s - m)
    denom = p.sum(axis=-1, keepdims=True) + jnp.exp(sink - m)
    o = jnp.einsum('bnkgqj,bnjkd->bnqkgd', (p / denom).astype(v.dtype), vband)
    return o.reshape(B, S, Hq * d)


def differential_attention(q, k, v, lam, subln_g, lam_init):
    B, S, H, _, dh = q.shape
    n_chunks = S // Q_BLOCK
    qc = q.reshape(B, n_chunks, Q_BLOCK, H, 2, dh).transpose(1, 0, 2, 3, 4, 5)
    scale = dh ** -0.5
    k_pos = jnp.arange(S)

    def one_chunk(args):
        c, qq = args
        q_pos = c * Q_BLOCK + jnp.arange(Q_BLOCK)
        s = jnp.einsum('bqhcd,bkhcd->bhcqk', qq, k).astype(jnp.float32) * scale
        s = jnp.where(k_pos[None, :] <= q_pos[:, None], s, -jnp.inf)
        p = jax.nn.softmax(s, axis=-1)
        a = (p[:, :, 0] - lam * p[:, :, 1]).astype(v.dtype)
        return jnp.einsum('bhqk,bkhd->bqhd', a, v)

    o = lax.map(one_chunk, (jnp.arange(n_chunks), qc))
    o = o.transpose(1, 0, 2, 3, 4).reshape(B, S, H, 2 * dh)
    o = rms_norm(o, subln_g) * (1.0 - lam_init)
    return o.reshape(B, S, H * 2 * dh)


def causal_depthwise_conv(u, w, b):
    S = u.shape[1]
    up = jnp.pad(u, ((0, 0), (CONV_W - 1, 0), (0, 0)))
    y = b.astype(u.dtype)
    for j in range(CONV_W):
        y = y + up[:, j:j + S] * w[j]
    return y


def setup_inputs(seed: int = 0) -> dict:
    key = jax.random.key(seed)
    ks = jax.random.split(key, 24)
    n = jax.random.normal
    f32 = jnp.float32

    def gain(k, shape):
        return 1.0 + 0.02 * n(k, shape, f32)

    return {
        "x": n(ks[0], (BATCH, SEQ, D_MODEL), f32),
        "attn_norm": gain(ks[1], (DEPTH, D_MODEL)),
        "w_in": n(ks[2], (DEPTH, D_MODEL, IN_COLS), f32) * D_MODEL ** -0.5,
        "qn_a": gain(ks[3], (DEPTH, HEAD_DIM)),
        "kn_a": gain(ks[4], (DEPTH, HEAD_DIM)),
        "qn_b": gain(ks[5], (DEPTH, HEAD_DIM)),
        "kn_b": gain(ks[6], (DEPTH, HEAD_DIM)),
        "sinks": 0.5 * n(ks[7], (DEPTH, B_HEADS), f32),
        "qn_c": gain(ks[8], (DEPTH, C_HALF)),
        "kn_c": gain(ks[9], (DEPTH, C_HALF)),
        "lam_q1": 0.1 * n(ks[10], (DEPTH, C_HALF), f32),
        "lam_k1": 0.1 * n(ks[11], (DEPTH, C_HALF), f32),
        "lam_q2": 0.1 * n(ks[12], (DEPTH, C_HALF), f32),
        "lam_k2": 0.1 * n(ks[13], (DEPTH, C_HALF), f32),
        "subln": gain(ks[14], (DEPTH, HEAD_DIM)),
        "w_pa": n(ks[15], (DEPTH, A_W, D_MODEL), f32) * A_W ** -0.5,
        "w_pb": n(ks[16], (DEPTH, B_QW, D_MODEL), f32) * B_QW ** -0.5,
        "w_pc": n(ks[17], (DEPTH, C_W, D_MODEL), f32) * C_W ** -0.5,
        "w_out": n(ks[18], (DEPTH, D_MODEL, D_MODEL), f32) * D_MODEL ** -0.5,
        "mlp_norm": gain(ks[19], (DEPTH, D_MODEL)),
        "w_up": n(ks[20], (DEPTH, D_MODEL, 2 * D_FF), f32) * D_MODEL ** -0.5,
        "conv_w": 0.5 * n(ks[21], (DEPTH, CONV_W, 2 * D_FF), f32),
        "conv_b": 0.01 * n(ks[22], (DEPTH, 2 * D_FF), f32),
        "w_down": n(ks[23], (DEPTH, D_FF, D_MODEL), f32) * D_FF ** -0.5,
    }


def reference(x, attn_norm, w_in, qn_a, kn_a, qn_b, kn_b, sinks, qn_c, kn_c,
              lam_q1, lam_k1, lam_q2, lam_k2, subln, w_pa, w_pb, w_pc, w_out,
              mlp_norm, w_up, conv_w, conv_b, w_down):
    B, S, D = x.shape
    cos64, sin64 = rope_tables(S, HEAD_DIM)
    cos32, sin32 = rope_tables(S, C_HALF)
    for i in range(DEPTH):
        lam_init = 0.8 - 0.6 * float(np.exp(-0.3 * i))
        h = rms_norm(x, attn_norm[i])
        proj = h @ w_in[i]
        qa, ka, va, qb, kb, vb, qc, kc, vc, gates = jnp.split(proj, SPLIT_POINTS, axis=-1)

        qa = apply_rope(rms_norm(qa.reshape(B, S, A_HEADS, HEAD_DIM), qn_a[i]), cos64, sin64)
        ka = apply_rope(rms_norm(ka.reshape(B, S, A_HEADS, HEAD_DIM), kn_a[i]), cos64, sin64)
        ya = moba_attention(qa, ka, va.reshape(B, S, A_HEADS, HEAD_DIM))

        qb = apply_rope(rms_norm(qb.reshape(B, S, B_HEADS, HEAD_DIM), qn_b[i]), cos64, sin64)
        kb = apply_rope(rms_norm(kb.reshape(B, S, B_KV_HEADS, HEAD_DIM), kn_b[i]), cos64, sin64)
        yb = sliding_window_sink_attention(qb, kb, vb.reshape(B, S, B_KV_HEADS, HEAD_DIM), sinks[i])

        qc = apply_rope(rms_norm(qc.reshape(B, S, C_HEADS, 2, C_HALF), qn_c[i]), cos32, sin32)
        kc = apply_rope(rms_norm(kc.reshape(B, S, C_HEADS, 2, C_HALF), kn_c[i]), cos32, sin32)
        lam = (jnp.exp(jnp.sum(lam_q1[i].astype(jnp.float32) * lam_k1[i].astype(jnp.float32)))
               - jnp.exp(jnp.sum(lam_q2[i].astype(jnp.float32) * lam_k2[i].astype(jnp.float32)))
               + lam_init)
        yc = differential_attention(qc, kc, vc.reshape(B, S, C_HEADS, HEAD_DIM), lam, subln[i], lam_init)

        g = jax.nn.sigmoid(gates.reshape(B, S, N_BRANCH, D))
        merged = g[:, :, 0] * (ya @ w_pa[i]) + g[:, :, 1] * (yb @ w_pb[i]) + g[:, :, 2] * (yc @ w_pc[i])
        x = x + merged @ w_out[i]

        h = rms_norm(x, mlp_norm[i])
        u = causal_depthwise_conv(h @ w_up[i], conv_w[i], conv_b[i])
        gate_u, val_u = jnp.split(u, 2, axis=-1)
        x = x + (jax.nn.silu(gate_u) * val_u) @ w_down[i]
    return x
```

```python
import functools

import numpy as np
import jax
import jax.numpy as jnp
from jax import lax
from jax.experimental import pallas as pl
from jax.experimental.pallas import tpu as pltpu

F32 = jnp.float32
BF16 = jnp.bfloat16
NEG_INF = float("-inf")

LANES = 128
VMEM_LIMIT = 48 * 1024 * 1024

EPS = 1e-6
HEAD_DIM = 64
ROPE_THETA = 10000.0
A_HEADS = 4
MOBA_BLOCK = 256
MOBA_TOPK = 3
B_HEADS = 8
B_KV_HEADS = 2
WINDOW = 128
C_HEADS = 4
C_HALF = HEAD_DIM // 2
N_BRANCH = 3
CONV_W = 3
CONV_HALO = 16

A_W = A_HEADS * HEAD_DIM
B_QW = B_HEADS * HEAD_DIM
B_KVW = B_KV_HEADS * HEAD_DIM
C_W = C_HEADS * HEAD_DIM


def _dot(a, b):
    return jnp.dot(a, b, preferred_element_type=F32)


def _dot_nt(a, b):
    return lax.dot_general(a, b, (((1,), (1,)), ((), ())), preferred_element_type=F32)


def _split3(a):
    a1 = a.astype(BF16)
    r = a - a1.astype(F32)
    a2 = r.astype(BF16)
    a3 = (r - a2.astype(F32)).astype(BF16)
    return a1, a2, a3


def _params(*semantics):
    return pltpu.CompilerParams(dimension_semantics=semantics, vmem_limit_bytes=VMEM_LIMIT)


def _inproj_kernel(x_ref, g_ref, w_ref, qkv_ref, gates_ref, h_scr, *, n_qkv_tiles):
    j = pl.program_id(1)

    @pl.when(j == 0)
    def _():
        x = x_ref[...]
        ms = jnp.mean(x * x, axis=-1, keepdims=True)
        h_scr[...] = (x * lax.rsqrt(ms + EPS) * g_ref[...]).astype(BF16)

    y = _dot(h_scr[...], w_ref[...])

    @pl.when(j < n_qkv_tiles)
    def _():
        qkv_ref[...] = y

    @pl.when(j >= n_qkv_tiles)
    def _():
        gates_ref[...] = y


def _inproj(x2, gain, w_bf, n_qkv, *, tm=512, tn=768):
    t, d = x2.shape
    n = w_bf.shape[1]
    nq = n_qkv // tn
    return pl.pallas_call(
        functools.partial(_inproj_kernel, n_qkv_tiles=nq),
        grid=(t // tm, n // tn),
        in_specs=[
            pl.BlockSpec((tm, d), lambda i, j: (i, 0)),
            pl.BlockSpec((1, d), lambda i, j: (0, 0)),
            pl.BlockSpec((d, tn), lambda i, j: (0, j)),
        ],
        out_specs=[pl.BlockSpec((tm, tn), lambda i, j: (i, jnp.minimum(j, nq - 1))),
                   pl.BlockSpec((tm, tn), lambda i, j: (i, jnp.maximum(j - nq, 0)))],
        out_shape=[jax.ShapeDtypeStruct((t, n_qkv), F32), jax.ShapeDtypeStruct((t, n - n_qkv), F32)],
        scratch_shapes=[pltpu.VMEM((tm, d), BF16)],
        compiler_params=_params("parallel", "arbitrary"),
        name="inproj",
    )(x2, gain.reshape(1, d), w_bf)


def _prep_kernel(*refs, hd, do_norm, scale, want_mean, want_swap):
    it = iter(refs)
    y_ref = next(it)
    if do_norm:
        gain_ref, cos_ref, sin_ref, gsum_ref = next(it), next(it), next(it), next(it)
    o_ref = next(it)
    mean_ref = next(it) if want_mean else None
    swap_ref = next(it) if want_swap else None

    y = y_ref[...]
    tm, w = y.shape
    if do_norm:
        sq = y * y
        hi = sq.astype(BF16)
        lo = (sq - hi.astype(F32)).astype(BF16)
        ms = (_dot(hi, gsum_ref[...]) + _dot(lo, gsum_ref[...])) * (1.0 / hd)
        yn = y * lax.rsqrt(ms + EPS) * gain_ref[...]
        half = hd // 2
        lane = lax.broadcasted_iota(jnp.int32, (tm, w), 1) % hd
        partner = jnp.where(lane < half, pltpu.roll(yn, w - half, 1), pltpu.roll(yn, half, 1))
        y = yn * cos_ref[...] + partner * sin_ref[...]
    if want_mean:
        mean_ref[...] = (jnp.sum(y, axis=0, keepdims=True) * (1.0 / tm)).reshape(1, 1, w)
    if scale != 1.0:
        y = y * scale
    o_ref[...] = y.astype(o_ref.dtype)
    if want_swap:
        swap_ref[...] = pltpu.roll(y, w // 2, 1).astype(swap_ref.dtype)


def _group_sum_matrix(w, hd):
    idx = np.arange(w) // hd
    return jnp.asarray((idx[:, None] == idx[None, :]).astype(np.float32), dtype=BF16)


def _prep(proj, col0, ncols, *, seq, tm, out_dtype, gain=None, cos=None, sin=None, hd=HEAD_DIM,
          scale=1.0, want_mean=False, want_swap=False):
    t = proj.shape[0]
    w = LANES
    do_norm = gain is not None
    cb0 = col0 // w
    nj = ncols // w
    tiles_per_seq = seq // tm
    in_specs = [pl.BlockSpec((tm, w), lambda i, j: (i, cb0 + j))]
    args = [proj]
    if do_norm:
        in_specs += [
            pl.BlockSpec((1, w), lambda i, j: (0, 0)),
            pl.BlockSpec((tm, w), lambda i, j: (i % tiles_per_seq, 0)),
            pl.BlockSpec((tm, w), lambda i, j: (i % tiles_per_seq, 0)),
            pl.BlockSpec((w, w), lambda i, j: (0, 0)),
        ]
        reps = w // hd
        args += [jnp.tile(gain, reps).reshape(1, w),
                 jnp.tile(jnp.concatenate([cos, cos], axis=-1), (1, reps)),
                 jnp.tile(jnp.concatenate([-sin, sin], axis=-1), (1, reps)),
                 _group_sum_matrix(w, hd)]
    out_shape = [jax.ShapeDtypeStruct((t, ncols), out_dtype)]
    out_specs = [pl.BlockSpec((tm, w), lambda i, j: (i, j))]
    if want_mean:
        out_shape.append(jax.ShapeDtypeStruct((t // tm, 1, ncols), F32))
        out_specs.append(pl.BlockSpec((1, 1, w), lambda i, j: (i, 0, j)))
    if want_swap:
        out_shape.append(jax.ShapeDtypeStruct((t, ncols), out_dtype))
        out_specs.append(pl.BlockSpec((tm, w), lambda i, j: (i, j)))
    kern = functools.partial(_prep_kernel, hd=hd, do_norm=do_norm, scale=scale,
                             want_mean=want_mean, want_swap=want_swap)
    outs = pl.pallas_call(
        kern,
        grid=(t // tm, nj),
        in_specs=in_specs,
        out_specs=out_specs,
        out_shape=out_shape,
        compiler_params=_params("parallel", "arbitrary"),
        name="prep",
    )(*args)
    return outs[0] if len(outs) == 1 else outs


def _moba_kernel(q_ref, k_ref, v_ref, km_ref, o_ref, *, tq):
    qi = pl.program_id(2)
    q = q_ref[...]
    km = km_ref[0]
    lane = lax.broadcasted_iota(jnp.int32, (tq, LANES), 1)
    row = lax.broadcasted_iota(jnp.int32, (tq, tq), 0)
    col = lax.broadcasted_iota(jnp.int32, (tq, tq), 1)
    start = pl.multiple_of(qi * tq, tq)
    k_own = k_ref[pl.ds(start, tq), :]
    v_own = v_ref[pl.ds(start, tq), :]
    km_parts = _split3(km)

    outs = []
    for h in range(LANES // HEAD_DIM):
        head = (lane >= h * HEAD_DIM) & (lane < (h + 1) * HEAD_DIM)
        qh = jnp.where(head, q, 0.0)

        q1, q2, q3 = _split3(qh)
        k1, k2, k3 = km_parts
        gate = (_dot_nt(q1, k1) + _dot_nt(q1, k2) + _dot_nt(q2, k1)
                + _dot_nt(q1, k3) + _dot_nt(q2, k2) + _dot_nt(q3, k1))
        g = jnp.where(lane < qi, gate, NEG_INF)
        sel = jnp.zeros((tq, LANES), F32)
        lane_f = lane.astype(F32)
        for _ in range(MOBA_TOPK):
            mx = jnp.max(g, axis=1, keepdims=True)
            first = jnp.min(jnp.where((g == mx) & (mx > NEG_INF), lane_f, float(LANES)), axis=1, keepdims=True)
            pick = lane_f == first
            sel = jnp.where(pick, 1.0, sel)
            g = jnp.where(pick, NEG_INF, g)

        qb = (qh * (HEAD_DIM ** -0.5)).astype(BF16)

        s = jnp.where(col <= row, _dot_nt(qb, k_own), NEG_INF)
        m = jnp.max(s, axis=1, keepdims=True)
        p = jnp.exp(s - m)
        l = jnp.sum(p, axis=1, keepdims=True)
        acc = _dot(p.astype(BF16), v_own)

        def body(j, carry, qb=qb, sel=sel):
            m, l, acc = carry
            off = pl.multiple_of(j * tq, tq)
            kj = k_ref[pl.ds(off, tq), :]
            vj = v_ref[pl.ds(off, tq), :]
            chosen = jnp.max(jnp.where(lane == j, sel, 0.0), axis=1, keepdims=True)
            s = jnp.where(chosen > 0.0, _dot_nt(qb, kj), NEG_INF)
            m_new = jnp.maximum(m, jnp.max(s, axis=1, keepdims=True))
            alpha = jnp.exp(m - m_new)
            p = jnp.exp(s - m_new)
            l = alpha * l + jnp.sum(p, axis=1, keepdims=True)
            acc = alpha * acc + _dot(p.astype(BF16), vj)
            return m_new, l, acc

        m, l, acc = lax.fori_loop(0, qi, body, (m, l, acc))
        outs.append(acc / l)

    o_ref[...] = jnp.where(lane < HEAD_DIM, outs[0], outs[1]).astype(o_ref.dtype)


def _moba(qa, ka, va, kmean, *, batch, seq):
    tq = MOBA_BLOCK
    t, width = qa.shape
    n_pairs = width // LANES
    n_q = seq // tq
    return pl.pallas_call(
        functools.partial(_moba_kernel, tq=tq),
        grid=(batch, n_pairs, n_q),
        in_specs=[
            pl.BlockSpec((tq, LANES), lambda b, hp, i: (b * n_q + i, hp)),
            pl.BlockSpec((seq, LANES), lambda b, hp, i: (b, hp)),
            pl.BlockSpec((seq, LANES), lambda b, hp, i: (b, hp)),
            pl.BlockSpec((1, LANES, LANES), lambda b, hp, i: (b, 0, hp)),
        ],
        out_specs=pl.BlockSpec((tq, LANES), lambda b, hp, i: (b * n_q + i, hp)),
        out_shape=jax.ShapeDtypeStruct((t, width), BF16),
        compiler_params=_params("parallel", "parallel", "arbitrary"),
        name="moba",
    )(qa, ka, va, kmean)


def _swa_kernel(q_ref, kpa_ref, kca_ref, kpb_ref, kcb_ref, vpa_ref, vca_ref, vpb_ref, vcb_ref,
                sink_ref, o_ref):
    i = pl.program_id(1)
    w = WINDOW
    k_a = jnp.concatenate([kpa_ref[...], kca_ref[...]], axis=0)
    k_b = jnp.concatenate([kpb_ref[...], kcb_ref[...]], axis=0)
    v_a = jnp.concatenate([vpa_ref[...], vca_ref[...]], axis=0)
    v_b = jnp.concatenate([vpb_ref[...], vcb_ref[...]], axis=0)
    r = lax.broadcasted_iota(jnp.int32, (w, 2 * w), 0)
    c = lax.broadcasted_iota(jnp.int32, (w, 2 * w), 1)
    rel = r + w - c
    ok = (rel >= 0) & (rel < w) & ((c >= w) | (i > 0))
    lane = lax.broadcasted_iota(jnp.int32, (w, LANES), 1)
    group = B_HEADS // B_KV_HEADS

    for pair in range(B_HEADS // 2):
        qblk = q_ref[:, pair * LANES:(pair + 1) * LANES]
        outs = []
        for hh in range(2):
            h = 2 * pair + hh
            kv = h // group
            kband, vband = (k_a, v_a) if hh == kv else (k_b, v_b)
            head = (lane >= hh * HEAD_DIM) & (lane < (hh + 1) * HEAD_DIM)
            qm = jnp.where(head, qblk, jnp.zeros_like(qblk))
            s = jnp.where(ok, _dot_nt(qm, kband), NEG_INF)
            sink = sink_ref[h:h + 1, 0:1]
            m = jnp.maximum(jnp.max(s, axis=1, keepdims=True), sink)
            p = jnp.exp(s - m)
            denom = jnp.sum(p, axis=1, keepdims=True) + jnp.exp(sink - m)
            outs.append(_dot(p.astype(BF16), vband) / denom)
        o_ref[:, pair * LANES:(pair + 1) * LANES] = jnp.where(lane < HEAD_DIM, outs[0], outs[1]).astype(o_ref.dtype)


def _swa(qb, kb_a, kb_b, vb_a, vb_b, sinks, *, batch, seq):
    t, qw = qb.shape
    w = WINDOW
    nb = seq // w
    prev = lambda b, i: (b * nb + jnp.maximum(i - 1, 0), 0)
    cur = lambda b, i: (b * nb + i, 0)
    kv_prev = pl.BlockSpec((w, LANES), prev)
    kv_cur = pl.BlockSpec((w, LANES), cur)
    sink_rows = jnp.broadcast_to(sinks.astype(F32)[:, None], (B_HEADS, LANES))
    return pl.pallas_call(
        _swa_kernel,
        grid=(batch, nb),
        in_specs=[pl.BlockSpec((w, qw), cur),
                  kv_prev, kv_cur, kv_prev, kv_cur, kv_prev, kv_cur, kv_prev, kv_cur,
                  pl.BlockSpec((B_HEADS, LANES), lambda b, i: (0, 0))],
        out_specs=pl.BlockSpec((w, qw), cur),
        out_shape=jax.ShapeDtypeStruct((t, qw), BF16),
        compiler_params=_params("parallel", "arbitrary"),
        name="swa",
    )(qb, kb_a, kb_a, kb_b, kb_b, vb_a, vb_a, vb_b, vb_b, sink_rows)


def _diff_kernel(q_ref, k_ref, v_ref, lam_ref, g_ref, o_ref, *, tq, lam_init):
    qi = pl.program_id(2)
    q = q_ref[...]
    lane = lax.broadcasted_iota(jnp.int32, (tq, LANES), 1)
    row = lax.broadcasted_iota(jnp.int32, (tq, tq), 0)
    col = lax.broadcasted_iota(jnp.int32, (tq, tq), 1)
    start = pl.multiple_of(qi * tq, tq)
    k_own = k_ref[pl.ds(start, tq), :]
    v_own = v_ref[pl.ds(start, tq), :]

    lq = lam_ref[...]
    lam = (jnp.exp(jnp.sum(lq[0:1] * lq[1:2], axis=1, keepdims=True))
           - jnp.exp(jnp.sum(lq[2:3] * lq[3:4], axis=1, keepdims=True)) + lam_init)

    def softmax_pv(qm):
        def step(s, vj, carry):
            m, l, acc = carry
            m_new = jnp.maximum(m, jnp.max(s, axis=1, keepdims=True))
            alpha = jnp.exp(m - m_new)
            p = jnp.exp(s - m_new)
            l = alpha * l + jnp.sum(p, axis=1, keepdims=True)
            acc = alpha * acc + _dot(p.astype(BF16), vj)
            return m_new, l, acc

        def body(j, carry):
            off = pl.multiple_of(j * tq, tq)
            return step(_dot_nt(qm, k_ref[pl.ds(off, tq), :]), v_ref[pl.ds(off, tq), :], carry)

        init = (jnp.full((tq, 1), NEG_INF, F32), jnp.zeros((tq, 1), F32), jnp.zeros((tq, LANES), F32))
        carry = lax.fori_loop(0, qi, body, init)
        s = jnp.where(col <= row, _dot_nt(qm, k_own), NEG_INF)
        _, l, acc = step(s, v_own, carry)
        return acc / l

    outs = []
    for h in range(LANES // HEAD_DIM):
        maps = []
        for c in range(2):
            lo = h * HEAD_DIM + c * C_HALF
            comp = (lane >= lo) & (lane < lo + C_HALF)
            maps.append(softmax_pv(jnp.where(comp, q, jnp.zeros_like(q))))
        outs.append(maps[0] - lam * maps[1])

    first = lane < HEAD_DIM
    o = jnp.where(first, outs[0], outs[1])
    sq = o * o
    ms0 = jnp.sum(jnp.where(first, sq, 0.0), axis=1, keepdims=True)
    ms1 = jnp.sum(jnp.where(first, 0.0, sq), axis=1, keepdims=True)
    ms = jnp.where(first, ms0, ms1) * (1.0 / HEAD_DIM)
    o_ref[...] = ((o * lax.rsqrt(ms + EPS) * g_ref[...]) * (1.0 - lam_init)).astype(o_ref.dtype)


def _diff(qc, kc, vc, lam_rows, subln, *, batch, seq, lam_init, tq=256):
    t, width = qc.shape
    n_pairs = width // LANES
    n_q = seq // tq
    return pl.pallas_call(
        functools.partial(_diff_kernel, tq=tq, lam_init=lam_init),
        grid=(batch, n_pairs, n_q),
        in_specs=[
            pl.BlockSpec((tq, LANES), lambda b, hp, i: (b * n_q + i, hp)),
            pl.BlockSpec((seq, LANES), lambda b, hp, i: (b, hp)),
            pl.BlockSpec((seq, LANES), lambda b, hp, i: (b, hp)),
            pl.BlockSpec((4, C_HALF), lambda b, hp, i: (0, 0)),
            pl.BlockSpec((1, LANES), lambda b, hp, i: (0, 0)),
        ],
        out_specs=pl.BlockSpec((tq, LANES), lambda b, hp, i: (b * n_q + i, hp)),
        out_shape=jax.ShapeDtypeStruct((t, width), BF16),
        compiler_params=_params("parallel", "parallel", "arbitrary"),
        name="diff",
    )(qc, kc, vc, lam_rows, jnp.tile(subln, LANES // HEAD_DIM).reshape(1, LANES))


def _merge_kernel(x_ref, ya_ref, yb_ref, yc_ref, ga_ref, gb_ref, gc_ref, wa_ref, wb_ref, wc_ref, wo_ref, o_ref):
    merged = (jax.nn.sigmoid(ga_ref[...]) * _dot(ya_ref[...], wa_ref[...])
              + jax.nn.sigmoid(gb_ref[...]) * _dot(yb_ref[...], wb_ref[...])
              + jax.nn.sigmoid(gc_ref[...]) * _dot(yc_ref[...], wc_ref[...]))
    o_ref[...] = x_ref[...] + _dot(merged.astype(BF16), wo_ref[...])


def _merge(x2, ya, yb, yc, gates, w_pa, w_pb, w_pc, w_out, *, tm=512):
    t, d = x2.shape
    rows = lambda i: (i, 0)
    fixed = lambda i: (0, 0)
    return pl.pallas_call(
        _merge_kernel,
        grid=(t // tm,),
        in_specs=[
            pl.BlockSpec((tm, d), rows),
            pl.BlockSpec((tm, ya.shape[1]), rows),
            pl.BlockSpec((tm, yb.shape[1]), rows),
            pl.BlockSpec((tm, yc.shape[1]), rows),
            pl.BlockSpec((tm, d), lambda i: (i, 0)),
            pl.BlockSpec((tm, d), lambda i: (i, 1)),
            pl.BlockSpec((tm, d), lambda i: (i, 2)),
            pl.BlockSpec(w_pa.shape, fixed),
            pl.BlockSpec(w_pb.shape, fixed),
            pl.BlockSpec(w_pc.shape, fixed),
            pl.BlockSpec(w_out.shape, fixed),
        ],
        out_specs=pl.BlockSpec((tm, d), rows),
        out_shape=jax.ShapeDtypeStruct((t, d), F32),
        compiler_params=_params("parallel"),
        name="merge",
    )(x2, ya, yb, yc, gates, gates, gates, w_pa, w_pb, w_pc, w_out)


def _convglu_kernel(x_ref, xh_ref, g_ref, wug_ref, wuv_ref, cwg_ref, cwv_ref, cbg_ref, cbv_ref, wd_ref, o_ref,
                    h_scr, ug_scr, uv_scr, acc_scr, *, tm, tiles_per_seq):
    i = pl.program_id(0)
    f = pl.program_id(1)
    halo = CONV_HALO

    def normed(x):
        ms = jnp.mean(x * x, axis=-1, keepdims=True)
        return x * lax.rsqrt(ms + EPS) * g_ref[...]

    @pl.when(f == 0)
    def _():
        keep = jnp.where(i % tiles_per_seq == 0, 0.0, 1.0)
        h_scr[0:halo, :] = (normed(xh_ref[...]) * keep).astype(BF16)
        h_scr[halo:, :] = normed(x_ref[...]).astype(BF16)
        acc_scr[...] = jnp.zeros_like(acc_scr)

    h = h_scr[...]
    ug_scr[...] = _dot(h, wug_ref[...])
    uv_scr[...] = _dot(h, wuv_ref[...])

    def conv(u_scr, cw_ref, cb_ref):
        y = cb_ref[...]
        for j in range(CONV_W):
            y = y + u_scr[pl.ds(halo - (CONV_W - 1) + j, tm), :] * cw_ref[j:j + 1, :]
        return y

    gate_u = conv(ug_scr, cwg_ref, cbg_ref)
    val_u = conv(uv_scr, cwv_ref, cbv_ref)
    act = (gate_u * jax.nn.sigmoid(gate_u)) * val_u
    acc_scr[...] += _dot(act.astype(BF16), wd_ref[...])

    @pl.when(f == pl.num_programs(1) - 1)
    def _():
        o_ref[...] = x_ref[...] + acc_scr[...]


def _convglu(x2, gain, w_up, conv_w, conv_b, w_down, *, seq, tm=512, tf=256):
    t, d = x2.shape
    ff = w_down.shape[0]
    nf = ff // tf
    halo = CONV_HALO
    tiles_per_seq = seq // tm
    halo_blocks = tm // halo
    return pl.pallas_call(
        functools.partial(_convglu_kernel, tm=tm, tiles_per_seq=tiles_per_seq),
        grid=(t // tm, nf),
        in_specs=[
            pl.BlockSpec((tm, d), lambda i, f: (i, 0)),
            pl.BlockSpec((halo, d), lambda i, f: (jnp.maximum(i * halo_blocks - 1, 0), 0)),
            pl.BlockSpec((1, d), lambda i, f: (0, 0)),
            pl.BlockSpec((d, tf), lambda i, f: (0, f)),
            pl.BlockSpec((d, tf), lambda i, f: (0, nf + f)),
            pl.BlockSpec((CONV_W, tf), lambda i, f: (0, f)),
            pl.BlockSpec((CONV_W, tf), lambda i, f: (0, nf + f)),
            pl.BlockSpec((1, tf), lambda i, f: (0, f)),
            pl.BlockSpec((1, tf), lambda i, f: (0, nf + f)),
            pl.BlockSpec((tf, d), lambda i, f: (f, 0)),
        ],
        out_specs=pl.BlockSpec((tm, d), lambda i, f: (i, 0)),
        out_shape=jax.ShapeDtypeStruct((t, d), F32),
        scratch_shapes=[
            pltpu.VMEM((tm + halo, d), BF16),
            pltpu.VMEM((tm + halo, tf), F32),
            pltpu.VMEM((tm + halo, tf), F32),
            pltpu.VMEM((tm, d), F32),
        ],
        compiler_params=_params("parallel", "arbitrary"),
        name="convglu",
    )(x2, x2, gain.reshape(1, d), w_up, w_up, conv_w, conv_w, conv_b.reshape(1, -1), conv_b.reshape(1, -1), w_down)


def _rope_tables(seq, dim):
    inv = 1.0 / (ROPE_THETA ** (jnp.arange(0, dim, 2, dtype=F32) / dim))
    ang = jnp.arange(seq, dtype=F32)[:, None] * inv[None, :]
    return jnp.cos(ang), jnp.sin(ang)


def kernel(x, attn_norm, w_in, qn_a, kn_a, qn_b, kn_b, sinks, qn_c, kn_c, lam_q1, lam_k1, lam_q2, lam_k2, subln,
           w_pa, w_pb, w_pc, w_out, mlp_norm, w_up, conv_w, conv_b, w_down):
    batch, seq, d = x.shape
    depth = w_in.shape[0]
    cos64, sin64 = _rope_tables(seq, HEAD_DIM)
    cos32, sin32 = _rope_tables(seq, C_HALF)
    x2 = x.reshape(batch * seq, d)

    sizes = (A_W, A_W, A_W, B_QW, B_KVW, B_KVW, C_W, C_W, C_W, N_BRANCH * d)
    offs = np.concatenate([[0], np.cumsum(sizes)]).tolist()
    o_qa, o_ka, o_va, o_qb, o_kb, o_vb, o_qc, o_kc, o_vc, o_g = offs[:10]

    for i in range(depth):
        lam_init = 0.8 - 0.6 * float(np.exp(-0.3 * i))
        qkv, gates = _inproj(x2, attn_norm[i], w_in[i].astype(BF16), o_g)

        prep = functools.partial(_prep, qkv, seq=seq)
        qa = prep(o_qa, A_W, tm=512, out_dtype=F32, gain=qn_a[i], cos=cos64, sin=sin64)
        ka, kmean = prep(o_ka, A_W, tm=MOBA_BLOCK, out_dtype=BF16, gain=kn_a[i], cos=cos64, sin=sin64,
                         want_mean=True)
        va = prep(o_va, A_W, tm=512, out_dtype=BF16)
        qb = prep(o_qb, B_QW, tm=512, out_dtype=BF16, gain=qn_b[i], cos=cos64, sin=sin64, scale=HEAD_DIM ** -0.5)
        kb_a, kb_b = prep(o_kb, B_KVW, tm=512, out_dtype=BF16, gain=kn_b[i], cos=cos64, sin=sin64, want_swap=True)
        vb_a, vb_b = prep(o_vb, B_KVW, tm=512, out_dtype=BF16, want_swap=True)
        qc = prep(o_qc, C_W, tm=512, out_dtype=BF16, gain=qn_c[i], cos=cos32, sin=sin32, hd=C_HALF,
                  scale=C_HALF ** -0.5)
        kc = prep(o_kc, C_W, tm=512, out_dtype=BF16, gain=kn_c[i], cos=cos32, sin=sin32, hd=C_HALF)
        vc = prep(o_vc, C_W, tm=512, out_dtype=BF16)

        n_blk = seq // MOBA_BLOCK
        kmean = jnp.pad(kmean.reshape(batch, n_blk, A_W), ((0, 0), (0, LANES - n_blk), (0, 0)))
        ya = _moba(qa, ka, va, kmean, batch=batch, seq=seq)
        yb = _swa(qb, kb_a, kb_b, vb_a, vb_b, sinks[i], batch=batch, seq=seq)
        lam_rows = jnp.stack([lam_q1[i], lam_k1[i], lam_q2[i], lam_k2[i]]).astype(F32)
        yc = _diff(qc, kc, vc, lam_rows, subln[i], batch=batch, seq=seq, lam_init=lam_init)

        x2 = _merge(x2, ya, yb, yc, gates, w_pa[i].astype(BF16), w_pb[i].astype(BF16), w_pc[i].astype(BF16),
                    w_out[i].astype(BF16))
        x2 = _convglu(x2, mlp_norm[i], w_up[i].astype(BF16), conv_w[i], conv_b[i], w_down[i].astype(BF16), seq=seq)

    return x2.reshape(batch, seq, d)
```

```python
import functools

import numpy as np
import jax
import jax.numpy as jnp
from jax import lax
from jax.experimental import pallas as pl
from jax.experimental.pallas import tpu as pltpu

F32 = jnp.float32
BF16 = jnp.bfloat16
NEG_INF = float("-inf")

LANES = 128
VMEM_LIMIT = 48 * 1024 * 1024

EPS = 1e-6
HEAD_DIM = 64
ROPE_THETA = 10000.0
A_HEADS = 4
MOBA_BLOCK = 256
MOBA_TOPK = 3
B_HEADS = 8
B_KV_HEADS = 2
WINDOW = 128
C_HEADS = 4
C_HALF = HEAD_DIM // 2
N_BRANCH = 3
CONV_W = 3
CONV_HALO = 16
BF16_SUBLANES = 16
V_ROWS = HEAD_DIM + BF16_SUBLANES
DIFF_BLOCK = 512
LOG2E = 1.4426950408889634

A_W = A_HEADS * HEAD_DIM
B_QW = B_HEADS * HEAD_DIM
B_KVW = B_KV_HEADS * HEAD_DIM
C_W = C_HEADS * HEAD_DIM


def _dot(a, b):
    return jnp.dot(a, b, preferred_element_type=F32)


def _dot_nt(a, b):
    return lax.dot_general(a, b, (((1,), (1,)), ((), ())), preferred_element_type=F32)


def _split3(a):
    a1 = a.astype(BF16)
    r = a - a1.astype(F32)
    a2 = r.astype(BF16)
    a3 = (r - a2.astype(F32)).astype(BF16)
    return a1, a2, a3


def _params(*semantics):
    return pltpu.CompilerParams(dimension_semantics=semantics, vmem_limit_bytes=VMEM_LIMIT)


def _inproj_kernel(x_ref, g_ref, w_ref, qkv_ref, gates_ref, h_scr, *, n_qkv_tiles):
    j = pl.program_id(1)

    @pl.when(j == 0)
    def _():
        x = x_ref[...]
        ms = jnp.mean(x * x, axis=-1, keepdims=True)
        h_scr[...] = (x * lax.rsqrt(ms + EPS) * g_ref[...]).astype(BF16)

    y = _dot(h_scr[...], w_ref[...])

    @pl.when(j < n_qkv_tiles)
    def _():
        qkv_ref[...] = y

    @pl.when(j >= n_qkv_tiles)
    def _():
        gates_ref[...] = y


def _inproj(x2, gain, w_bf, n_qkv, *, tm=512, tn=768):
    t, d = x2.shape
    n = w_bf.shape[1]
    nq = n_qkv // tn
    return pl.pallas_call(
        functools.partial(_inproj_kernel, n_qkv_tiles=nq),
        grid=(t // tm, n // tn),
        in_specs=[
            pl.BlockSpec((tm, d), lambda i, j: (i, 0)),
            pl.BlockSpec((1, d), lambda i, j: (0, 0)),
            pl.BlockSpec((d, tn), lambda i, j: (0, j)),
        ],
        out_specs=[pl.BlockSpec((tm, tn), lambda i, j: (i, jnp.minimum(j, nq - 1))),
                   pl.BlockSpec((tm, tn), lambda i, j: (i, jnp.maximum(j - nq, 0)))],
        out_shape=[jax.ShapeDtypeStruct((t, n_qkv), F32), jax.ShapeDtypeStruct((t, n - n_qkv), F32)],
        scratch_shapes=[pltpu.VMEM((tm, d), BF16)],
        compiler_params=_params("parallel", "arbitrary"),
        name="inproj",
    )(x2, gain.reshape(1, d), w_bf)


def _prep_kernel(*refs, hd, do_norm, scale, want_mean, want_swap):
    it = iter(refs)
    y_ref = next(it)
    if do_norm:
        gain_ref, cos_ref, sin_ref, gsum_ref = next(it), next(it), next(it), next(it)
    o_ref = next(it)
    mean_ref = next(it) if want_mean else None
    swap_ref = next(it) if want_swap else None

    y = y_ref[...]
    tm, w = y.shape
    if do_norm:
        sq = y * y
        hi = sq.astype(BF16)
        lo = (sq - hi.astype(F32)).astype(BF16)
        ms = (_dot(hi, gsum_ref[...]) + _dot(lo, gsum_ref[...])) * (1.0 / hd)
        yn = y * lax.rsqrt(ms + EPS) * gain_ref[...]
        half = hd // 2
        lane = lax.broadcasted_iota(jnp.int32, (tm, w), 1) % hd
        partner = jnp.where(lane < half, pltpu.roll(yn, w - half, 1), pltpu.roll(yn, half, 1))
        y = yn * cos_ref[...] + partner * sin_ref[...]
    if want_mean:
        mean_ref[...] = (jnp.sum(y, axis=0, keepdims=True) * (1.0 / tm)).reshape(1, 1, w)
    if scale != 1.0:
        y = y * scale
    o_ref[...] = y.astype(o_ref.dtype)
    if want_swap:
        swap_ref[...] = pltpu.roll(y, w // 2, 1).astype(swap_ref.dtype)


def _group_sum_matrix(w, hd):
    idx = np.arange(w) // hd
    return jnp.asarray((idx[:, None] == idx[None, :]).astype(np.float32), dtype=BF16)


def _prep(proj, col0, ncols, *, seq, tm, out_dtype, gain=None, cos=None, sin=None, hd=HEAD_DIM,
          scale=1.0, want_mean=False, want_swap=False):
    t = proj.shape[0]
    w = LANES
    do_norm = gain is not None
    cb0 = col0 // w
    nj = ncols // w
    tiles_per_seq = seq // tm
    in_specs = [pl.BlockSpec((tm, w), lambda i, j: (i, cb0 + j))]
    args = [proj]
    if do_norm:
        in_specs += [
            pl.BlockSpec((1, w), lambda i, j: (0, 0)),
            pl.BlockSpec((tm, w), lambda i, j: (i % tiles_per_seq, 0)),
            pl.BlockSpec((tm, w), lambda i, j: (i % tiles_per_seq, 0)),
            pl.BlockSpec((w, w), lambda i, j: (0, 0)),
        ]
        reps = w // hd
        args += [jnp.tile(gain, reps).reshape(1, w),
                 jnp.tile(jnp.concatenate([cos, cos], axis=-1), (1, reps)),
                 jnp.tile(jnp.concatenate([-sin, sin], axis=-1), (1, reps)),
                 _group_sum_matrix(w, hd)]
    out_shape = [jax.ShapeDtypeStruct((t, ncols), out_dtype)]
    out_specs = [pl.BlockSpec((tm, w), lambda i, j: (i, j))]
    if want_mean:
        out_shape.append(jax.ShapeDtypeStruct((t // tm, 1, ncols), F32))
        out_specs.append(pl.BlockSpec((1, 1, w), lambda i, j: (i, 0, j)))
    if want_swap:
        out_shape.append(jax.ShapeDtypeStruct((t, ncols), out_dtype))
        out_specs.append(pl.BlockSpec((tm, w), lambda i, j: (i, j)))
    kern = functools.partial(_prep_kernel, hd=hd, do_norm=do_norm, scale=scale,
                             want_mean=want_mean, want_swap=want_swap)
    outs = pl.pallas_call(
        kern,
        grid=(t // tm, nj),
        in_specs=in_specs,
        out_specs=out_specs,
        out_shape=out_shape,
        compiler_params=_params("parallel", "arbitrary"),
        name="prep",
    )(*args)
    return outs[0] if len(outs) == 1 else outs


def _vt_kernel(y_ref, o_ref):
    yt = y_ref[...].T
    tk = yt.shape[1]
    ones = jnp.ones((V_ROWS - HEAD_DIM, tk), F32)
    for h in range(LANES // HEAD_DIM):
        o_ref[0, h, 0] = jnp.concatenate([yt[h * HEAD_DIM:(h + 1) * HEAD_DIM], ones], axis=0).astype(o_ref.dtype)


def _values_transposed(proj, col0, ncols, *, batch, seq, tk):
    n_heads = LANES // HEAD_DIM
    cb0 = col0 // LANES
    nkb = seq // tk
    return pl.pallas_call(
        _vt_kernel,
        grid=(batch * nkb, ncols // LANES),
        in_specs=[pl.BlockSpec((tk, LANES), lambda i, j: (i, cb0 + j))],
        out_specs=pl.BlockSpec((1, n_heads, 1, V_ROWS, tk), lambda i, j: (i // nkb, j, i % nkb, 0, 0)),
        out_shape=jax.ShapeDtypeStruct((batch, ncols // HEAD_DIM, nkb, V_ROWS, tk), BF16),
        compiler_params=_params("parallel", "arbitrary"),
        name="vt",
    )(proj)


def _flash_update(acc_ref, stream, s, vt, m):
    m_new = jnp.maximum(m, jnp.max(s, axis=0, keepdims=True))
    alpha = jnp.exp2(m - m_new)
    p = jnp.exp2(s - m_new).astype(BF16)
    acc_ref[stream] = alpha * acc_ref[stream] + _dot(vt, p)
    return m_new


def _moba_kernel(q_ref, k_ref, vt_ref, km_ref, o_ref, acc_scr, sel_scr, *, tq):
    qi = pl.program_id(2)
    n_heads = LANES // HEAD_DIM
    qt = q_ref[...].T
    km = km_ref[0]
    feat = lax.broadcasted_iota(jnp.int32, (LANES, tq), 0)
    blk_f = feat.astype(F32)
    krow = lax.broadcasted_iota(jnp.int32, (tq, tq), 0)
    qcol = lax.broadcasted_iota(jnp.int32, (tq, tq), 1)
    k1, k2, k3 = _split3(km)

    qts = []
    for h in range(n_heads):
        qh = jnp.where((feat >= h * HEAD_DIM) & (feat < (h + 1) * HEAD_DIM), qt, 0.0)

        q1, q2, q3 = _split3(qh)
        gate = (_dot(k1, q1) + _dot(k2, q1) + _dot(k1, q2) + _dot(k3, q1) + _dot(k2, q2) + _dot(k1, q3))
        g = jnp.where(feat < qi, gate, NEG_INF)
        sel = jnp.zeros((LANES, tq), F32)
        for _ in range(MOBA_TOPK):
            mx = jnp.max(g, axis=0, keepdims=True)
            first = jnp.min(jnp.where((g == mx) & (mx > NEG_INF), blk_f, float(LANES)), axis=0, keepdims=True)
            pick = blk_f == first
            sel = jnp.where(pick, 1.0, sel)
            g = jnp.where(pick, NEG_INF, g)
        sel_scr[h] = sel
        qts.append((qh * (HEAD_DIM ** -0.5 * LOG2E)).astype(BF16))

    acc_scr[...] = jnp.zeros_like(acc_scr)

    start = pl.multiple_of(qi * tq, tq)
    k_own = k_ref[pl.ds(start, tq), :]
    ms = []
    for h in range(n_heads):
        s = jnp.where(krow <= qcol, _dot(k_own, qts[h]), NEG_INF)
        ms.append(_flash_update(acc_scr, h, s, vt_ref[0, h, qi], jnp.full((1, tq), NEG_INF, F32)))

    def body(j, ms):
        kj = k_ref[pl.ds(pl.multiple_of(j * tq, tq), tq), :]
        out = []
        for h in range(n_heads):
            chosen = sel_scr[h, pl.ds(j, 1), :]
            s = jnp.where(chosen > 0.0, _dot(kj, qts[h]), NEG_INF)
            out.append(_flash_update(acc_scr, h, s, vt_ref[0, h, j], ms[h]))
        return tuple(out)

    lax.fori_loop(0, qi, body, tuple(ms))

    outs = [acc_scr[h, 0:HEAD_DIM, :] / acc_scr[h, HEAD_DIM:HEAD_DIM + 1, :] for h in range(n_heads)]
    o_ref[...] = jnp.concatenate(outs, axis=0).T.astype(o_ref.dtype)


def _moba(qa, ka, vta, kmean, *, batch, seq):
    tq = MOBA_BLOCK
    t, width = qa.shape
    n_pairs = width // LANES
    n_heads = LANES // HEAD_DIM
    n_q = seq // tq
    return pl.pallas_call(
        functools.partial(_moba_kernel, tq=tq),
        grid=(batch, n_pairs, n_q),
        in_specs=[
            pl.BlockSpec((tq, LANES), lambda b, hp, i: (b * n_q + i, hp)),
            pl.BlockSpec((seq, LANES), lambda b, hp, i: (b, hp)),
            pl.BlockSpec((1, n_heads, n_q, V_ROWS, tq), lambda b, hp, i: (b, hp, 0, 0, 0)),
            pl.BlockSpec((1, LANES, LANES), lambda b, hp, i: (b, 0, hp)),
        ],
        out_specs=pl.BlockSpec((tq, LANES), lambda b, hp, i: (b * n_q + i, hp)),
        out_shape=jax.ShapeDtypeStruct((t, width), BF16),
        scratch_shapes=[pltpu.VMEM((n_heads, V_ROWS, tq), F32), pltpu.VMEM((n_heads, LANES, tq), F32)],
        compiler_params=_params("parallel", "parallel", "arbitrary"),
        name="moba",
    )(qa, ka, vta, kmean)


def _swa_kernel(q_ref, kpa_ref, kca_ref, kpb_ref, kcb_ref, vpa_ref, vca_ref, vpb_ref, vcb_ref,
                sink_ref, o_ref):
    i = pl.program_id(1)
    w = WINDOW
    k_a = jnp.concatenate([kpa_ref[...], kca_ref[...]], axis=0)
    k_b = jnp.concatenate([kpb_ref[...], kcb_ref[...]], axis=0)
    v_a = jnp.concatenate([vpa_ref[...], vca_ref[...]], axis=0)
    v_b = jnp.concatenate([vpb_ref[...], vcb_ref[...]], axis=0)
    r = lax.broadcasted_iota(jnp.int32, (w, 2 * w), 0)
    c = lax.broadcasted_iota(jnp.int32, (w, 2 * w), 1)
    rel = r + w - c
    ok = (rel >= 0) & (rel < w) & ((c >= w) | (i > 0))
    lane = lax.broadcasted_iota(jnp.int32, (w, LANES), 1)
    group = B_HEADS // B_KV_HEADS

    for pair in range(B_HEADS // 2):
        qblk = q_ref[:, pair * LANES:(pair + 1) * LANES]
        outs = []
        for hh in range(2):
            h = 2 * pair + hh
            kv = h // group
            kband, vband = (k_a, v_a) if hh == kv else (k_b, v_b)
            head = (lane >= hh * HEAD_DIM) & (lane < (hh + 1) * HEAD_DIM)
            qm = jnp.where(head, qblk, jnp.zeros_like(qblk))
            s = jnp.where(ok, _dot_nt(qm, kband), NEG_INF)
            sink = sink_ref[h:h + 1, 0:1]
            m = jnp.maximum(jnp.max(s, axis=1, keepdims=True), sink)
            p = jnp.exp(s - m)
            denom = jnp.sum(p, axis=1, keepdims=True) + jnp.exp(sink - m)
            outs.append(_dot(p.astype(BF16), vband) / denom)
        o_ref[:, pair * LANES:(pair + 1) * LANES] = jnp.where(lane < HEAD_DIM, outs[0], outs[1]).astype(o_ref.dtype)


def _swa(qb, kb_a, kb_b, vb_a, vb_b, sinks, *, batch, seq):
    t, qw = qb.shape
    w = WINDOW
    nb = seq // w
    prev = lambda b, i: (b * nb + jnp.maximum(i - 1, 0), 0)
    cur = lambda b, i: (b * nb + i, 0)
    kv_prev = pl.BlockSpec((w, LANES), prev)
    kv_cur = pl.BlockSpec((w, LANES), cur)
    sink_rows = jnp.broadcast_to(sinks.astype(F32)[:, None], (B_HEADS, LANES))
    return pl.pallas_call(
        _swa_kernel,
        grid=(batch, nb),
        in_specs=[pl.BlockSpec((w, qw), cur),
                  kv_prev, kv_cur, kv_prev, kv_cur, kv_prev, kv_cur, kv_prev, kv_cur,
                  pl.BlockSpec((B_HEADS, LANES), lambda b, i: (0, 0))],
        out_specs=pl.BlockSpec((w, qw), cur),
        out_shape=jax.ShapeDtypeStruct((t, qw), BF16),
        compiler_params=_params("parallel", "arbitrary"),
        name="swa",
    )(qb, kb_a, kb_a, kb_b, kb_b, vb_a, vb_a, vb_b, vb_b, sink_rows)


def _diff_kernel(q_ref, k_ref, vt_ref, lam_ref, g_ref, o_ref, acc_scr, *, blk, lam_init):
    qi = pl.program_id(2)
    n_heads = LANES // HEAD_DIM
    qt = q_ref[...].T
    feat = lax.broadcasted_iota(jnp.int32, (LANES, blk), 0)
    krow = lax.broadcasted_iota(jnp.int32, (blk, blk), 0)
    qcol = lax.broadcasted_iota(jnp.int32, (blk, blk), 1)

    lq = lam_ref[...]
    lam = (jnp.exp(jnp.sum(lq[0:1] * lq[1:2], axis=1, keepdims=True))
           - jnp.exp(jnp.sum(lq[2:3] * lq[3:4], axis=1, keepdims=True)) + lam_init)

    n_streams = 2 * n_heads
    qts = [jnp.where((feat >= st * C_HALF) & (feat < (st + 1) * C_HALF), qt, 0.0).astype(BF16)
           for st in range(n_streams)]
    acc_scr[...] = jnp.zeros_like(acc_scr)

    def body(j, ms):
        kj = k_ref[pl.ds(pl.multiple_of(j * blk, blk), blk), :]
        return tuple(_flash_update(acc_scr, st, _dot(kj, qts[st]), vt_ref[0, st // 2, j], ms[st])
                     for st in range(n_streams))

    ms = lax.fori_loop(0, qi, body, tuple(jnp.full((1, blk), NEG_INF, F32) for _ in range(n_streams)))
    k_own = k_ref[pl.ds(pl.multiple_of(qi * blk, blk), blk), :]
    for st in range(n_streams):
        s = jnp.where(krow <= qcol, _dot(k_own, qts[st]), NEG_INF)
        _flash_update(acc_scr, st, s, vt_ref[0, st // 2, qi], ms[st])

    outs = []
    for h in range(n_heads):
        maps = [acc_scr[2 * h + c, 0:HEAD_DIM, :] / acc_scr[2 * h + c, HEAD_DIM:HEAD_DIM + 1, :] for c in range(2)]
        o = maps[0] - lam * maps[1]
        ms_o = jnp.mean(o * o, axis=0, keepdims=True)
        outs.append((o * lax.rsqrt(ms_o + EPS) * g_ref[...]) * (1.0 - lam_init))
    o_ref[...] = jnp.concatenate(outs, axis=0).T.astype(o_ref.dtype)


def _diff(qc, kc, vtc, lam_rows, subln, *, batch, seq, lam_init, blk):
    t, width = qc.shape
    n_pairs = width // LANES
    n_heads = LANES // HEAD_DIM
    n_q = seq // blk
    return pl.pallas_call(
        functools.partial(_diff_kernel, blk=blk, lam_init=lam_init),
        grid=(batch, n_pairs, n_q),
        in_specs=[
            pl.BlockSpec((blk, LANES), lambda b, hp, i: (b * n_q + i, hp)),
            pl.BlockSpec((seq, LANES), lambda b, hp, i: (b, hp)),
            pl.BlockSpec((1, n_heads, n_q, V_ROWS, blk), lambda b, hp, i: (b, hp, 0, 0, 0)),
            pl.BlockSpec((4, C_HALF), lambda b, hp, i: (0, 0)),
            pl.BlockSpec((HEAD_DIM, 1), lambda b, hp, i: (0, 0)),
        ],
        out_specs=pl.BlockSpec((blk, LANES), lambda b, hp, i: (b * n_q + i, hp)),
        out_shape=jax.ShapeDtypeStruct((t, width), BF16),
        scratch_shapes=[pltpu.VMEM((2 * n_heads, V_ROWS, blk), F32)],
        compiler_params=_params("parallel", "parallel", "arbitrary"),
        name="diff",
    )(qc, kc, vtc, lam_rows, subln.reshape(HEAD_DIM, 1))


def _merge_kernel(x_ref, ya_ref, yb_ref, yc_ref, ga_ref, gb_ref, gc_ref, wa_ref, wb_ref, wc_ref, wo_ref, o_ref):
    merged = (jax.nn.sigmoid(ga_ref[...]) * _dot(ya_ref[...], wa_ref[...])
              + jax.nn.sigmoid(gb_ref[...]) * _dot(yb_ref[...], wb_ref[...])
              + jax.nn.sigmoid(gc_ref[...]) * _dot(yc_ref[...], wc_ref[...]))
    o_ref[...] = x_ref[...] + _dot(merged.astype(BF16), wo_ref[...])


def _merge(x2, ya, yb, yc, gates, w_pa, w_pb, w_pc, w_out, *, tm=512):
    t, d = x2.shape
    rows = lambda i: (i, 0)
    fixed = lambda i: (0, 0)
    return pl.pallas_call(
        _merge_kernel,
        grid=(t // tm,),
        in_specs=[
            pl.BlockSpec((tm, d), rows),
            pl.BlockSpec((tm, ya.shape[1]), rows),
            pl.BlockSpec((tm, yb.shape[1]), rows),
            pl.BlockSpec((tm, yc.shape[1]), rows),
            pl.BlockSpec((tm, d), lambda i: (i, 0)),
            pl.BlockSpec((tm, d), lambda i: (i, 1)),
            pl.BlockSpec((tm, d), lambda i: (i, 2)),
            pl.BlockSpec(w_pa.shape, fixed),
            pl.BlockSpec(w_pb.shape, fixed),
            pl.BlockSpec(w_pc.shape, fixed),
            pl.BlockSpec(w_out.shape, fixed),
        ],
        out_specs=pl.BlockSpec((tm, d), rows),
        out_shape=jax.ShapeDtypeStruct((t, d), F32),
        compiler_params=_params("parallel"),
        name="merge",
    )(x2, ya, yb, yc, gates, gates, gates, w_pa, w_pb, w_pc, w_out)


def _convglu_kernel(x_ref, xh_ref, g_ref, wug_ref, wuv_ref, cwg_ref, cwv_ref, cbg_ref, cbv_ref, wd_ref, o_ref,
                    h_scr, ug_scr, uv_scr, acc_scr, *, tm, tiles_per_seq):
    i = pl.program_id(0)
    f = pl.program_id(1)
    halo = CONV_HALO

    def normed(x):
        ms = jnp.mean(x * x, axis=-1, keepdims=True)
        return x * lax.rsqrt(ms + EPS) * g_ref[...]

    @pl.when(f == 0)
    def _():
        keep = jnp.where(i % tiles_per_seq == 0, 0.0, 1.0)
        h_scr[0:halo, :] = (normed(xh_ref[...]) * keep).astype(BF16)
        h_scr[halo:, :] = normed(x_ref[...]).astype(BF16)
        acc_scr[...] = jnp.zeros_like(acc_scr)

    h = h_scr[...]
    ug_scr[...] = _dot(h, wug_ref[...])
    uv_scr[...] = _dot(h, wuv_ref[...])

    def conv(u_scr, cw_ref, cb_ref):
        y = cb_ref[...]
        for j in range(CONV_W):
            y = y + u_scr[pl.ds(halo - (CONV_W - 1) + j, tm), :] * cw_ref[j:j + 1, :]
        return y

    gate_u = conv(ug_scr, cwg_ref, cbg_ref)
    val_u = conv(uv_scr, cwv_ref, cbv_ref)
    act = (gate_u * jax.nn.sigmoid(gate_u)) * val_u
    acc_scr[...] += _dot(act.astype(BF16), wd_ref[...])

    @pl.when(f == pl.num_programs(1) - 1)
    def _():
        o_ref[...] = x_ref[...] + acc_scr[...]


def _convglu(x2, gain, w_up, conv_w, conv_b, w_down, *, seq, tm=512, tf=256):
    t, d = x2.shape
    ff = w_down.shape[0]
    nf = ff // tf
    halo = CONV_HALO
    tiles_per_seq = seq // tm
    halo_blocks = tm // halo
    return pl.pallas_call(
        functools.partial(_convglu_kernel, tm=tm, tiles_per_seq=tiles_per_seq),
        grid=(t // tm, nf),
        in_specs=[
            pl.BlockSpec((tm, d), lambda i, f: (i, 0)),
            pl.BlockSpec((halo, d), lambda i, f: (jnp.maximum(i * halo_blocks - 1, 0), 0)),
            pl.BlockSpec((1, d), lambda i, f: (0, 0)),
            pl.BlockSpec((d, tf), lambda i, f: (0, f)),
            pl.BlockSpec((d, tf), lambda i, f: (0, nf + f)),
            pl.BlockSpec((CONV_W, tf), lambda i, f: (0, f)),
            pl.BlockSpec((CONV_W, tf), lambda i, f: (0, nf + f)),
            pl.BlockSpec((1, tf), lambda i, f: (0, f)),
            pl.BlockSpec((1, tf), lambda i, f: (0, nf + f)),
            pl.BlockSpec((tf, d), lambda i, f: (f, 0)),
        ],
        out_specs=pl.BlockSpec((tm, d), lambda i, f: (i, 0)),
        out_shape=jax.ShapeDtypeStruct((t, d), F32),
        scratch_shapes=[
            pltpu.VMEM((tm + halo, d), BF16),
            pltpu.VMEM((tm + halo, tf), F32),
            pltpu.VMEM((tm + halo, tf), F32),
            pltpu.VMEM((tm, d), F32),
        ],
        compiler_params=_params("parallel", "arbitrary"),
        name="convglu",
    )(x2, x2, gain.reshape(1, d), w_up, w_up, conv_w, conv_w, conv_b.reshape(1, -1), conv_b.reshape(1, -1), w_down)


def _rope_tables(seq, dim):
    inv = 1.0 / (ROPE_THETA ** (jnp.arange(0, dim, 2, dtype=F32) / dim))
    ang = jnp.arange(seq, dtype=F32)[:, None] * inv[None, :]
    return jnp.cos(ang), jnp.sin(ang)


def kernel(x, attn_norm, w_in, qn_a, kn_a, qn_b, kn_b, sinks, qn_c, kn_c, lam_q1, lam_k1, lam_q2, lam_k2, subln,
           w_pa, w_pb, w_pc, w_out, mlp_norm, w_up, conv_w, conv_b, w_down):
    batch, seq, d = x.shape
    depth = w_in.shape[0]
    cos64, sin64 = _rope_tables(seq, HEAD_DIM)
    cos32, sin32 = _rope_tables(seq, C_HALF)
    x2 = x.reshape(batch * seq, d)

    sizes = (A_W, A_W, A_W, B_QW, B_KVW, B_KVW, C_W, C_W, C_W, N_BRANCH * d)
    offs = np.concatenate([[0], np.cumsum(sizes)]).tolist()
    o_qa, o_ka, o_va, o_qb, o_kb, o_vb, o_qc, o_kc, o_vc, o_g = offs[:10]

    for i in range(depth):
        lam_init = 0.8 - 0.6 * float(np.exp(-0.3 * i))
        qkv, gates = _inproj(x2, attn_norm[i], w_in[i].astype(BF16), o_g)

        prep = functools.partial(_prep, qkv, seq=seq)
        qa = prep(o_qa, A_W, tm=512, out_dtype=F32, gain=qn_a[i], cos=cos64, sin=sin64)
        ka, kmean = prep(o_ka, A_W, tm=MOBA_BLOCK, out_dtype=BF16, gain=kn_a[i], cos=cos64, sin=sin64,
                         want_mean=True)
        vta = _values_transposed(qkv, o_va, A_W, batch=batch, seq=seq, tk=MOBA_BLOCK)
        qb = prep(o_qb, B_QW, tm=512, out_dtype=BF16, gain=qn_b[i], cos=cos64, sin=sin64, scale=HEAD_DIM ** -0.5)
        kb_a, kb_b = prep(o_kb, B_KVW, tm=512, out_dtype=BF16, gain=kn_b[i], cos=cos64, sin=sin64, want_swap=True)
        vb_a, vb_b = prep(o_vb, B_KVW, tm=512, out_dtype=BF16, want_swap=True)
        qc = prep(o_qc, C_W, tm=512, out_dtype=F32, gain=qn_c[i], cos=cos32, sin=sin32, hd=C_HALF,
                  scale=C_HALF ** -0.5 * LOG2E)
        kc = prep(o_kc, C_W, tm=512, out_dtype=BF16, gain=kn_c[i], cos=cos32, sin=sin32, hd=C_HALF)
        vtc = _values_transposed(qkv, o_vc, C_W, batch=batch, seq=seq, tk=DIFF_BLOCK)

        n_blk = seq // MOBA_BLOCK
        kmean = jnp.pad(kmean.reshape(batch, n_blk, A_W), ((0, 0), (0, LANES - n_blk), (0, 0)))
        ya = _moba(qa, ka, vta, kmean, batch=batch, seq=seq)
        yb = _swa(qb, kb_a, kb_b, vb_a, vb_b, sinks[i], batch=batch, seq=seq)
        lam_rows = jnp.stack([lam_q1[i], lam_k1[i], lam_q2[i], lam_k2[i]]).astype(F32)
        yc = _diff(qc, kc, vtc, lam_rows, subln[i], batch=batch, seq=seq, lam_init=lam_init, blk=DIFF_BLOCK)

        x2 = _merge(x2, ya, yb, yc, gates, w_pa[i].astype(BF16), w_pb[i].astype(BF16), w_pc[i].astype(BF16),
                    w_out[i].astype(BF16))
        x2 = _convglu(x2, mlp_norm[i], w_up[i].astype(BF16), conv_w[i], conv_b[i], w_down[i].astype(BF16), seq=seq)

    return x2.reshape(batch, seq, d)
```

```python
import functools

import numpy as np
import jax
import jax.numpy as jnp
from jax import lax
from jax.experimental import pallas as pl
from jax.experimental.pallas import tpu as pltpu

F32 = jnp.float32
BF16 = jnp.bfloat16
NEG_INF = float("-inf")

LANES = 128
VMEM_LIMIT = 48 * 1024 * 1024

EPS = 1e-6
HEAD_DIM = 64
ROPE_THETA = 10000.0
A_HEADS = 4
MOBA_BLOCK = 256
MOBA_TOPK = 3
B_HEADS = 8
B_KV_HEADS = 2
WINDOW = 128
C_HEADS = 4
C_HALF = HEAD_DIM // 2
N_BRANCH = 3
CONV_W = 3
CONV_HALO = 16
BF16_SUBLANES = 16
V_ROWS = HEAD_DIM + BF16_SUBLANES
DIFF_BLOCK = 512
LOG2E = 1.4426950408889634

A_W = A_HEADS * HEAD_DIM
B_QW = B_HEADS * HEAD_DIM
B_KVW = B_KV_HEADS * HEAD_DIM
C_W = C_HEADS * HEAD_DIM


def _dot(a, b):
    return jnp.dot(a, b, preferred_element_type=F32)


def _split3(a):
    a1 = a.astype(BF16)
    r = a - a1.astype(F32)
    a2 = r.astype(BF16)
    a3 = (r - a2.astype(F32)).astype(BF16)
    return a1, a2, a3


def _params(*semantics):
    return pltpu.CompilerParams(dimension_semantics=semantics, vmem_limit_bytes=VMEM_LIMIT)


def _inproj_kernel(x_ref, g_ref, w_ref, qkv_ref, gates_ref, h_scr, *, n_qkv_tiles):
    j = pl.program_id(1)

    @pl.when(j == 0)
    def _():
        x = x_ref[...]
        ms = jnp.mean(x * x, axis=-1, keepdims=True)
        h_scr[...] = (x * lax.rsqrt(ms + EPS) * g_ref[...]).astype(BF16)

    y = _dot(h_scr[...], w_ref[...])

    @pl.when(j < n_qkv_tiles)
    def _():
        qkv_ref[...] = y

    @pl.when(j >= n_qkv_tiles)
    def _():
        gates_ref[...] = y


def _inproj(x2, gain, w_bf, n_qkv, *, tm=512, tn=768):
    t, d = x2.shape
    n = w_bf.shape[1]
    nq = n_qkv // tn
    return pl.pallas_call(
        functools.partial(_inproj_kernel, n_qkv_tiles=nq),
        grid=(t // tm, n // tn),
        in_specs=[
            pl.BlockSpec((tm, d), lambda i, j: (i, 0)),
            pl.BlockSpec((1, d), lambda i, j: (0, 0)),
            pl.BlockSpec((d, tn), lambda i, j: (0, j)),
        ],
        out_specs=[pl.BlockSpec((tm, tn), lambda i, j: (i, jnp.minimum(j, nq - 1))),
                   pl.BlockSpec((tm, tn), lambda i, j: (i, jnp.maximum(j - nq, 0)))],
        out_shape=[jax.ShapeDtypeStruct((t, n_qkv), F32), jax.ShapeDtypeStruct((t, n - n_qkv), F32)],
        scratch_shapes=[pltpu.VMEM((tm, d), BF16)],
        compiler_params=_params("parallel", "arbitrary"),
        name="inproj",
    )(x2, gain.reshape(1, d), w_bf)


def _prep_kernel(*refs, hd, do_norm, scale, want_mean):
    it = iter(refs)
    y_ref = next(it)
    if do_norm:
        gain_ref, cos_ref, sin_ref, gsum_ref = next(it), next(it), next(it), next(it)
    o_ref = next(it)
    mean_ref = next(it) if want_mean else None

    y = y_ref[...]
    tm, w = y.shape
    if do_norm:
        sq = y * y
        hi = sq.astype(BF16)
        lo = (sq - hi.astype(F32)).astype(BF16)
        ms = (_dot(hi, gsum_ref[...]) + _dot(lo, gsum_ref[...])) * (1.0 / hd)
        yn = y * lax.rsqrt(ms + EPS) * gain_ref[...]
        half = hd // 2
        lane = lax.broadcasted_iota(jnp.int32, (tm, w), 1) % hd
        partner = jnp.where(lane < half, pltpu.roll(yn, w - half, 1), pltpu.roll(yn, half, 1))
        y = yn * cos_ref[...] + partner * sin_ref[...]
    if want_mean:
        mean_ref[...] = (jnp.sum(y, axis=0, keepdims=True) * (1.0 / tm)).reshape(1, 1, w)
    if scale != 1.0:
        y = y * scale
    o_ref[...] = y.astype(o_ref.dtype)


def _group_sum_matrix(w, hd):
    idx = np.arange(w) // hd
    return jnp.asarray((idx[:, None] == idx[None, :]).astype(np.float32), dtype=BF16)


def _prep(proj, col0, ncols, *, seq, tm, out_dtype, gain=None, cos=None, sin=None, hd=HEAD_DIM,
          scale=1.0, want_mean=False):
    t = proj.shape[0]
    w = LANES
    do_norm = gain is not None
    cb0 = col0 // w
    nj = ncols // w
    tiles_per_seq = seq // tm
    in_specs = [pl.BlockSpec((tm, w), lambda i, j: (i, cb0 + j))]
    args = [proj]
    if do_norm:
        in_specs += [
            pl.BlockSpec((1, w), lambda i, j: (0, 0)),
            pl.BlockSpec((tm, w), lambda i, j: (i % tiles_per_seq, 0)),
            pl.BlockSpec((tm, w), lambda i, j: (i % tiles_per_seq, 0)),
            pl.BlockSpec((w, w), lambda i, j: (0, 0)),
        ]
        reps = w // hd
        args += [jnp.tile(gain, reps).reshape(1, w),
                 jnp.tile(jnp.concatenate([cos, cos], axis=-1), (1, reps)),
                 jnp.tile(jnp.concatenate([-sin, sin], axis=-1), (1, reps)),
                 _group_sum_matrix(w, hd)]
    out_shape = [jax.ShapeDtypeStruct((t, ncols), out_dtype)]
    out_specs = [pl.BlockSpec((tm, w), lambda i, j: (i, j))]
    if want_mean:
        out_shape.append(jax.ShapeDtypeStruct((t // tm, 1, ncols), F32))
        out_specs.append(pl.BlockSpec((1, 1, w), lambda i, j: (i, 0, j)))
    kern = functools.partial(_prep_kernel, hd=hd, do_norm=do_norm, scale=scale, want_mean=want_mean)
    outs = pl.pallas_call(
        kern,
        grid=(t // tm, nj),
        in_specs=in_specs,
        out_specs=out_specs,
        out_shape=out_shape,
        compiler_params=_params("parallel", "arbitrary"),
        name="prep",
    )(*args)
    return outs[0] if len(outs) == 1 else outs


def _vt_kernel(y_ref, o_ref):
    yt = y_ref[...].T
    tk = yt.shape[1]
    ones = jnp.ones((V_ROWS - HEAD_DIM, tk), F32)
    for h in range(LANES // HEAD_DIM):
        o_ref[0, h, 0] = jnp.concatenate([yt[h * HEAD_DIM:(h + 1) * HEAD_DIM], ones], axis=0).astype(o_ref.dtype)


def _values_transposed(proj, col0, ncols, *, batch, seq, tk):
    n_heads = LANES // HEAD_DIM
    cb0 = col0 // LANES
    nkb = seq // tk
    return pl.pallas_call(
        _vt_kernel,
        grid=(batch * nkb, ncols // LANES),
        in_specs=[pl.BlockSpec((tk, LANES), lambda i, j: (i, cb0 + j))],
        out_specs=pl.BlockSpec((1, n_heads, 1, V_ROWS, tk), lambda i, j: (i // nkb, j, i % nkb, 0, 0)),
        out_shape=jax.ShapeDtypeStruct((batch, ncols // HEAD_DIM, nkb, V_ROWS, tk), BF16),
        compiler_params=_params("parallel", "arbitrary"),
        name="vt",
    )(proj)


def _flash_update(acc_ref, stream, s, vts, m):
    m_new = jnp.maximum(m, jnp.max(s, axis=0, keepdims=True))
    alpha = jnp.exp2(m - m_new)
    p = jnp.exp2(s - m_new).astype(BF16)
    tk = vts[0].shape[1]
    pv = _dot(vts[0], p[0:tk])
    for n in range(1, len(vts)):
        pv = pv + _dot(vts[n], p[n * tk:(n + 1) * tk])
    acc_ref[stream] = alpha * acc_ref[stream] + pv
    return m_new


def _pipelined_streams(n_streams, scores, update):
    out = []
    s_next = scores(0)
    for st in range(n_streams):
        s = s_next
        if st + 1 < n_streams:
            s_next = scores(st + 1)
        out.append(update(st, s))
    return tuple(out)


def _moba_kernel(q_ref, k_ref, vt_ref, km_ref, o_ref, acc_scr, sel_scr, *, tq):
    qi = pl.program_id(1)
    n_heads = q_ref.shape[1] // HEAD_DIM
    per_block = LANES // HEAD_DIM
    qt = q_ref[...].T
    feat = lax.broadcasted_iota(jnp.int32, (LANES, tq), 0)
    blk_f = feat.astype(F32)
    krow = lax.broadcasted_iota(jnp.int32, (tq, tq), 0)
    qcol = lax.broadcasted_iota(jnp.int32, (tq, tq), 1)

    def lane_block(x, h):
        b = h // per_block
        return x[:, b * LANES:(b + 1) * LANES]

    qts = []
    for h in range(n_heads):
        b, hh = divmod(h, per_block)
        if hh == 0:
            km_parts = _split3(km_ref[0, :, b * LANES:(b + 1) * LANES])
        qh = jnp.where((feat >= hh * HEAD_DIM) & (feat < (hh + 1) * HEAD_DIM), qt[b * LANES:(b + 1) * LANES], 0.0)

        k1, k2, k3 = km_parts
        q1, q2, q3 = _split3(qh)
        gate = (_dot(k1, q1) + _dot(k2, q1) + _dot(k1, q2) + _dot(k3, q1) + _dot(k2, q2) + _dot(k1, q3))
        g = jnp.where(feat < qi, gate, NEG_INF)
        sel = jnp.zeros((LANES, tq), F32)
        for _ in range(MOBA_TOPK):
            mx = jnp.max(g, axis=0, keepdims=True)
            first = jnp.min(jnp.where((g == mx) & (mx > NEG_INF), blk_f, float(LANES)), axis=0, keepdims=True)
            pick = blk_f == first
            sel = jnp.where(pick, 1.0, sel)
            g = jnp.where(pick, NEG_INF, g)
        sel_scr[h] = sel
        qts.append((qh * (HEAD_DIM ** -0.5 * LOG2E)).astype(BF16))

    acc_scr[...] = jnp.zeros_like(acc_scr)

    k_own = k_ref[pl.ds(pl.multiple_of(qi * tq, tq), tq), :]
    ms = _pipelined_streams(
        n_heads,
        lambda h: _dot(lane_block(k_own, h), qts[h]),
        lambda h, s: _flash_update(acc_scr, h, jnp.where(krow <= qcol, s, NEG_INF), [vt_ref[0, h, qi]],
                                   jnp.full((1, tq), NEG_INF, F32)))

    def body(t, ms):
        kj = k_ref[pl.ds(pl.multiple_of(t * 2 * tq, 2 * tq), 2 * tq), :]

        def update(h, s):
            first = sel_scr[h, pl.ds(2 * t, 1), :]
            second = sel_scr[h, pl.ds(2 * t + 1, 1), :]
            s = jnp.concatenate([jnp.where(first > 0.0, s[0:tq], NEG_INF),
                                 jnp.where(second > 0.0, s[tq:2 * tq], NEG_INF)], axis=0)
            return _flash_update(acc_scr, h, s, [vt_ref[0, h, 2 * t], vt_ref[0, h, 2 * t + 1]], ms[h])

        return _pipelined_streams(n_heads, lambda h: _dot(lane_block(kj, h), qts[h]), update)

    lax.fori_loop(0, (qi + 1) // 2, body, ms)

    outs = [acc_scr[h, 0:HEAD_DIM, :] / acc_scr[h, HEAD_DIM:HEAD_DIM + 1, :] for h in range(n_heads)]
    o_ref[...] = jnp.concatenate(outs, axis=0).T.astype(o_ref.dtype)


def _moba(qa, ka, vta, kmean, *, batch, seq):
    tq = MOBA_BLOCK
    t, width = qa.shape
    n_heads = width // HEAD_DIM
    n_q = seq // tq
    return pl.pallas_call(
        functools.partial(_moba_kernel, tq=tq),
        grid=(batch, n_q),
        in_specs=[
            pl.BlockSpec((tq, width), lambda b, i: (b * n_q + i, 0)),
            pl.BlockSpec((seq, width), lambda b, i: (b, 0)),
            pl.BlockSpec((1, n_heads, n_q, V_ROWS, tq), lambda b, i: (b, 0, 0, 0, 0)),
            pl.BlockSpec((1, LANES, width), lambda b, i: (b, 0, 0)),
        ],
        out_specs=pl.BlockSpec((tq, width), lambda b, i: (b * n_q + i, 0)),
        out_shape=jax.ShapeDtypeStruct((t, width), BF16),
        scratch_shapes=[pltpu.VMEM((n_heads, V_ROWS, tq), F32), pltpu.VMEM((n_heads, LANES, tq), F32)],
        compiler_params=_params("parallel", "arbitrary"),
        name="moba",
    )(qa, ka, vta, kmean)


def _swa_kernel(q_ref, kp_ref, kc_ref, vtp_ref, vtc_ref, sink_ref, o_ref):
    i = pl.program_id(1)
    w = WINDOW
    group = B_HEADS // B_KV_HEADS
    qt = q_ref[...].T
    kband = jnp.concatenate([kp_ref[...], kc_ref[...]], axis=0)
    krow = lax.broadcasted_iota(jnp.int32, (2 * w, w), 0)
    qcol = lax.broadcasted_iota(jnp.int32, (2 * w, w), 1)
    rel = qcol + w - krow
    ok = (rel >= 0) & (rel < w) & ((krow >= w) | (i > 0))
    ok = jnp.concatenate([ok] * group, axis=1)
    zeros = jnp.zeros((HEAD_DIM, w), F32)

    outs = []
    for kv in range(B_KV_HEADS):
        cols = []
        for g in range(group):
            h = kv * group + g
            qh = qt[h * HEAD_DIM:(h + 1) * HEAD_DIM]
            cols.append(jnp.concatenate([qh if n == kv else zeros for n in range(B_KV_HEADS)], axis=0))
        qg = jnp.concatenate(cols, axis=1).astype(BF16)
        s = jnp.where(ok, _dot(kband, qg), NEG_INF)
        sink = sink_ref[:, kv * group * w:(kv + 1) * group * w] * LOG2E
        m = jnp.maximum(jnp.max(s, axis=0, keepdims=True), sink)
        p = jnp.exp2(s - m).astype(BF16)
        pv = _dot(vtp_ref[0, kv, 0], p[0:w]) + _dot(vtc_ref[0, kv, 0], p[w:2 * w])
        o = pv[0:HEAD_DIM] / (pv[HEAD_DIM:HEAD_DIM + 1] + jnp.exp2(sink - m))
        outs += [o[:, g * w:(g + 1) * w] for g in range(group)]
    o_ref[...] = jnp.concatenate(outs, axis=0).T.astype(o_ref.dtype)


def _swa(qb, kb, vtb, sinks, *, batch, seq):
    t, qw = qb.shape
    w = WINDOW
    nb = seq // w
    prev = lambda b, i: (b * nb + jnp.maximum(i - 1, 0), 0)
    cur = lambda b, i: (b * nb + i, 0)
    vt_block = (1, B_KV_HEADS, 1, V_ROWS, w)
    sink_row = jnp.repeat(sinks.astype(F32), w).reshape(1, B_HEADS * w)
    return pl.pallas_call(
        _swa_kernel,
        grid=(batch, nb),
        in_specs=[pl.BlockSpec((w, qw), cur),
                  pl.BlockSpec((w, LANES), prev),
                  pl.BlockSpec((w, LANES), cur),
                  pl.BlockSpec(vt_block, lambda b, i: (b, 0, jnp.maximum(i - 1, 0), 0, 0)),
                  pl.BlockSpec(vt_block, lambda b, i: (b, 0, i, 0, 0)),
                  pl.BlockSpec((1, B_HEADS * w), lambda b, i: (0, 0))],
        out_specs=pl.BlockSpec((w, qw), cur),
        out_shape=jax.ShapeDtypeStruct((t, qw), BF16),
        compiler_params=_params("parallel", "arbitrary"),
        name="swa",
    )(qb, kb, kb, vtb, vtb, sink_row)


def _diff_kernel(q_ref, k_ref, vt_ref, lam_ref, g_ref, o_ref, acc_scr, *, blk, lam_init):
    qi = pl.program_id(2)
    n_heads = LANES // HEAD_DIM
    qt = q_ref[...].T
    feat = lax.broadcasted_iota(jnp.int32, (LANES, blk), 0)
    krow = lax.broadcasted_iota(jnp.int32, (blk, blk), 0)
    qcol = lax.broadcasted_iota(jnp.int32, (blk, blk), 1)

    lq = lam_ref[...]
    lam = (jnp.exp(jnp.sum(lq[0:1] * lq[1:2], axis=1, keepdims=True))
           - jnp.exp(jnp.sum(lq[2:3] * lq[3:4], axis=1, keepdims=True)) + lam_init)

    n_streams = 2 * n_heads
    qts = [jnp.where((feat >= st * C_HALF) & (feat < (st + 1) * C_HALF), qt, 0.0).astype(BF16)
           for st in range(n_streams)]
    acc_scr[...] = jnp.zeros_like(acc_scr)

    def body(j, ms):
        kj = k_ref[pl.ds(pl.multiple_of(j * blk, blk), blk), :]
        return _pipelined_streams(
            n_streams,
            lambda st: _dot(kj, qts[st]),
            lambda st, s: _flash_update(acc_scr, st, s, [vt_ref[0, st // 2, j]], ms[st]))

    ms = lax.fori_loop(0, qi, body, tuple(jnp.full((1, blk), NEG_INF, F32) for _ in range(n_streams)))
    k_own = k_ref[pl.ds(pl.multiple_of(qi * blk, blk), blk), :]
    _pipelined_streams(
        n_streams,
        lambda st: _dot(k_own, qts[st]),
        lambda st, s: _flash_update(acc_scr, st, jnp.where(krow <= qcol, s, NEG_INF), [vt_ref[0, st // 2, qi]],
                                    ms[st]))

    outs = []
    for h in range(n_heads):
        maps = [acc_scr[2 * h + c, 0:HEAD_DIM, :] / acc_scr[2 * h + c, HEAD_DIM:HEAD_DIM + 1, :] for c in range(2)]
        o = maps[0] - lam * maps[1]
        ms_o = jnp.mean(o * o, axis=0, keepdims=True)
        outs.append((o * lax.rsqrt(ms_o + EPS) * g_ref[...]) * (1.0 - lam_init))
    o_ref[...] = jnp.concatenate(outs, axis=0).T.astype(o_ref.dtype)


def _diff(qc, kc, vtc, lam_rows, subln, *, batch, seq, lam_init, blk):
    t, width = qc.shape
    n_pairs = width // LANES
    n_heads = LANES // HEAD_DIM
    n_q = seq // blk
    return pl.pallas_call(
        functools.partial(_diff_kernel, blk=blk, lam_init=lam_init),
        grid=(batch, n_pairs, n_q),
        in_specs=[
            pl.BlockSpec((blk, LANES), lambda b, hp, i: (b * n_q + i, hp)),
            pl.BlockSpec((seq, LANES), lambda b, hp, i: (b, hp)),
            pl.BlockSpec((1, n_heads, n_q, V_ROWS, blk), lambda b, hp, i: (b, hp, 0, 0, 0)),
            pl.BlockSpec((4, C_HALF), lambda b, hp, i: (0, 0)),
            pl.BlockSpec((HEAD_DIM, 1), lambda b, hp, i: (0, 0)),
        ],
        out_specs=pl.BlockSpec((blk, LANES), lambda b, hp, i: (b * n_q + i, hp)),
        out_shape=jax.ShapeDtypeStruct((t, width), BF16),
        scratch_shapes=[pltpu.VMEM((2 * n_heads, V_ROWS, blk), F32)],
        compiler_params=_params("parallel", "parallel", "arbitrary"),
        name="diff",
    )(qc, kc, vtc, lam_rows, subln.reshape(HEAD_DIM, 1))


def _merge_kernel(x_ref, ya_ref, yb_ref, yc_ref, ga_ref, gb_ref, gc_ref, wa_ref, wb_ref, wc_ref, wo_ref, o_ref):
    merged = (jax.nn.sigmoid(ga_ref[...]) * _dot(ya_ref[...], wa_ref[...])
              + jax.nn.sigmoid(gb_ref[...]) * _dot(yb_ref[...], wb_ref[...])
              + jax.nn.sigmoid(gc_ref[...]) * _dot(yc_ref[...], wc_ref[...]))
    o_ref[...] = x_ref[...] + _dot(merged.astype(BF16), wo_ref[...])


def _merge(x2, ya, yb, yc, gates, w_pa, w_pb, w_pc, w_out, *, tm=512):
    t, d = x2.shape
    rows = lambda i: (i, 0)
    fixed = lambda i: (0, 0)
    return pl.pallas_call(
        _merge_kernel,
        grid=(t // tm,),
        in_specs=[
            pl.BlockSpec((tm, d), rows),
            pl.BlockSpec((tm, ya.shape[1]), rows),
            pl.BlockSpec((tm, yb.shape[1]), rows),
            pl.BlockSpec((tm, yc.shape[1]), rows),
            pl.BlockSpec((tm, d), lambda i: (i, 0)),
            pl.BlockSpec((tm, d), lambda i: (i, 1)),
            pl.BlockSpec((tm, d), lambda i: (i, 2)),
            pl.BlockSpec(w_pa.shape, fixed),
            pl.BlockSpec(w_pb.shape, fixed),
            pl.BlockSpec(w_pc.shape, fixed),
            pl.BlockSpec(w_out.shape, fixed),
        ],
        out_specs=pl.BlockSpec((tm, d), rows),
        out_shape=jax.ShapeDtypeStruct((t, d), F32),
        compiler_params=_params("parallel"),
        name="merge",
    )(x2, ya, yb, yc, gates, gates, gates, w_pa, w_pb, w_pc, w_out)


def _convglu_kernel(x_ref, xh_ref, g_ref, wug_ref, wuv_ref, cwg_ref, cwv_ref, cbg_ref, cbv_ref, wd_ref, o_ref,
                    h_scr, ug_scr, uv_scr, acc_scr, *, tm, tiles_per_seq):
    i = pl.program_id(0)
    f = pl.program_id(1)
    halo = CONV_HALO

    def normed(x):
        ms = jnp.mean(x * x, axis=-1, keepdims=True)
        return x * lax.rsqrt(ms + EPS) * g_ref[...]

    @pl.when(f == 0)
    def _():
        keep = jnp.where(i % tiles_per_seq == 0, 0.0, 1.0)
        h_scr[0:halo, :] = (normed(xh_ref[...]) * keep).astype(BF16)
        h_scr[halo:, :] = normed(x_ref[...]).astype(BF16)
        acc_scr[...] = jnp.zeros_like(acc_scr)

    h = h_scr[...]
    ug_scr[...] = _dot(h, wug_ref[...])
    uv_scr[...] = _dot(h, wuv_ref[...])

    def conv(u_scr, cw_ref, cb_ref):
        y = cb_ref[...]
        for j in range(CONV_W):
            y = y + u_scr[pl.ds(halo - (CONV_W - 1) + j, tm), :] * cw_ref[j:j + 1, :]
        return y

    gate_u = conv(ug_scr, cwg_ref, cbg_ref)
    val_u = conv(uv_scr, cwv_ref, cbv_ref)
    act = (gate_u * jax.nn.sigmoid(gate_u)) * val_u
    acc_scr[...] += _dot(act.astype(BF16), wd_ref[...])

    @pl.when(f == pl.num_programs(1) - 1)
    def _():
        o_ref[...] = x_ref[...] + acc_scr[...]


def _convglu(x2, gain, w_up, conv_w, conv_b, w_down, *, seq, tm=512, tf=256):
    t, d = x2.shape
    ff = w_down.shape[0]
    nf = ff // tf
    halo = CONV_HALO
    tiles_per_seq = seq // tm
    halo_blocks = tm // halo
    return pl.pallas_call(
        functools.partial(_convglu_kernel, tm=tm, tiles_per_seq=tiles_per_seq),
        grid=(t // tm, nf),
        in_specs=[
            pl.BlockSpec((tm, d), lambda i, f: (i, 0)),
            pl.BlockSpec((halo, d), lambda i, f: (jnp.maximum(i * halo_blocks - 1, 0), 0)),
            pl.BlockSpec((1, d), lambda i, f: (0, 0)),
            pl.BlockSpec((d, tf), lambda i, f: (0, f)),
            pl.BlockSpec((d, tf), lambda i, f: (0, nf + f)),
            pl.BlockSpec((CONV_W, tf), lambda i, f: (0, f)),
            pl.BlockSpec((CONV_W, tf), lambda i, f: (0, nf + f)),
            pl.BlockSpec((1, tf), lambda i, f: (0, f)),
            pl.BlockSpec((1, tf), lambda i, f: (0, nf + f)),
            pl.BlockSpec((tf, d), lambda i, f: (f, 0)),
        ],
        out_specs=pl.BlockSpec((tm, d), lambda i, f: (i, 0)),
        out_shape=jax.ShapeDtypeStruct((t, d), F32),
        scratch_shapes=[
            pltpu.VMEM((tm + halo, d), BF16),
            pltpu.VMEM((tm + halo, tf), F32),
            pltpu.VMEM((tm + halo, tf), F32),
            pltpu.VMEM((tm, d), F32),
        ],
        compiler_params=_params("parallel", "arbitrary"),
        name="convglu",
    )(x2, x2, gain.reshape(1, d), w_up, w_up, conv_w, conv_w, conv_b.reshape(1, -1), conv_b.reshape(1, -1), w_down)


def _rope_tables(seq, dim):
    inv = 1.0 / (ROPE_THETA ** (jnp.arange(0, dim, 2, dtype=F32) / dim))
    ang = jnp.arange(seq, dtype=F32)[:, None] * inv[None, :]
    return jnp.cos(ang), jnp.sin(ang)


def kernel(x, attn_norm, w_in, qn_a, kn_a, qn_b, kn_b, sinks, qn_c, kn_c, lam_q1, lam_k1, lam_q2, lam_k2, subln,
           w_pa, w_pb, w_pc, w_out, mlp_norm, w_up, conv_w, conv_b, w_down):
    batch, seq, d = x.shape
    depth = w_in.shape[0]
    cos64, sin64 = _rope_tables(seq, HEAD_DIM)
    cos32, sin32 = _rope_tables(seq, C_HALF)
    x2 = x.reshape(batch * seq, d)

    sizes = (A_W, A_W, A_W, B_QW, B_KVW, B_KVW, C_W, C_W, C_W, N_BRANCH * d)
    offs = np.concatenate([[0], np.cumsum(sizes)]).tolist()
    o_qa, o_ka, o_va, o_qb, o_kb, o_vb, o_qc, o_kc, o_vc, o_g = offs[:10]

    for i in range(depth):
        lam_init = 0.8 - 0.6 * float(np.exp(-0.3 * i))
        qkv, gates = _inproj(x2, attn_norm[i], w_in[i].astype(BF16), o_g)

        prep = functools.partial(_prep, qkv, seq=seq)
        qa = prep(o_qa, A_W, tm=512, out_dtype=F32, gain=qn_a[i], cos=cos64, sin=sin64)
        ka, kmean = prep(o_ka, A_W, tm=MOBA_BLOCK, out_dtype=BF16, gain=kn_a[i], cos=cos64, sin=sin64,
                         want_mean=True)
        vta = _values_transposed(qkv, o_va, A_W, batch=batch, seq=seq, tk=MOBA_BLOCK)
        qb = prep(o_qb, B_QW, tm=512, out_dtype=F32, gain=qn_b[i], cos=cos64, sin=sin64,
                  scale=HEAD_DIM ** -0.5 * LOG2E)
        kb = prep(o_kb, B_KVW, tm=512, out_dtype=BF16, gain=kn_b[i], cos=cos64, sin=sin64)
        vtb = _values_transposed(qkv, o_vb, B_KVW, batch=batch, seq=seq, tk=WINDOW)
        qc = prep(o_qc, C_W, tm=512, out_dtype=F32, gain=qn_c[i], cos=cos32, sin=sin32, hd=C_HALF,
                  scale=C_HALF ** -0.5 * LOG2E)
        kc = prep(o_kc, C_W, tm=512, out_dtype=BF16, gain=kn_c[i], cos=cos32, sin=sin32, hd=C_HALF)
        vtc = _values_transposed(qkv, o_vc, C_W, batch=batch, seq=seq, tk=DIFF_BLOCK)

        n_blk = seq // MOBA_BLOCK
        kmean = jnp.pad(kmean.reshape(batch, n_blk, A_W), ((0, 0), (0, LANES - n_blk), (0, 0)))
        ya = _moba(qa, ka, vta, kmean, batch=batch, seq=seq)
        yb = _swa(qb, kb, vtb, sinks[i], batch=batch, seq=seq)
        lam_rows = jnp.stack([lam_q1[i], lam_k1[i], lam_q2[i], lam_k2[i]]).astype(F32)
        yc = _diff(qc, kc, vtc, lam_rows, subln[i], batch=batch, seq=seq, lam_init=lam_init, blk=DIFF_BLOCK)

        x2 = _merge(x2, ya, yb, yc, gates, w_pa[i].astype(BF16), w_pb[i].astype(BF16), w_pc[i].astype(BF16),
                    w_out[i].astype(BF16))
        x2 = _convglu(x2, mlp_norm[i], w_up[i].astype(BF16), conv_w[i], conv_b[i], w_down[i].astype(BF16), seq=seq)

    return x2.reshape(batch, seq, d)
```

```python
import functools

import numpy as np
import jax
import jax.numpy as jnp
from jax import lax
from jax.experimental import pallas as pl
from jax.experimental.pallas import tpu as pltpu

F32 = jnp.float32
BF16 = jnp.bfloat16
NEG_INF = float("-inf")

LANES = 128
VMEM_LIMIT = 48 * 1024 * 1024

EPS = 1e-6
HEAD_DIM = 64
ROPE_THETA = 10000.0
A_HEADS = 4
MOBA_BLOCK = 256
MOBA_TOPK = 3
B_HEADS = 8
B_KV_HEADS = 2
WINDOW = 128
C_HEADS = 4
C_HALF = HEAD_DIM // 2
N_BRANCH = 3
CONV_W = 3
CONV_HALO = 16
BF16_SUBLANES = 16
V_ROWS = HEAD_DIM + BF16_SUBLANES
DIFF_BLOCK = 512
LOG2E = 1.4426950408889634

A_W = A_HEADS * HEAD_DIM
B_QW = B_HEADS * HEAD_DIM
B_KVW = B_KV_HEADS * HEAD_DIM
C_W = C_HEADS * HEAD_DIM


def _dot(a, b):
    return jnp.dot(a, b, preferred_element_type=F32)


def _split3(a):
    a1 = a.astype(BF16)
    r = a - a1.astype(F32)
    a2 = r.astype(BF16)
    a3 = (r - a2.astype(F32)).astype(BF16)
    return a1, a2, a3


def _resident(shape):
    return pl.BlockSpec(shape, lambda *_: (0,) * len(shape), pipeline_mode=pl.Buffered(1))


def _params(*semantics):
    return pltpu.CompilerParams(dimension_semantics=semantics, vmem_limit_bytes=VMEM_LIMIT)


def _group_sum_matrix(w, hd):
    idx = np.arange(w) // hd
    return jnp.asarray((idx[:, None] == idx[None, :]).astype(np.float32), dtype=BF16)


MM_COLS = 256


def _inproj_kernel(x_ref, g_ref, w_ref, gains_ref, cos64_ref, sin64_ref, cos32_ref, sin32_ref, gs64_ref, gs32_ref,
                   qa_ref, ka_ref, km_ref, vta_ref, qb_ref, kb_ref, vtb_ref, qc_ref, kc_ref, vtc_ref, gates_ref,
                   h_scr, *, blocks, n_qkv):
    x = x_ref[...]
    tm = x.shape[0]
    ms = jnp.mean(x * x, axis=-1, keepdims=True)
    h_scr[...] = (x * lax.rsqrt(ms + EPS) * g_ref[...]).astype(BF16)
    lane = lax.broadcasted_iota(jnp.int32, (tm, LANES), 1)

    def norm_rope(y, gain_row, hd, scale):
        cos_ref, sin_ref, gs_ref = ((cos64_ref, sin64_ref, gs64_ref) if hd == HEAD_DIM
                                    else (cos32_ref, sin32_ref, gs32_ref))
        sq = y * y
        hi = sq.astype(BF16)
        lo = (sq - hi.astype(F32)).astype(BF16)
        msq = (_dot(hi, gs_ref[...]) + _dot(lo, gs_ref[...])) * (1.0 / hd)
        yn = y * lax.rsqrt(msq + EPS) * gains_ref[gain_row:gain_row + 1, :]
        half = hd // 2
        partner = jnp.where(lane % hd < half, pltpu.roll(yn, LANES - half, 1), pltpu.roll(yn, half, 1))
        out = yn * cos_ref[...] + partner * sin_ref[...]
        return out * scale if scale != 1.0 else out

    def store_vt(vt_ref, y, lb):
        tk = vt_ref.shape[-1]
        yt = y.T
        ones = jnp.ones((V_ROWS - HEAD_DIM, tk), F32)
        for hh in range(LANES // HEAD_DIM):
            for n in range(tm // tk):
                blk = yt[hh * HEAD_DIM:(hh + 1) * HEAD_DIM, n * tk:(n + 1) * tk]
                vt_ref[0, lb * (LANES // HEAD_DIM) + hh, n] = jnp.concatenate([blk, ones], axis=0).astype(BF16)

    q_scale = HEAD_DIM ** -0.5 * LOG2E
    c_scale = C_HALF ** -0.5 * LOG2E

    def epilogue(kind, lb, y):
        cols = slice(lb * LANES, (lb + 1) * LANES)
        if kind == "qa":
            qa_ref[:, cols] = norm_rope(y, 0, HEAD_DIM, 1.0)
        elif kind == "ka":
            yk = norm_rope(y, 1, HEAD_DIM, 1.0)
            ka_ref[:, cols] = yk.astype(BF16)
            for n in range(tm // MOBA_BLOCK):
                rows = yk[n * MOBA_BLOCK:(n + 1) * MOBA_BLOCK]
                km_ref[n, :, cols] = jnp.sum(rows, axis=0, keepdims=True) * (1.0 / MOBA_BLOCK)
        elif kind == "va":
            store_vt(vta_ref, y, lb)
        elif kind == "qb":
            qb_ref[:, cols] = norm_rope(y, 2, HEAD_DIM, q_scale)
        elif kind == "kb":
            kb_ref[:, cols] = norm_rope(y, 3, HEAD_DIM, 1.0).astype(BF16)
        elif kind == "vb":
            store_vt(vtb_ref, y, lb)
        elif kind == "qc":
            qc_ref[:, cols] = norm_rope(y, 4, C_HALF, c_scale)
        elif kind == "kc":
            kc_ref[:, cols] = norm_rope(y, 5, C_HALF, 1.0).astype(BF16)
        elif kind == "vc":
            store_vt(vtc_ref, y, lb)

    def chunk(c):
        return _dot(h_scr[...], w_ref[:, c * MM_COLS:(c + 1) * MM_COLS])

    n_chunks = w_ref.shape[1] // MM_COLS
    y_next = chunk(0)
    for c in range(n_chunks):
        y = y_next
        if c + 1 < n_chunks:
            y_next = chunk(c + 1)
        col0 = c * MM_COLS
        if col0 < n_qkv:
            for part in range(MM_COLS // LANES):
                kind, lb = blocks[col0 // LANES + part]
                epilogue(kind, lb, y[:, part * LANES:(part + 1) * LANES])
        else:
            gates_ref[:, col0 - n_qkv:col0 - n_qkv + MM_COLS] = y.astype(gates_ref.dtype)


def _inproj(x2, gain, w_bf, qk_gains, tables, *, batch, seq, segments, tm=512):
    t, d = x2.shape
    n = w_bf.shape[1]
    blocks = [(name, lb) for name, width in segments[:-1] for lb in range(width // LANES)]
    n_qkv = len(blocks) * LANES
    width = dict(segments)
    tps = seq // tm
    rows = lambda i: (i, 0)
    pos = lambda i: (i % tps, 0)

    def vt_spec(heads, tk):
        return pl.BlockSpec((1, heads, tm // tk, V_ROWS, tk), lambda i: (i // tps, 0, i % tps, 0, 0))

    def vt_shape(heads, tk):
        return jax.ShapeDtypeStruct((batch, heads, seq // tk, V_ROWS, tk), BF16)

    n_mb = tm // MOBA_BLOCK
    out_specs = [
        pl.BlockSpec((tm, width["qa"]), rows),
        pl.BlockSpec((tm, width["ka"]), rows),
        pl.BlockSpec((n_mb, 1, width["ka"]), lambda i: (i, 0, 0)),
        vt_spec(A_HEADS, MOBA_BLOCK),
        pl.BlockSpec((tm, width["qb"]), rows),
        pl.BlockSpec((tm, width["kb"]), rows),
        vt_spec(B_KV_HEADS, WINDOW),
        pl.BlockSpec((tm, width["qc"]), rows),
        pl.BlockSpec((tm, width["kc"]), rows),
        vt_spec(C_HEADS, DIFF_BLOCK),
        pl.BlockSpec((tm, width["gates"]), rows),
    ]
    out_shape = [
        jax.ShapeDtypeStruct((t, width["qa"]), F32),
        jax.ShapeDtypeStruct((t, width["ka"]), BF16),
        jax.ShapeDtypeStruct((t // MOBA_BLOCK, 1, width["ka"]), F32),
        vt_shape(A_HEADS, MOBA_BLOCK),
        jax.ShapeDtypeStruct((t, width["qb"]), F32),
        jax.ShapeDtypeStruct((t, width["kb"]), BF16),
        vt_shape(B_KV_HEADS, WINDOW),
        jax.ShapeDtypeStruct((t, width["qc"]), F32),
        jax.ShapeDtypeStruct((t, width["kc"]), BF16),
        vt_shape(C_HEADS, DIFF_BLOCK),
        jax.ShapeDtypeStruct((t, width["gates"]), BF16),
    ]
    table_spec = pl.BlockSpec((tm, LANES), pos)
    return pl.pallas_call(
        functools.partial(_inproj_kernel, blocks=blocks, n_qkv=n_qkv),
        grid=(t // tm,),
        in_specs=[pl.BlockSpec((tm, d), rows), _resident((1, d)), _resident((d, n)), _resident(qk_gains.shape),
                  table_spec, table_spec, table_spec, table_spec,
                  _resident((LANES, LANES)), _resident((LANES, LANES))],
        out_specs=out_specs,
        out_shape=out_shape,
        scratch_shapes=[pltpu.VMEM((tm, d), BF16)],
        compiler_params=_params("parallel"),
        name="inproj",
    )(x2, gain.reshape(1, d), w_bf, qk_gains, *tables, _group_sum_matrix(LANES, HEAD_DIM),
      _group_sum_matrix(LANES, C_HALF))


def _flash_update(acc_ref, stream, s, vts, m):
    m_new = jnp.maximum(m, jnp.max(s, axis=0, keepdims=True))
    alpha = jnp.exp2(m - m_new)
    p = jnp.exp2(s - m_new).astype(BF16)
    tk = vts[0].shape[1]
    pv = _dot(vts[0], p[0:tk])
    for n in range(1, len(vts)):
        pv = pv + _dot(vts[n], p[n * tk:(n + 1) * tk])
    acc_ref[stream] = alpha * acc_ref[stream] + pv
    return m_new


def _pipelined_streams(n_streams, scores, update):
    out = []
    s_next = scores(0)
    for st in range(n_streams):
        s = s_next
        if st + 1 < n_streams:
            s_next = scores(st + 1)
        out.append(update(st, s))
    return tuple(out)


def _moba_kernel(q_ref, k_ref, vt_ref, km_ref, o_ref, acc_scr, sel_scr, *, tq):
    qi = pl.program_id(1)
    n_heads = q_ref.shape[1] // HEAD_DIM
    per_block = LANES // HEAD_DIM
    qt = q_ref[...].T
    feat = lax.broadcasted_iota(jnp.int32, (LANES, tq), 0)
    blk_f = feat.astype(F32)
    krow = lax.broadcasted_iota(jnp.int32, (tq, tq), 0)
    qcol = lax.broadcasted_iota(jnp.int32, (tq, tq), 1)

    def lane_block(x, h):
        b = h // per_block
        return x[:, b * LANES:(b + 1) * LANES]

    qts = []
    for h in range(n_heads):
        b, hh = divmod(h, per_block)
        if hh == 0:
            km_parts = _split3(km_ref[0, :, b * LANES:(b + 1) * LANES])
        qh = jnp.where((feat >= hh * HEAD_DIM) & (feat < (hh + 1) * HEAD_DIM), qt[b * LANES:(b + 1) * LANES], 0.0)

        k1, k2, k3 = km_parts
        q1, q2, q3 = _split3(qh)
        gate = (_dot(k1, q1) + _dot(k2, q1) + _dot(k1, q2) + _dot(k3, q1) + _dot(k2, q2) + _dot(k1, q3))
        g = jnp.where(feat < qi, gate, NEG_INF)
        sel = jnp.zeros((LANES, tq), F32)
        for _ in range(MOBA_TOPK):
            mx = jnp.max(g, axis=0, keepdims=True)
            first = jnp.min(jnp.where((g == mx) & (mx > NEG_INF), blk_f, float(LANES)), axis=0, keepdims=True)
            pick = blk_f == first
            sel = jnp.where(pick, 1.0, sel)
            g = jnp.where(pick, NEG_INF, g)
        sel_scr[h] = sel
        qts.append((qh * (HEAD_DIM ** -0.5 * LOG2E)).astype(BF16))

    acc_scr[...] = jnp.zeros_like(acc_scr)

    k_own = k_ref[pl.ds(pl.multiple_of(qi * tq, tq), tq), :]
    ms = _pipelined_streams(
        n_heads,
        lambda h: _dot(lane_block(k_own, h), qts[h]),
        lambda h, s: _flash_update(acc_scr, h, jnp.where(krow <= qcol, s, NEG_INF), [vt_ref[0, h, qi]],
                                   jnp.full((1, tq), NEG_INF, F32)))

    def body(t, ms):
        kj = k_ref[pl.ds(pl.multiple_of(t * 2 * tq, 2 * tq), 2 * tq), :]

        def update(h, s):
            first = sel_scr[h, pl.ds(2 * t, 1), :]
            second = sel_scr[h, pl.ds(2 * t + 1, 1), :]
            s = jnp.concatenate([jnp.where(first > 0.0, s[0:tq], NEG_INF),
                                 jnp.where(second > 0.0, s[tq:2 * tq], NEG_INF)], axis=0)
            return _flash_update(acc_scr, h, s, [vt_ref[0, h, 2 * t], vt_ref[0, h, 2 * t + 1]], ms[h])

        return _pipelined_streams(n_heads, lambda h: _dot(lane_block(kj, h), qts[h]), update)

    lax.fori_loop(0, (qi + 1) // 2, body, ms)

    outs = [acc_scr[h, 0:HEAD_DIM, :] / acc_scr[h, HEAD_DIM:HEAD_DIM + 1, :] for h in range(n_heads)]
    o_ref[...] = jnp.concatenate(outs, axis=0).T.astype(o_ref.dtype)


def _moba(qa, ka, vta, kmean, *, batch, seq):
    tq = MOBA_BLOCK
    t, width = qa.shape
    n_heads = width // HEAD_DIM
    n_q = seq // tq
    return pl.pallas_call(
        functools.partial(_moba_kernel, tq=tq),
        grid=(batch, n_q),
        in_specs=[
            pl.BlockSpec((tq, width), lambda b, i: (b * n_q + i, 0)),
            pl.BlockSpec((seq, width), lambda b, i: (b, 0)),
            pl.BlockSpec((1, n_heads, n_q, V_ROWS, tq), lambda b, i: (b, 0, 0, 0, 0)),
            pl.BlockSpec((1, LANES, width), lambda b, i: (b, 0, 0)),
        ],
        out_specs=pl.BlockSpec((tq, width), lambda b, i: (b * n_q + i, 0)),
        out_shape=jax.ShapeDtypeStruct((t, width), BF16),
        scratch_shapes=[pltpu.VMEM((n_heads, V_ROWS, tq), F32), pltpu.VMEM((n_heads, LANES, tq), F32)],
        compiler_params=_params("parallel", "arbitrary"),
        name="moba",
    )(qa, ka, vta, kmean)


def _swa_kernel(q_ref, kp_ref, kc_ref, vtp_ref, vtc_ref, sink_ref, o_ref):
    i = pl.program_id(1)
    w = WINDOW
    group = B_HEADS // B_KV_HEADS
    qt = q_ref[...].T
    kband = jnp.concatenate([kp_ref[...], kc_ref[...]], axis=0)
    krow = lax.broadcasted_iota(jnp.int32, (2 * w, w), 0)
    qcol = lax.broadcasted_iota(jnp.int32, (2 * w, w), 1)
    rel = qcol + w - krow
    ok = (rel >= 0) & (rel < w) & ((krow >= w) | (i > 0))
    ok = jnp.concatenate([ok] * group, axis=1)
    zeros = jnp.zeros((HEAD_DIM, w), F32)

    outs = []
    for kv in range(B_KV_HEADS):
        cols = []
        for g in range(group):
            h = kv * group + g
            qh = qt[h * HEAD_DIM:(h + 1) * HEAD_DIM]
            cols.append(jnp.concatenate([qh if n == kv else zeros for n in range(B_KV_HEADS)], axis=0))
        qg = jnp.concatenate(cols, axis=1).astype(BF16)
        s = jnp.where(ok, _dot(kband, qg), NEG_INF)
        sink = sink_ref[:, kv * group * w:(kv + 1) * group * w] * LOG2E
        m = jnp.maximum(jnp.max(s, axis=0, keepdims=True), sink)
        p = jnp.exp2(s - m).astype(BF16)
        pv = _dot(vtp_ref[0, kv, 0], p[0:w]) + _dot(vtc_ref[0, kv, 0], p[w:2 * w])
        o = pv[0:HEAD_DIM] / (pv[HEAD_DIM:HEAD_DIM + 1] + jnp.exp2(sink - m))
        outs += [o[:, g * w:(g + 1) * w] for g in range(group)]
    o_ref[...] = jnp.concatenate(outs, axis=0).T.astype(o_ref.dtype)


def _swa(qb, kb, vtb, sinks, *, batch, seq):
    t, qw = qb.shape
    w = WINDOW
    nb = seq // w
    prev = lambda b, i: (b * nb + jnp.maximum(i - 1, 0), 0)
    cur = lambda b, i: (b * nb + i, 0)
    vt_block = (1, B_KV_HEADS, 1, V_ROWS, w)
    sink_row = jnp.repeat(sinks.astype(F32), w).reshape(1, B_HEADS * w)
    return pl.pallas_call(
        _swa_kernel,
        grid=(batch, nb),
        in_specs=[pl.BlockSpec((w, qw), cur),
                  pl.BlockSpec((w, LANES), prev),
                  pl.BlockSpec((w, LANES), cur),
                  pl.BlockSpec(vt_block, lambda b, i: (b, 0, jnp.maximum(i - 1, 0), 0, 0)),
                  pl.BlockSpec(vt_block, lambda b, i: (b, 0, i, 0, 0)),
                  pl.BlockSpec((1, B_HEADS * w), lambda b, i: (0, 0))],
        out_specs=pl.BlockSpec((w, qw), cur),
        out_shape=jax.ShapeDtypeStruct((t, qw), BF16),
        compiler_params=_params("parallel", "arbitrary"),
        name="swa",
    )(qb, kb, kb, vtb, vtb, sink_row)


def _diff_kernel(q_ref, k_ref, vt_ref, lam_ref, g_ref, o_ref, acc_scr, *, blk, lam_init):
    qi = pl.program_id(2)
    n_heads = LANES // HEAD_DIM
    qt = q_ref[...].T
    feat = lax.broadcasted_iota(jnp.int32, (LANES, blk), 0)
    krow = lax.broadcasted_iota(jnp.int32, (blk, blk), 0)
    qcol = lax.broadcasted_iota(jnp.int32, (blk, blk), 1)

    lq = lam_ref[...]
    lam = (jnp.exp(jnp.sum(lq[0:1] * lq[1:2], axis=1, keepdims=True))
           - jnp.exp(jnp.sum(lq[2:3] * lq[3:4], axis=1, keepdims=True)) + lam_init)

    n_streams = 2 * n_heads
    qts = [jnp.where((feat >= st * C_HALF) & (feat < (st + 1) * C_HALF), qt, 0.0).astype(BF16)
           for st in range(n_streams)]
    acc_scr[...] = jnp.zeros_like(acc_scr)

    def body(j, ms):
        kj = k_ref[pl.ds(pl.multiple_of(j * blk, blk), blk), :]
        return _pipelined_streams(
            n_streams,
            lambda st: _dot(kj, qts[st]),
            lambda st, s: _flash_update(acc_scr, st, s, [vt_ref[0, st // 2, j]], ms[st]))

    ms = lax.fori_loop(0, qi, body, tuple(jnp.full((1, blk), NEG_INF, F32) for _ in range(n_streams)))
    k_own = k_ref[pl.ds(pl.multiple_of(qi * blk, blk), blk), :]
    _pipelined_streams(
        n_streams,
        lambda st: _dot(k_own, qts[st]),
        lambda st, s: _flash_update(acc_scr, st, jnp.where(krow <= qcol, s, NEG_INF), [vt_ref[0, st // 2, qi]],
                                    ms[st]))

    outs = []
    for h in range(n_heads):
        maps = [acc_scr[2 * h + c, 0:HEAD_DIM, :] / acc_scr[2 * h + c, HEAD_DIM:HEAD_DIM + 1, :] for c in range(2)]
        o = maps[0] - lam * maps[1]
        ms_o = jnp.mean(o * o, axis=0, keepdims=True)
        outs.append((o * lax.rsqrt(ms_o + EPS) * g_ref[...]) * (1.0 - lam_init))
    o_ref[...] = jnp.concatenate(outs, axis=0).T.astype(o_ref.dtype)


def _diff(qc, kc, vtc, lam_rows, subln, *, batch, seq, lam_init, blk):
    t, width = qc.shape
    n_pairs = width // LANES
    n_heads = LANES // HEAD_DIM
    n_q = seq // blk
    return pl.pallas_call(
        functools.partial(_diff_kernel, blk=blk, lam_init=lam_init),
        grid=(batch, n_pairs, n_q),
        in_specs=[
            pl.BlockSpec((blk, LANES), lambda b, hp, i: (b * n_q + i, hp)),
            pl.BlockSpec((seq, LANES), lambda b, hp, i: (b, hp)),
            pl.BlockSpec((1, n_heads, n_q, V_ROWS, blk), lambda b, hp, i: (b, hp, 0, 0, 0)),
            pl.BlockSpec((4, C_HALF), lambda b, hp, i: (0, 0)),
            pl.BlockSpec((HEAD_DIM, 1), lambda b, hp, i: (0, 0)),
        ],
        out_specs=pl.BlockSpec((blk, LANES), lambda b, hp, i: (b * n_q + i, hp)),
        out_shape=jax.ShapeDtypeStruct((t, width), BF16),
        scratch_shapes=[pltpu.VMEM((2 * n_heads, V_ROWS, blk), F32)],
        compiler_params=_params("parallel", "parallel", "arbitrary"),
        name="diff",
    )(qc, kc, vtc, lam_rows, subln.reshape(HEAD_DIM, 1))


def _merge_kernel(x_ref, ya_ref, yb_ref, yc_ref, ga_ref, gb_ref, gc_ref, wa_ref, wb_ref, wc_ref, wo_ref, o_ref):
    merged = (jax.nn.sigmoid(ga_ref[...].astype(F32)) * _dot(ya_ref[...], wa_ref[...])
              + jax.nn.sigmoid(gb_ref[...].astype(F32)) * _dot(yb_ref[...], wb_ref[...])
              + jax.nn.sigmoid(gc_ref[...].astype(F32)) * _dot(yc_ref[...], wc_ref[...]))
    o_ref[...] = x_ref[...] + _dot(merged.astype(BF16), wo_ref[...])


def _merge(x2, ya, yb, yc, gates, w_pa, w_pb, w_pc, w_out, *, tm=512):
    t, d = x2.shape
    rows = lambda i: (i, 0)
    return pl.pallas_call(
        _merge_kernel,
        grid=(t // tm,),
        in_specs=[
            pl.BlockSpec((tm, d), rows),
            pl.BlockSpec((tm, ya.shape[1]), rows),
            pl.BlockSpec((tm, yb.shape[1]), rows),
            pl.BlockSpec((tm, yc.shape[1]), rows),
            pl.BlockSpec((tm, d), lambda i: (i, 0)),
            pl.BlockSpec((tm, d), lambda i: (i, 1)),
            pl.BlockSpec((tm, d), lambda i: (i, 2)),
            _resident(w_pa.shape),
            _resident(w_pb.shape),
            _resident(w_pc.shape),
            _resident(w_out.shape),
        ],
        out_specs=pl.BlockSpec((tm, d), rows),
        out_shape=jax.ShapeDtypeStruct((t, d), F32),
        compiler_params=_params("parallel"),
        name="merge",
    )(x2, ya, yb, yc, gates, gates, gates, w_pa, w_pb, w_pc, w_out)


def _convglu_kernel(x_ref, xh_ref, g_ref, wu_ref, cw_ref, cb_ref, wd_ref, o_ref, h_scr, u_scr, act_scr,
                    *, tm, tf, tiles_per_seq):
    i = pl.program_id(0)
    halo = CONV_HALO
    ff = wd_ref.shape[0]
    n_chunks = ff // tf

    def normed(x):
        ms = jnp.mean(x * x, axis=-1, keepdims=True)
        return x * lax.rsqrt(ms + EPS) * g_ref[...]

    keep = jnp.where(i % tiles_per_seq == 0, 0.0, 1.0)
    h_scr[0:halo, :] = (normed(xh_ref[...]) * keep).astype(BF16)
    h_scr[halo:, :] = normed(x_ref[...]).astype(BF16)

    def up(c):
        h = h_scr[...]
        u_scr[c % 2, 0] = _dot(h, wu_ref[:, c * tf:(c + 1) * tf])
        u_scr[c % 2, 1] = _dot(h, wu_ref[:, ff + c * tf:ff + (c + 1) * tf])

    def conv(c, half):
        col0 = half * ff + c * tf
        y = cb_ref[:, col0:col0 + tf]
        for j in range(CONV_W):
            y = y + u_scr[c % 2, half, pl.ds(halo - (CONV_W - 1) + j, tm), :] * cw_ref[j:j + 1, col0:col0 + tf]
        return y

    up(0)
    for c in range(n_chunks):
        if c + 1 < n_chunks:
            up(c + 1)
        gate_u = conv(c, 0)
        val_u = conv(c, 1)
        act_scr[:, c * tf:(c + 1) * tf] = ((gate_u * jax.nn.sigmoid(gate_u)) * val_u).astype(BF16)

    o_ref[...] = x_ref[...] + _dot(act_scr[...], wd_ref[...])


def _convglu(x2, gain, w_up, conv_w, conv_b, w_down, *, seq, tm=512, tf=256):
    t, d = x2.shape
    ff = w_down.shape[0]
    halo = CONV_HALO
    tiles_per_seq = seq // tm
    halo_blocks = tm // halo
    return pl.pallas_call(
        functools.partial(_convglu_kernel, tm=tm, tf=tf, tiles_per_seq=tiles_per_seq),
        grid=(t // tm,),
        in_specs=[
            pl.BlockSpec((tm, d), lambda i: (i, 0)),
            pl.BlockSpec((halo, d), lambda i: (jnp.maximum(i * halo_blocks - 1, 0), 0)),
            _resident((1, d)),
            _resident(w_up.shape),
            _resident(conv_w.shape),
            _resident((1, 2 * ff)),
            _resident(w_down.shape),
        ],
        out_specs=pl.BlockSpec((tm, d), lambda i: (i, 0)),
        out_shape=jax.ShapeDtypeStruct((t, d), F32),
        scratch_shapes=[
            pltpu.VMEM((tm + halo, d), BF16),
            pltpu.VMEM((2, 2, tm + halo, tf), F32),
            pltpu.VMEM((tm, ff), BF16),
        ],
        compiler_params=_params("parallel"),
        name="convglu",
    )(x2, x2, gain.reshape(1, d), w_up, conv_w, conv_b.reshape(1, -1), w_down)


def _rope_tables(seq, dim):
    inv = 1.0 / (ROPE_THETA ** (jnp.arange(0, dim, 2, dtype=F32) / dim))
    ang = jnp.arange(seq, dtype=F32)[:, None] * inv[None, :]
    return jnp.cos(ang), jnp.sin(ang)


def kernel(x, attn_norm, w_in, qn_a, kn_a, qn_b, kn_b, sinks, qn_c, kn_c, lam_q1, lam_k1, lam_q2, lam_k2, subln,
           w_pa, w_pb, w_pc, w_out, mlp_norm, w_up, conv_w, conv_b, w_down):
    batch, seq, d = x.shape
    depth = w_in.shape[0]
    x2 = x.reshape(batch * seq, d)

    def rope_block(dim):
        cos, sin = _rope_tables(seq, dim)
        reps = LANES // dim
        return (jnp.tile(jnp.concatenate([cos, cos], axis=-1), (1, reps)),
                jnp.tile(jnp.concatenate([-sin, sin], axis=-1), (1, reps)))

    tables = rope_block(HEAD_DIM) + rope_block(C_HALF)

    segments = (("qa", A_W), ("ka", A_W), ("va", A_W), ("qb", B_QW), ("kb", B_KVW), ("vb", B_KVW),
                ("qc", C_W), ("kc", C_W), ("vc", C_W), ("gates", N_BRANCH * d))

    for i in range(depth):
        lam_init = 0.8 - 0.6 * float(np.exp(-0.3 * i))
        qk_gains = jnp.stack([jnp.tile(g, LANES // g.shape[0])
                              for g in (qn_a[i], kn_a[i], qn_b[i], kn_b[i], qn_c[i], kn_c[i])]).astype(F32)
        qa, ka, kmean, vta, qb, kb, vtb, qc, kc, vtc, gates = _inproj(
            x2, attn_norm[i], w_in[i].astype(BF16), qk_gains, tables, batch=batch, seq=seq, segments=segments)

        n_blk = seq // MOBA_BLOCK
        kmean = jnp.pad(kmean.reshape(batch, n_blk, A_W), ((0, 0), (0, LANES - n_blk), (0, 0)))
        ya = _moba(qa, ka, vta, kmean, batch=batch, seq=seq)
        yb = _swa(qb, kb, vtb, sinks[i], batch=batch, seq=seq)
        lam_rows = jnp.stack([lam_q1[i], lam_k1[i], lam_q2[i], lam_k2[i]]).astype(F32)
        yc = _diff(qc, kc, vtc, lam_rows, subln[i], batch=batch, seq=seq, lam_init=lam_init, blk=DIFF_BLOCK)

        x2 = _merge(x2, ya, yb, yc, gates, w_pa[i].astype(BF16), w_pb[i].astype(BF16), w_pc[i].astype(BF16),
                    w_out[i].astype(BF16))
        x2 = _convglu(x2, mlp_norm[i], w_up[i].astype(BF16), conv_w[i], conv_b[i], w_down[i].astype(BF16), seq=seq)

    return x2.reshape(batch, seq, d)
```

```python
import functools

import numpy as np
import jax
import jax.numpy as jnp
from jax import lax
from jax.experimental import pallas as pl
from jax.experimental.pallas import tpu as pltpu

F32 = jnp.float32
BF16 = jnp.bfloat16
NEG_INF = float("-inf")

LANES = 128
VMEM_LIMIT = 48 * 1024 * 1024

EPS = 1e-6
HEAD_DIM = 64
ROPE_THETA = 10000.0
A_HEADS = 4
MOBA_BLOCK = 256
MOBA_TOPK = 3
B_HEADS = 8
B_KV_HEADS = 2
WINDOW = 128
C_HEADS = 4
C_HALF = HEAD_DIM // 2
N_BRANCH = 3
CONV_W = 3
CONV_HALO = 16
BF16_SUBLANES = 16
V_ROWS = HEAD_DIM + BF16_SUBLANES
DIFF_BLOCK = 512
LOG2E = 1.4426950408889634

A_W = A_HEADS * HEAD_DIM
B_QW = B_HEADS * HEAD_DIM
B_KVW = B_KV_HEADS * HEAD_DIM
C_W = C_HEADS * HEAD_DIM


def _dot(a, b):
    return jnp.dot(a, b, preferred_element_type=F32)


def _split3(a):
    a1 = a.astype(BF16)
    r = a - a1.astype(F32)
    a2 = r.astype(BF16)
    a3 = (r - a2.astype(F32)).astype(BF16)
    return a1, a2, a3


def _resident(shape):
    return pl.BlockSpec(shape, lambda *_: (0,) * len(shape), pipeline_mode=pl.Buffered(1))


def _params(*semantics):
    return pltpu.CompilerParams(dimension_semantics=semantics, vmem_limit_bytes=VMEM_LIMIT)


def _group_sum_matrix(w, hd):
    idx = np.arange(w) // hd
    return jnp.asarray((idx[:, None] == idx[None, :]).astype(np.float32), dtype=BF16)


MM_COLS = 256


def _inproj_kernel(x_ref, g_ref, w_ref, gains_ref, cos64_ref, sin64_ref, cos32_ref, sin32_ref, gs64_ref, gs32_ref,
                   qa_ref, ka_ref, km_ref, vta_ref, qb_ref, kb_ref, vtb_ref, qc_ref, kc_ref, vtc_ref, gates_ref,
                   h_scr, *, blocks, n_qkv):
    x = x_ref[...]
    tm = x.shape[0]
    ms = jnp.mean(x * x, axis=-1, keepdims=True)
    h_scr[...] = (x * lax.rsqrt(ms + EPS) * g_ref[...]).astype(BF16)
    lane = lax.broadcasted_iota(jnp.int32, (tm, LANES), 1)

    def norm_rope(y, gain_row, hd, scale):
        cos_ref, sin_ref, gs_ref = ((cos64_ref, sin64_ref, gs64_ref) if hd == HEAD_DIM
                                    else (cos32_ref, sin32_ref, gs32_ref))
        sq = y * y
        hi = sq.astype(BF16)
        lo = (sq - hi.astype(F32)).astype(BF16)
        msq = (_dot(hi, gs_ref[...]) + _dot(lo, gs_ref[...])) * (1.0 / hd)
        yn = y * lax.rsqrt(msq + EPS) * gains_ref[gain_row:gain_row + 1, :]
        half = hd // 2
        partner = jnp.where(lane % hd < half, pltpu.roll(yn, LANES - half, 1), pltpu.roll(yn, half, 1))
        out = yn * cos_ref[...] + partner * sin_ref[...]
        return out * scale if scale != 1.0 else out

    def store_vt(vt_ref, y, lb):
        tk = vt_ref.shape[-1]
        yt = y.T
        ones = jnp.ones((V_ROWS - HEAD_DIM, tk), F32)
        for hh in range(LANES // HEAD_DIM):
            for n in range(tm // tk):
                blk = yt[hh * HEAD_DIM:(hh + 1) * HEAD_DIM, n * tk:(n + 1) * tk]
                vt_ref[0, lb * (LANES // HEAD_DIM) + hh, n] = jnp.concatenate([blk, ones], axis=0).astype(BF16)

    q_scale = HEAD_DIM ** -0.5 * LOG2E
    c_scale = C_HALF ** -0.5 * LOG2E

    def epilogue(kind, lb, y):
        cols = slice(lb * LANES, (lb + 1) * LANES)
        if kind == "qa":
            qa_ref[:, cols] = norm_rope(y, 0, HEAD_DIM, 1.0)
        elif kind == "ka":
            yk = norm_rope(y, 1, HEAD_DIM, 1.0)
            ka_ref[:, cols] = yk.astype(BF16)
            for n in range(tm // MOBA_BLOCK):
                rows = yk[n * MOBA_BLOCK:(n + 1) * MOBA_BLOCK]
                km_ref[n, :, cols] = jnp.sum(rows, axis=0, keepdims=True) * (1.0 / MOBA_BLOCK)
        elif kind == "va":
            store_vt(vta_ref, y, lb)
        elif kind == "qb":
            qb_ref[:, cols] = norm_rope(y, 2, HEAD_DIM, q_scale)
        elif kind == "kb":
            kb_ref[:, cols] = norm_rope(y, 3, HEAD_DIM, 1.0).astype(BF16)
        elif kind == "vb":
            store_vt(vtb_ref, y, lb)
        elif kind == "qc":
            qc_ref[:, cols] = norm_rope(y, 4, C_HALF, c_scale)
        elif kind == "kc":
            kc_ref[:, cols] = norm_rope(y, 5, C_HALF, 1.0).astype(BF16)
        elif kind == "vc":
            store_vt(vtc_ref, y, lb)

    def chunk(c):
        return _dot(h_scr[...], w_ref[:, c * MM_COLS:(c + 1) * MM_COLS])

    n_chunks = w_ref.shape[1] // MM_COLS
    y_next = chunk(0)
    for c in range(n_chunks):
        y = y_next
        if c + 1 < n_chunks:
            y_next = chunk(c + 1)
        col0 = c * MM_COLS
        if col0 < n_qkv:
            for part in range(MM_COLS // LANES):
                kind, lb = blocks[col0 // LANES + part]
                epilogue(kind, lb, y[:, part * LANES:(part + 1) * LANES])
        else:
            gates_ref[:, col0 - n_qkv:col0 - n_qkv + MM_COLS] = y.astype(gates_ref.dtype)


def _inproj(x2, gain, w_bf, qk_gains, tables, *, batch, seq, segments, tm=512):
    t, d = x2.shape
    n = w_bf.shape[1]
    blocks = [(name, lb) for name, width in segments[:-1] for lb in range(width // LANES)]
    n_qkv = len(blocks) * LANES
    width = dict(segments)
    tps = seq // tm
    rows = lambda i: (i, 0)
    pos = lambda i: (i % tps, 0)

    def vt_spec(heads, tk):
        return pl.BlockSpec((1, heads, tm // tk, V_ROWS, tk), lambda i: (i // tps, 0, i % tps, 0, 0))

    def vt_shape(heads, tk):
        return jax.ShapeDtypeStruct((batch, heads, seq // tk, V_ROWS, tk), BF16)

    n_mb = tm // MOBA_BLOCK
    out_specs = [
        pl.BlockSpec((tm, width["qa"]), rows),
        pl.BlockSpec((tm, width["ka"]), rows),
        pl.BlockSpec((n_mb, 1, width["ka"]), lambda i: (i, 0, 0)),
        vt_spec(A_HEADS, MOBA_BLOCK),
        pl.BlockSpec((tm, width["qb"]), rows),
        pl.BlockSpec((tm, width["kb"]), rows),
        vt_spec(B_KV_HEADS, WINDOW),
        pl.BlockSpec((tm, width["qc"]), rows),
        pl.BlockSpec((tm, width["kc"]), rows),
        vt_spec(C_HEADS, DIFF_BLOCK),
        pl.BlockSpec((tm, width["gates"]), rows),
    ]
    out_shape = [
        jax.ShapeDtypeStruct((t, width["qa"]), F32),
        jax.ShapeDtypeStruct((t, width["ka"]), BF16),
        jax.ShapeDtypeStruct((t // MOBA_BLOCK, 1, width["ka"]), F32),
        vt_shape(A_HEADS, MOBA_BLOCK),
        jax.ShapeDtypeStruct((t, width["qb"]), F32),
        jax.ShapeDtypeStruct((t, width["kb"]), BF16),
        vt_shape(B_KV_HEADS, WINDOW),
        jax.ShapeDtypeStruct((t, width["qc"]), F32),
        jax.ShapeDtypeStruct((t, width["kc"]), BF16),
        vt_shape(C_HEADS, DIFF_BLOCK),
        jax.ShapeDtypeStruct((t, width["gates"]), BF16),
    ]
    table_spec = pl.BlockSpec((tm, LANES), pos)
    return pl.pallas_call(
        functools.partial(_inproj_kernel, blocks=blocks, n_qkv=n_qkv),
        grid=(t // tm,),
        in_specs=[pl.BlockSpec((tm, d), rows), _resident((1, d)), _resident((d, n)), _resident(qk_gains.shape),
                  table_spec, table_spec, table_spec, table_spec,
                  _resident((LANES, LANES)), _resident((LANES, LANES))],
        out_specs=out_specs,
        out_shape=out_shape,
        scratch_shapes=[pltpu.VMEM((tm, d), BF16)],
        compiler_params=_params("parallel"),
        name="inproj",
    )(x2, gain.reshape(1, d), w_bf, qk_gains, *tables, _group_sum_matrix(LANES, HEAD_DIM),
      _group_sum_matrix(LANES, C_HALF))


def _flash_update(acc_ref, stream, s, vt_groups, m):
    m_new = jnp.maximum(m, jnp.max(s, axis=0, keepdims=True))
    shift = jnp.where(m_new == NEG_INF, 0.0, m_new)
    alpha = jnp.exp2(m - shift)
    p = jnp.exp2(s - shift).astype(BF16)
    cw = p.shape[1] // len(vt_groups)
    pvs = []
    for g, vts in enumerate(vt_groups):
        tk = vts[0].shape[1]
        pv = _dot(vts[0], p[0:tk, g * cw:(g + 1) * cw])
        for n in range(1, len(vts)):
            pv = pv + _dot(vts[n], p[n * tk:(n + 1) * tk, g * cw:(g + 1) * cw])
        pvs.append(pv)
    pv = pvs[0] if len(pvs) == 1 else jnp.concatenate(pvs, axis=1)
    acc_ref[stream] = alpha * acc_ref[stream] + pv
    return m_new


def _rolling_streams(n_streams, s_scr, scores, next_first_scores, update):
    out = []
    s = s_scr[...]
    for st in range(n_streams):
        if st + 1 < n_streams:
            s_next = scores(st + 1)
        elif next_first_scores is not None:
            s_scr[...] = next_first_scores()
        out.append(update(st, s))
        if st + 1 < n_streams:
            s = s_next
    return tuple(out)


def _moba_kernel(q_ref, k_ref, vt_ref, km_ref, o_ref, acc_scr, sel_scr, s_scr, *, tq):
    qi = pl.program_id(1)
    n_heads = q_ref.shape[1] // HEAD_DIM
    n_blk = km_ref.shape[1]
    per_block = LANES // HEAD_DIM
    qt = q_ref[...].T
    feat = lax.broadcasted_iota(jnp.int32, (LANES, tq), 0)
    blk = lax.broadcasted_iota(jnp.int32, (n_blk, tq), 0)
    blk_f = blk.astype(F32)
    krow = lax.broadcasted_iota(jnp.int32, (tq, per_block * tq), 0)
    qcol = lax.broadcasted_iota(jnp.int32, (tq, per_block * tq), 1) % tq
    causal = krow <= qcol

    n_streams = n_heads // per_block

    def lane_block(x, b):
        return x[:, b * LANES:(b + 1) * LANES]

    qts, sels = [], []
    for h in range(n_heads):
        b, hh = divmod(h, per_block)
        if hh == 0:
            km_parts = _split3(km_ref[0, :, b * LANES:(b + 1) * LANES])
        qh = jnp.where((feat >= hh * HEAD_DIM) & (feat < (hh + 1) * HEAD_DIM), qt[b * LANES:(b + 1) * LANES], 0.0)

        k1, k2, k3 = km_parts
        q1, q2, q3 = _split3(qh)
        gate = (_dot(k1, q1) + _dot(k2, q1) + _dot(k1, q2) + _dot(k3, q1) + _dot(k2, q2) + _dot(k1, q3))
        g = jnp.where(blk < qi, gate, NEG_INF)
        sel = jnp.zeros((n_blk, tq), F32)
        for _ in range(MOBA_TOPK):
            mx = jnp.max(g, axis=0, keepdims=True)
            first = jnp.min(jnp.where((g == mx) & (mx > NEG_INF), blk_f, float(n_blk)), axis=0, keepdims=True)
            pick = blk_f == first
            sel = jnp.where(pick, 1.0, sel)
            g = jnp.where(pick, NEG_INF, g)
        sels.append(sel)
        qts.append((qh * (HEAD_DIM ** -0.5 * LOG2E)).astype(BF16))
        if hh == per_block - 1:
            sel_scr[b] = jnp.concatenate(sels, axis=1)
            sels = []
    qts = [jnp.concatenate(qts[b * per_block:(b + 1) * per_block], axis=1) for b in range(n_streams)]

    acc_scr[...] = jnp.zeros_like(acc_scr)

    def pair_keys(t):
        return k_ref[pl.ds(pl.multiple_of(t * 2 * tq, 2 * tq), 2 * tq), :]

    def pair_values(b, t):
        return [[vt_ref[0, b * per_block + hh, 2 * t], vt_ref[0, b * per_block + hh, 2 * t + 1]]
                for hh in range(per_block)]

    def masked(s, first, second):
        return jnp.concatenate([jnp.where(first, s[0:tq], NEG_INF), jnp.where(second, s[tq:2 * tq], NEG_INF)], axis=0)

    last = qi // 2
    s_scr[...] = _dot(lane_block(pair_keys(0), 0), qts[0])

    def body(t, ms):
        kj = pair_keys(t)
        kn = pair_keys(t + 1)

        def update(b, s):
            first = sel_scr[b, pl.ds(2 * t, 1), :] > 0.0
            second = sel_scr[b, pl.ds(2 * t + 1, 1), :] > 0.0
            return _flash_update(acc_scr, b, masked(s, first, second), pair_values(b, t), ms[b])

        return _rolling_streams(n_streams, s_scr, lambda b: _dot(lane_block(kj, b), qts[b]),
                                lambda: _dot(lane_block(kn, 0), qts[0]), update)

    ms = lax.fori_loop(0, last, body,
                       tuple(jnp.full((1, per_block * tq), NEG_INF, F32) for _ in range(n_streams)))

    k_last = pair_keys(last)
    own_second = qi % 2 == 1

    def update_last(b, s):
        picked = sel_scr[b, pl.ds(2 * last, 1), :] > 0.0
        first = (own_second & picked) | (jnp.logical_not(own_second) & causal)
        second = own_second & causal
        return _flash_update(acc_scr, b, masked(s, first, second), pair_values(b, last), ms[b])

    _rolling_streams(n_streams, s_scr, lambda b: _dot(lane_block(k_last, b), qts[b]), None, update_last)

    outs = []
    for b in range(n_streams):
        acc = acc_scr[b]
        o = acc[0:HEAD_DIM] / acc[HEAD_DIM:HEAD_DIM + 1]
        outs += [o[:, hh * tq:(hh + 1) * tq] for hh in range(per_block)]
    o_ref[...] = jnp.concatenate(outs, axis=0).T.astype(o_ref.dtype)


def _moba(qa, ka, vta, kmean, *, batch, seq):
    tq = MOBA_BLOCK
    t, width = qa.shape
    n_heads = width // HEAD_DIM
    n_q = seq // tq
    n_streams = width // LANES
    cols = (LANES // HEAD_DIM) * tq
    return pl.pallas_call(
        functools.partial(_moba_kernel, tq=tq),
        grid=(batch, n_q),
        in_specs=[
            pl.BlockSpec((tq, width), lambda b, i: (b * n_q + i, 0)),
            pl.BlockSpec((seq, width), lambda b, i: (b, 0)),
            pl.BlockSpec((1, n_heads, n_q, V_ROWS, tq), lambda b, i: (b, 0, 0, 0, 0)),
            pl.BlockSpec((1, n_q, width), lambda b, i: (b, 0, 0)),
        ],
        out_specs=pl.BlockSpec((tq, width), lambda b, i: (b * n_q + i, 0)),
        out_shape=jax.ShapeDtypeStruct((t, width), BF16),
        scratch_shapes=[pltpu.VMEM((n_streams, V_ROWS, cols), F32), pltpu.VMEM((n_streams, n_q, cols), F32),
                        pltpu.VMEM((2 * tq, cols), F32)],
        compiler_params=_params("parallel", "arbitrary"),
        name="moba",
    )(qa, ka, vta, kmean)


def _swa_kernel(q_ref, kp_ref, kc_ref, vtp_ref, vtc_ref, sink_ref, o_ref):
    i = pl.program_id(1)
    w = WINDOW
    group = B_HEADS // B_KV_HEADS
    qt = q_ref[...].T
    kband = jnp.concatenate([kp_ref[...], kc_ref[...]], axis=0)
    krow = lax.broadcasted_iota(jnp.int32, (2 * w, w), 0)
    qcol = lax.broadcasted_iota(jnp.int32, (2 * w, w), 1)
    rel = qcol + w - krow
    ok = (rel >= 0) & (rel < w) & ((krow >= w) | (i > 0))
    ok = jnp.concatenate([ok] * group, axis=1)
    zeros = jnp.zeros((HEAD_DIM, w), F32)

    def scores(kv):
        cols = []
        for g in range(group):
            h = kv * group + g
            qh = qt[h * HEAD_DIM:(h + 1) * HEAD_DIM]
            cols.append(jnp.concatenate([qh if n == kv else zeros for n in range(B_KV_HEADS)], axis=0))
        return _dot(kband, jnp.concatenate(cols, axis=1).astype(BF16))

    all_scores = [scores(kv) for kv in range(B_KV_HEADS)]
    outs = []
    for kv in range(B_KV_HEADS):
        s = jnp.where(ok, all_scores[kv], NEG_INF)
        sink = sink_ref[:, kv * group * w:(kv + 1) * group * w] * LOG2E
        m = jnp.maximum(jnp.max(s, axis=0, keepdims=True), sink)
        p = jnp.exp2(s - m).astype(BF16)
        pv = _dot(vtp_ref[0, kv, 0], p[0:w]) + _dot(vtc_ref[0, kv, 0], p[w:2 * w])
        o = pv[0:HEAD_DIM] / (pv[HEAD_DIM:HEAD_DIM + 1] + jnp.exp2(sink - m))
        outs += [o[:, g * w:(g + 1) * w] for g in range(group)]
    o_ref[...] = jnp.concatenate(outs, axis=0).T.astype(o_ref.dtype)


def _swa(qb, kb, vtb, sinks, *, batch, seq):
    t, qw = qb.shape
    w = WINDOW
    nb = seq // w
    prev = lambda b, i: (b * nb + jnp.maximum(i - 1, 0), 0)
    cur = lambda b, i: (b * nb + i, 0)
    vt_block = (1, B_KV_HEADS, 1, V_ROWS, w)
    sink_row = jnp.repeat(sinks.astype(F32), w).reshape(1, B_HEADS * w)
    return pl.pallas_call(
        _swa_kernel,
        grid=(batch, nb),
        in_specs=[pl.BlockSpec((w, qw), cur),
                  pl.BlockSpec((w, LANES), prev),
                  pl.BlockSpec((w, LANES), cur),
                  pl.BlockSpec(vt_block, lambda b, i: (b, 0, jnp.maximum(i - 1, 0), 0, 0)),
                  pl.BlockSpec(vt_block, lambda b, i: (b, 0, i, 0, 0)),
                  pl.BlockSpec((1, B_HEADS * w), lambda b, i: (0, 0))],
        out_specs=pl.BlockSpec((w, qw), cur),
        out_shape=jax.ShapeDtypeStruct((t, qw), BF16),
        compiler_params=_params("parallel", "arbitrary"),
        name="swa",
    )(qb, kb, kb, vtb, vtb, sink_row)


def _diff_kernel(q_ref, k_ref, vt_ref, lam_ref, g_ref, o_ref, acc_scr, s_scr, *, blk, lam_init):
    qi = pl.program_id(2)
    n_heads = LANES // HEAD_DIM
    qt = q_ref[...].T
    feat = lax.broadcasted_iota(jnp.int32, (LANES, blk), 0)
    krow = lax.broadcasted_iota(jnp.int32, (blk, blk), 0)
    qcol = lax.broadcasted_iota(jnp.int32, (blk, blk), 1)

    lq = lam_ref[...]
    lam = (jnp.exp(jnp.sum(lq[0:1] * lq[1:2], axis=1, keepdims=True))
           - jnp.exp(jnp.sum(lq[2:3] * lq[3:4], axis=1, keepdims=True)) + lam_init)

    n_streams = 2 * n_heads
    qts = [jnp.where((feat >= st * C_HALF) & (feat < (st + 1) * C_HALF), qt, 0.0).astype(BF16)
           for st in range(n_streams)]
    acc_scr[...] = jnp.zeros_like(acc_scr)

    def keys(j):
        return k_ref[pl.ds(pl.multiple_of(j * blk, blk), blk), :]

    s_scr[...] = _dot(keys(0), qts[0])

    def body(j, ms):
        kj = keys(j)
        kn = keys(j + 1)
        return _rolling_streams(
            n_streams, s_scr, lambda st: _dot(kj, qts[st]), lambda: _dot(kn, qts[0]),
            lambda st, s: _flash_update(acc_scr, st, s, [[vt_ref[0, st // 2, j]]], ms[st]))

    ms = lax.fori_loop(0, qi, body, tuple(jnp.full((1, blk), NEG_INF, F32) for _ in range(n_streams)))
    k_own = keys(qi)
    _rolling_streams(
        n_streams, s_scr, lambda st: _dot(k_own, qts[st]), None,
        lambda st, s: _flash_update(acc_scr, st, jnp.where(krow <= qcol, s, NEG_INF), [[vt_ref[0, st // 2, qi]]],
                                    ms[st]))

    outs = []
    for h in range(n_heads):
        maps = [acc_scr[2 * h + c, 0:HEAD_DIM, :] / acc_scr[2 * h + c, HEAD_DIM:HEAD_DIM + 1, :] for c in range(2)]
        o = maps[0] - lam * maps[1]
        ms_o = jnp.mean(o * o, axis=0, keepdims=True)
        outs.append((o * lax.rsqrt(ms_o + EPS) * g_ref[...]) * (1.0 - lam_init))
    o_ref[...] = jnp.concatenate(outs, axis=0).T.astype(o_ref.dtype)


def _diff(qc, kc, vtc, lam_rows, subln, *, batch, seq, lam_init, blk):
    t, width = qc.shape
    n_pairs = width // LANES
    n_heads = LANES // HEAD_DIM
    n_q = seq // blk
    return pl.pallas_call(
        functools.partial(_diff_kernel, blk=blk, lam_init=lam_init),
        grid=(batch, n_pairs, n_q),
        in_specs=[
            pl.BlockSpec((blk, LANES), lambda b, hp, i: (b * n_q + i, hp)),
            pl.BlockSpec((seq, LANES), lambda b, hp, i: (b, hp)),
            pl.BlockSpec((1, n_heads, n_q, V_ROWS, blk), lambda b, hp, i: (b, hp, 0, 0, 0)),
            pl.BlockSpec((4, C_HALF), lambda b, hp, i: (0, 0)),
            pl.BlockSpec((HEAD_DIM, 1), lambda b, hp, i: (0, 0)),
        ],
        out_specs=pl.BlockSpec((blk, LANES), lambda b, hp, i: (b * n_q + i, hp)),
        out_shape=jax.ShapeDtypeStruct((t, width), BF16),
        scratch_shapes=[pltpu.VMEM((2 * n_heads, V_ROWS, blk), F32), pltpu.VMEM((blk, blk), F32)],
        compiler_params=_params("parallel", "parallel", "arbitrary"),
        name="diff",
    )(qc, kc, vtc, lam_rows, subln.reshape(HEAD_DIM, 1))


def _merge_kernel(x_ref, ya_ref, yb_ref, yc_ref, ga_ref, gb_ref, gc_ref, wa_ref, wb_ref, wc_ref, wo_ref, o_ref):
    merged = (jax.nn.sigmoid(ga_ref[...].astype(F32)) * _dot(ya_ref[...], wa_ref[...])
              + jax.nn.sigmoid(gb_ref[...].astype(F32)) * _dot(yb_ref[...], wb_ref[...])
              + jax.nn.sigmoid(gc_ref[...].astype(F32)) * _dot(yc_ref[...], wc_ref[...]))
    o_ref[...] = x_ref[...] + _dot(merged.astype(BF16), wo_ref[...])


def _merge(x2, ya, yb, yc, gates, w_pa, w_pb, w_pc, w_out, *, tm=512):
    t, d = x2.shape
    rows = lambda i: (i, 0)
    return pl.pallas_call(
        _merge_kernel,
        grid=(t // tm,),
        in_specs=[
            pl.BlockSpec((tm, d), rows),
            pl.BlockSpec((tm, ya.shape[1]), rows),
            pl.BlockSpec((tm, yb.shape[1]), rows),
            pl.BlockSpec((tm, yc.shape[1]), rows),
            pl.BlockSpec((tm, d), lambda i: (i, 0)),
            pl.BlockSpec((tm, d), lambda i: (i, 1)),
            pl.BlockSpec((tm, d), lambda i: (i, 2)),
            _resident(w_pa.shape),
            _resident(w_pb.shape),
            _resident(w_pc.shape),
            _resident(w_out.shape),
        ],
        out_specs=pl.BlockSpec((tm, d), rows),
        out_shape=jax.ShapeDtypeStruct((t, d), F32),
        compiler_params=_params("parallel"),
        name="merge",
    )(x2, ya, yb, yc, gates, gates, gates, w_pa, w_pb, w_pc, w_out)


def _convglu_kernel(x_ref, xh_ref, g_ref, wu_ref, cw_ref, cb_ref, wd_ref, o_ref, h_scr, u_scr, act_scr,
                    *, tm, tf, tiles_per_seq):
    i = pl.program_id(0)
    halo = CONV_HALO
    ff = wd_ref.shape[0]
    n_chunks = ff // tf

    def normed(x):
        ms = jnp.mean(x * x, axis=-1, keepdims=True)
        return x * lax.rsqrt(ms + EPS) * g_ref[...]

    keep = jnp.where(i % tiles_per_seq == 0, 0.0, 1.0)
    h_scr[0:halo, :] = (normed(xh_ref[...]) * keep).astype(BF16)
    h_scr[halo:, :] = normed(x_ref[...]).astype(BF16)

    def up(c):
        h = h_scr[...]
        u_scr[c % 2, 0] = _dot(h, wu_ref[:, c * tf:(c + 1) * tf])
        u_scr[c % 2, 1] = _dot(h, wu_ref[:, ff + c * tf:ff + (c + 1) * tf])

    def conv(c, half):
        col0 = half * ff + c * tf
        y = cb_ref[:, col0:col0 + tf]
        for j in range(CONV_W):
            y = y + u_scr[c % 2, half, pl.ds(halo - (CONV_W - 1) + j, tm), :] * cw_ref[j:j + 1, col0:col0 + tf]
        return y

    up(0)
    for c in range(n_chunks):
        if c + 1 < n_chunks:
            up(c + 1)
        gate_u = conv(c, 0)
        val_u = conv(c, 1)
        act_scr[:, c * tf:(c + 1) * tf] = ((gate_u * jax.nn.sigmoid(gate_u)) * val_u).astype(BF16)

    o_ref[...] = x_ref[...] + _dot(act_scr[...], wd_ref[...])


def _convglu(x2, gain, w_up, conv_w, conv_b, w_down, *, seq, tm=512, tf=256):
    t, d = x2.shape
    ff = w_down.shape[0]
    halo = CONV_HALO
    tiles_per_seq = seq // tm
    halo_blocks = tm // halo
    return pl.pallas_call(
        functools.partial(_convglu_kernel, tm=tm, tf=tf, tiles_per_seq=tiles_per_seq),
        grid=(t // tm,),
        in_specs=[
            pl.BlockSpec((tm, d), lambda i: (i, 0)),
            pl.BlockSpec((halo, d), lambda i: (jnp.maximum(i * halo_blocks - 1, 0), 0)),
            _resident((1, d)),
            _resident(w_up.shape),
            _resident(conv_w.shape),
            _resident((1, 2 * ff)),
            _resident(w_down.shape),
        ],
        out_specs=pl.BlockSpec((tm, d), lambda i: (i, 0)),
        out_shape=jax.ShapeDtypeStruct((t, d), F32),
        scratch_shapes=[
            pltpu.VMEM((tm + halo, d), BF16),
            pltpu.VMEM((2, 2, tm + halo, tf), F32),
            pltpu.VMEM((tm, ff), BF16),
        ],
        compiler_params=_params("parallel"),
        name="convglu",
    )(x2, x2, gain.reshape(1, d), w_up, conv_w, conv_b.reshape(1, -1), w_down)


def _rope_tables(seq, dim):
    inv = 1.0 / (ROPE_THETA ** (jnp.arange(0, dim, 2, dtype=F32) / dim))
    ang = jnp.arange(seq, dtype=F32)[:, None] * inv[None, :]
    return jnp.cos(ang), jnp.sin(ang)


def kernel(x, attn_norm, w_in, qn_a, kn_a, qn_b, kn_b, sinks, qn_c, kn_c, lam_q1, lam_k1, lam_q2, lam_k2, subln,
           w_pa, w_pb, w_pc, w_out, mlp_norm, w_up, conv_w, conv_b, w_down):
    batch, seq, d = x.shape
    depth = w_in.shape[0]
    x2 = x.reshape(batch * seq, d)

    def rope_block(dim):
        cos, sin = _rope_tables(seq, dim)
        reps = LANES // dim
        return (jnp.tile(jnp.concatenate([cos, cos], axis=-1), (1, reps)),
                jnp.tile(jnp.concatenate([-sin, sin], axis=-1), (1, reps)))

    tables = rope_block(HEAD_DIM) + rope_block(C_HALF)

    segments = (("qa", A_W), ("ka", A_W), ("va", A_W), ("qb", B_QW), ("kb", B_KVW), ("vb", B_KVW),
                ("qc", C_W), ("kc", C_W), ("vc", C_W), ("gates", N_BRANCH * d))

    for i in range(depth):
        lam_init = 0.8 - 0.6 * float(np.exp(-0.3 * i))
        qk_gains = jnp.stack([jnp.tile(g, LANES // g.shape[0])
                              for g in (qn_a[i], kn_a[i], qn_b[i], kn_b[i], qn_c[i], kn_c[i])]).astype(F32)
        qa, ka, kmean, vta, qb, kb, vtb, qc, kc, vtc, gates = _inproj(
            x2, attn_norm[i], w_in[i].astype(BF16), qk_gains, tables, batch=batch, seq=seq, segments=segments)

        kmean = kmean.reshape(batch, seq // MOBA_BLOCK, A_W)
        ya = _moba(qa, ka, vta, kmean, batch=batch, seq=seq)
        yb = _swa(qb, kb, vtb, sinks[i], batch=batch, seq=seq)
        lam_rows = jnp.stack([lam_q1[i], lam_k1[i], lam_q2[i], lam_k2[i]]).astype(F32)
        yc = _diff(qc, kc, vtc, lam_rows, subln[i], batch=batch, seq=seq, lam_init=lam_init, blk=DIFF_BLOCK)

        x2 = _merge(x2, ya, yb, yc, gates, w_pa[i].astype(BF16), w_pb[i].astype(BF16), w_pc[i].astype(BF16),
                    w_out[i].astype(BF16))
        x2 = _convglu(x2, mlp_norm[i], w_up[i].astype(BF16), conv_w[i], conv_b[i], w_down[i].astype(BF16), seq=seq)

    return x2.reshape(batch, seq, d)
```

```python
import functools

import numpy as np
import jax
import jax.numpy as jnp
from jax import lax
from jax.experimental import pallas as pl
from jax.experimental.pallas import tpu as pltpu

F32 = jnp.float32
BF16 = jnp.bfloat16
NEG_INF = float("-inf")

LANES = 128
VMEM_LIMIT = 48 * 1024 * 1024

EPS = 1e-6
HEAD_DIM = 64
ROPE_THETA = 10000.0
A_HEADS = 4
MOBA_BLOCK = 256
MOBA_TOPK = 3
B_HEADS = 8
B_KV_HEADS = 2
WINDOW = 128
C_HEADS = 4
C_HALF = HEAD_DIM // 2
N_BRANCH = 3
CONV_W = 3
CONV_HALO = 16
BF16_SUBLANES = 16
V_ROWS = HEAD_DIM + BF16_SUBLANES
DIFF_BLOCK = 512
LOG2E = 1.4426950408889634

A_W = A_HEADS * HEAD_DIM
B_QW = B_HEADS * HEAD_DIM
B_KVW = B_KV_HEADS * HEAD_DIM
C_W = C_HEADS * HEAD_DIM


def _dot(a, b):
    return jnp.dot(a, b, preferred_element_type=F32)


def _split3(a):
    a1 = a.astype(BF16)
    r = a - a1.astype(F32)
    a2 = r.astype(BF16)
    a3 = (r - a2.astype(F32)).astype(BF16)
    return a1, a2, a3


def _resident(shape):
    return pl.BlockSpec(shape, lambda *_: (0,) * len(shape), pipeline_mode=pl.Buffered(1))


def _params(*semantics):
    return pltpu.CompilerParams(dimension_semantics=semantics, vmem_limit_bytes=VMEM_LIMIT)


def _group_sum_matrix(w, hd):
    idx = np.arange(w) // hd
    return jnp.asarray((idx[:, None] == idx[None, :]).astype(np.float32), dtype=BF16)


MM_COLS = 256


def _inproj_kernel(x_ref, g_ref, w_ref, gains_ref, cos64_ref, sin64_ref, cos32_ref, sin32_ref, gs64_ref, gs32_ref,
                   qa_ref, ka_ref, km_ref, vta_ref, qb_ref, kb_ref, vtb_ref, qc_ref, kc_ref, vtc_ref, gates_ref,
                   h_scr, *, blocks, n_qkv):
    x = x_ref[...]
    tm = x.shape[0]
    ms = jnp.mean(x * x, axis=-1, keepdims=True)
    h_scr[...] = (x * lax.rsqrt(ms + EPS) * g_ref[...]).astype(BF16)
    lane = lax.broadcasted_iota(jnp.int32, (tm, LANES), 1)

    def norm_rope(y, gain_row, hd, scale):
        cos_ref, sin_ref, gs_ref = ((cos64_ref, sin64_ref, gs64_ref) if hd == HEAD_DIM
                                    else (cos32_ref, sin32_ref, gs32_ref))
        sq = y * y
        hi = sq.astype(BF16)
        lo = (sq - hi.astype(F32)).astype(BF16)
        msq = (_dot(hi, gs_ref[...]) + _dot(lo, gs_ref[...])) * (1.0 / hd)
        yn = y * lax.rsqrt(msq + EPS) * gains_ref[gain_row:gain_row + 1, :]
        half = hd // 2
        partner = jnp.where(lane % hd < half, pltpu.roll(yn, LANES - half, 1), pltpu.roll(yn, half, 1))
        out = yn * cos_ref[...] + partner * sin_ref[...]
        return out * scale if scale != 1.0 else out

    def store_vt(vt_ref, y, lb):
        tk = vt_ref.shape[-1]
        yt = y.T
        ones = jnp.ones((V_ROWS - HEAD_DIM, tk), F32)
        for hh in range(LANES // HEAD_DIM):
            for n in range(tm // tk):
                blk = yt[hh * HEAD_DIM:(hh + 1) * HEAD_DIM, n * tk:(n + 1) * tk]
                vt_ref[0, lb * (LANES // HEAD_DIM) + hh, n] = jnp.concatenate([blk, ones], axis=0).astype(BF16)

    q_scale = HEAD_DIM ** -0.5 * LOG2E
    c_scale = C_HALF ** -0.5 * LOG2E

    def epilogue(kind, lb, y):
        cols = slice(lb * LANES, (lb + 1) * LANES)
        if kind == "qa":
            qa_ref[:, cols] = norm_rope(y, 0, HEAD_DIM, 1.0)
        elif kind == "ka":
            yk = norm_rope(y, 1, HEAD_DIM, 1.0)
            ka_ref[:, cols] = yk.astype(BF16)
            for n in range(tm // MOBA_BLOCK):
                rows = yk[n * MOBA_BLOCK:(n + 1) * MOBA_BLOCK]
                km_ref[n, :, cols] = jnp.sum(rows, axis=0, keepdims=True) * (1.0 / MOBA_BLOCK)
        elif kind == "va":
            store_vt(vta_ref, y, lb)
        elif kind == "qb":
            qb_ref[:, cols] = norm_rope(y, 2, HEAD_DIM, q_scale)
        elif kind == "kb":
            kb_ref[:, cols] = norm_rope(y, 3, HEAD_DIM, 1.0).astype(BF16)
        elif kind == "vb":
            store_vt(vtb_ref, y, lb)
        elif kind == "qc":
            qc_ref[:, cols] = norm_rope(y, 4, C_HALF, c_scale)
        elif kind == "kc":
            kc_ref[:, cols] = norm_rope(y, 5, C_HALF, 1.0).astype(BF16)
        elif kind == "vc":
            store_vt(vtc_ref, y, lb)

    def chunk(c):
        return _dot(h_scr[...], w_ref[:, c * MM_COLS:(c + 1) * MM_COLS])

    n_chunks = w_ref.shape[1] // MM_COLS
    y_next = chunk(0)
    for c in range(n_chunks):
        y = y_next
        if c + 1 < n_chunks:
            y_next = chunk(c + 1)
        col0 = c * MM_COLS
        if col0 < n_qkv:
            for part in range(MM_COLS // LANES):
                kind, lb = blocks[col0 // LANES + part]
                epilogue(kind, lb, y[:, part * LANES:(part + 1) * LANES])
        else:
            gates_ref[:, col0 - n_qkv:col0 - n_qkv + MM_COLS] = y.astype(gates_ref.dtype)


def _inproj(x2, gain, w_bf, qk_gains, tables, *, batch, seq, segments, tm=512):
    t, d = x2.shape
    n = w_bf.shape[1]
    blocks = [(name, lb) for name, width in segments[:-1] for lb in range(width // LANES)]
    n_qkv = len(blocks) * LANES
    width = dict(segments)
    tps = seq // tm
    rows = lambda i: (i, 0)
    pos = lambda i: (i % tps, 0)

    def vt_spec(heads, tk):
        return pl.BlockSpec((1, heads, tm // tk, V_ROWS, tk), lambda i: (i // tps, 0, i % tps, 0, 0))

    def vt_shape(heads, tk):
        return jax.ShapeDtypeStruct((batch, heads, seq // tk, V_ROWS, tk), BF16)

    n_mb = tm // MOBA_BLOCK
    out_specs = [
        pl.BlockSpec((tm, width["qa"]), rows),
        pl.BlockSpec((tm, width["ka"]), rows),
        pl.BlockSpec((n_mb, 1, width["ka"]), lambda i: (i, 0, 0)),
        vt_spec(A_HEADS, MOBA_BLOCK),
        pl.BlockSpec((tm, width["qb"]), rows),
        pl.BlockSpec((tm, width["kb"]), rows),
        vt_spec(B_KV_HEADS, WINDOW),
        pl.BlockSpec((tm, width["qc"]), rows),
        pl.BlockSpec((tm, width["kc"]), rows),
        vt_spec(C_HEADS, DIFF_BLOCK),
        pl.BlockSpec((tm, width["gates"]), rows),
    ]
    out_shape = [
        jax.ShapeDtypeStruct((t, width["qa"]), F32),
        jax.ShapeDtypeStruct((t, width["ka"]), BF16),
        jax.ShapeDtypeStruct((t // MOBA_BLOCK, 1, width["ka"]), F32),
        vt_shape(A_HEADS, MOBA_BLOCK),
        jax.ShapeDtypeStruct((t, width["qb"]), F32),
        jax.ShapeDtypeStruct((t, width["kb"]), BF16),
        vt_shape(B_KV_HEADS, WINDOW),
        jax.ShapeDtypeStruct((t, width["qc"]), F32),
        jax.ShapeDtypeStruct((t, width["kc"]), BF16),
        vt_shape(C_HEADS, DIFF_BLOCK),
        jax.ShapeDtypeStruct((t, width["gates"]), BF16),
    ]
    table_spec = pl.BlockSpec((tm, LANES), pos)
    return pl.pallas_call(
        functools.partial(_inproj_kernel, blocks=blocks, n_qkv=n_qkv),
        grid=(t // tm,),
        in_specs=[pl.BlockSpec((tm, d), rows), _resident((1, d)), _resident((d, n)), _resident(qk_gains.shape),
                  table_spec, table_spec, table_spec, table_spec,
                  _resident((LANES, LANES)), _resident((LANES, LANES))],
        out_specs=out_specs,
        out_shape=out_shape,
        scratch_shapes=[pltpu.VMEM((tm, d), BF16)],
        compiler_params=_params("parallel"),
        name="inproj",
    )(x2, gain.reshape(1, d), w_bf, qk_gains, *tables, _group_sum_matrix(LANES, HEAD_DIM),
      _group_sum_matrix(LANES, C_HALF))


def _flash_update(acc_ref, stream, s, vt_groups, m, bound=None):
    if bound is None:
        m_new = jnp.maximum(m, jnp.max(s, axis=0, keepdims=True))
        shift = jnp.where(m_new == NEG_INF, 0.0, m_new)
        alpha = jnp.exp2(m - shift)
    else:
        m_new, shift, alpha = m, bound, None
    p = jnp.exp2(s - shift).astype(BF16)
    cw = p.shape[1] // len(vt_groups)
    pvs = []
    for g, vts in enumerate(vt_groups):
        tk = vts[0].shape[1]
        pv = _dot(vts[0], p[0:tk, g * cw:(g + 1) * cw])
        for n in range(1, len(vts)):
            pv = pv + _dot(vts[n], p[n * tk:(n + 1) * tk, g * cw:(g + 1) * cw])
        pvs.append(pv)
    pv = pvs[0] if len(pvs) == 1 else jnp.concatenate(pvs, axis=1)
    acc_ref[stream] = acc_ref[stream] + pv if alpha is None else alpha * acc_ref[stream] + pv
    return m_new


MAX_FIXED_SHIFT = 60.0


def _score_bound(q_gain, k_gain, hd):
    return (jnp.max(jnp.abs(q_gain)) * jnp.max(jnp.abs(k_gain)) * (hd ** 0.5 * LOG2E)).reshape(1).astype(F32)


def _with_score_bound(bound_ref, attend):
    bound = bound_ref[0]
    fixed = bound <= MAX_FIXED_SHIFT

    @pl.when(fixed)
    def _():
        attend(bound)

    @pl.when(jnp.logical_not(fixed))
    def _():
        attend(None)


def _rolling_streams(n_streams, s_scr, scores, next_scores, update):
    ahead_by = s_scr.shape[0]
    assert ahead_by < n_streams
    in_flight = {}
    out = []
    for st in range(n_streams):
        ahead = st + ahead_by
        if ahead < n_streams:
            in_flight[ahead] = scores(ahead)
        elif next_scores is not None:
            s_scr[ahead - n_streams] = next_scores(ahead - n_streams)
        s = s_scr[st] if st < ahead_by else in_flight.pop(st)
        out.append(update(st, s))
    return tuple(out)


def _prime_streams(s_scr, scores):
    for st in range(s_scr.shape[0]):
        s_scr[st] = scores(st)


def _moba_kernel(bound_ref, q_ref, k_ref, vt_ref, km_ref, o_ref, acc_scr, sel_scr, s_scr, *, tq):
    qi = pl.program_id(1)
    n_heads = q_ref.shape[1] // HEAD_DIM
    n_blk = km_ref.shape[1]
    per_block = LANES // HEAD_DIM
    qt = q_ref[...].T
    feat = lax.broadcasted_iota(jnp.int32, (LANES, tq), 0)
    blk = lax.broadcasted_iota(jnp.int32, (n_blk, tq), 0)
    blk_f = blk.astype(F32)
    krow = lax.broadcasted_iota(jnp.int32, (tq, per_block * tq), 0)
    qcol = lax.broadcasted_iota(jnp.int32, (tq, per_block * tq), 1) % tq
    causal = krow <= qcol

    n_streams = n_heads // per_block

    def lane_block(x, b):
        return x[:, b * LANES:(b + 1) * LANES]

    qts, sels = [], []
    for h in range(n_heads):
        b, hh = divmod(h, per_block)
        if hh == 0:
            km_parts = _split3(km_ref[0, :, b * LANES:(b + 1) * LANES])
        qh = jnp.where((feat >= hh * HEAD_DIM) & (feat < (hh + 1) * HEAD_DIM), qt[b * LANES:(b + 1) * LANES], 0.0)

        k1, k2, k3 = km_parts
        q1, q2, q3 = _split3(qh)
        gate = (_dot(k1, q1) + _dot(k2, q1) + _dot(k1, q2) + _dot(k3, q1) + _dot(k2, q2) + _dot(k1, q3))
        g = jnp.where(blk < qi, gate, NEG_INF)
        sel = jnp.zeros((n_blk, tq), F32)
        for _ in range(MOBA_TOPK):
            mx = jnp.max(g, axis=0, keepdims=True)
            first = jnp.min(jnp.where((g == mx) & (mx > NEG_INF), blk_f, float(n_blk)), axis=0, keepdims=True)
            pick = blk_f == first
            sel = jnp.where(pick, 1.0, sel)
            g = jnp.where(pick, NEG_INF, g)
        sels.append(sel)
        qts.append((qh * (HEAD_DIM ** -0.5 * LOG2E)).astype(BF16))
        if hh == per_block - 1:
            sel_scr[b] = jnp.concatenate(sels, axis=1)
            sels = []
    qts = [jnp.concatenate(qts[b * per_block:(b + 1) * per_block], axis=1) for b in range(n_streams)]

    acc_scr[...] = jnp.zeros_like(acc_scr)

    def pair_keys(t):
        return k_ref[pl.ds(pl.multiple_of(t * 2 * tq, 2 * tq), 2 * tq), :]

    def pair_values(b, t):
        return [[vt_ref[0, b * per_block + hh, 2 * t], vt_ref[0, b * per_block + hh, 2 * t + 1]]
                for hh in range(per_block)]

    def masked(s, first, second):
        return jnp.concatenate([jnp.where(first, s[0:tq], NEG_INF), jnp.where(second, s[tq:2 * tq], NEG_INF)], axis=0)

    last = qi // 2
    own_second = qi % 2 == 1

    def attend(bound):
        k_first = pair_keys(0)
        _prime_streams(s_scr, lambda b: _dot(lane_block(k_first, b), qts[b]))

        def body(t, ms):
            kj = pair_keys(t)
            kn = pair_keys(t + 1)

            def update(b, s):
                first = sel_scr[b, pl.ds(2 * t, 1), :] > 0.0
                second = sel_scr[b, pl.ds(2 * t + 1, 1), :] > 0.0
                return _flash_update(acc_scr, b, masked(s, first, second), pair_values(b, t), ms[b], bound)

            return _rolling_streams(n_streams, s_scr, lambda b: _dot(lane_block(kj, b), qts[b]),
                                    lambda b: _dot(lane_block(kn, b), qts[b]), update)

        ms = lax.fori_loop(0, last, body,
                           tuple(jnp.full((1, per_block * tq), NEG_INF, F32) for _ in range(n_streams)))
        k_last = pair_keys(last)

        def update_last(b, s):
            picked = sel_scr[b, pl.ds(2 * last, 1), :] > 0.0
            first = (own_second & picked) | (jnp.logical_not(own_second) & causal)
            second = own_second & causal
            return _flash_update(acc_scr, b, masked(s, first, second), pair_values(b, last), ms[b], bound)

        _rolling_streams(n_streams, s_scr, lambda b: _dot(lane_block(k_last, b), qts[b]), None, update_last)

    _with_score_bound(bound_ref, attend)

    outs = []
    for b in range(n_streams):
        acc = acc_scr[b]
        o = acc[0:HEAD_DIM] / acc[HEAD_DIM:HEAD_DIM + 1]
        outs += [o[:, hh * tq:(hh + 1) * tq] for hh in range(per_block)]
    o_ref[...] = jnp.concatenate(outs, axis=0).T.astype(o_ref.dtype)


def _moba(bound, qa, ka, vta, kmean, *, batch, seq):
    tq = MOBA_BLOCK
    t, width = qa.shape
    n_heads = width // HEAD_DIM
    n_q = seq // tq
    n_streams = width // LANES
    cols = (LANES // HEAD_DIM) * tq
    return pl.pallas_call(
        functools.partial(_moba_kernel, tq=tq),
        grid=(batch, n_q),
        in_specs=[
            pl.BlockSpec(memory_space=pltpu.SMEM),
            pl.BlockSpec((tq, width), lambda b, i: (b * n_q + i, 0)),
            pl.BlockSpec((seq, width), lambda b, i: (b, 0)),
            pl.BlockSpec((1, n_heads, n_q, V_ROWS, tq), lambda b, i: (b, 0, 0, 0, 0)),
            pl.BlockSpec((1, n_q, width), lambda b, i: (b, 0, 0)),
        ],
        out_specs=pl.BlockSpec((tq, width), lambda b, i: (b * n_q + i, 0)),
        out_shape=jax.ShapeDtypeStruct((t, width), BF16),
        scratch_shapes=[pltpu.VMEM((n_streams, V_ROWS, cols), F32), pltpu.VMEM((n_streams, n_q, cols), F32),
                        pltpu.VMEM((1, 2 * tq, cols), F32)],
        compiler_params=_params("parallel", "arbitrary"),
        name="moba",
    )(bound, qa, ka, vta, kmean)


def _swa_kernel(q_ref, kp_ref, kc_ref, vtp_ref, vtc_ref, sink_ref, o_ref):
    i = pl.program_id(1)
    w = WINDOW
    n_sub = q_ref.shape[0] // w
    group = B_HEADS // B_KV_HEADS
    qt = q_ref[...].T
    krow = lax.broadcasted_iota(jnp.int32, (2 * w, w), 0)
    qcol = lax.broadcasted_iota(jnp.int32, (2 * w, w), 1)
    rel = qcol + w - krow
    in_window = (rel >= 0) & (rel < w)
    zeros = jnp.zeros((HEAD_DIM, w), F32)

    def key_block(j):
        return kp_ref[...] if j < 0 else kc_ref[j * w:(j + 1) * w, :]

    def value_block(kv, j):
        return vtp_ref[0, kv, 0] if j < 0 else vtc_ref[0, kv, j]

    def scores(j, kv):
        cols = []
        for g in range(group):
            h = kv * group + g
            qh = qt[h * HEAD_DIM:(h + 1) * HEAD_DIM, j * w:(j + 1) * w]
            cols.append(jnp.concatenate([qh if n == kv else zeros for n in range(B_KV_HEADS)], axis=0))
        kband = jnp.concatenate([key_block(j - 1), key_block(j)], axis=0)
        return _dot(kband, jnp.concatenate(cols, axis=1).astype(BF16))

    jobs = [(j, kv) for j in range(n_sub) for kv in range(B_KV_HEADS)]
    all_scores = [scores(j, kv) for j, kv in jobs]
    outs = {}
    for (j, kv), s in zip(jobs, all_scores):
        ok = in_window & ((krow >= w) | (i > 0)) if j == 0 else in_window
        s = jnp.where(jnp.concatenate([ok] * group, axis=1), s, NEG_INF)
        sink = sink_ref[:, kv * group * w:(kv + 1) * group * w] * LOG2E
        m = jnp.maximum(jnp.max(s, axis=0, keepdims=True), sink)
        p = jnp.exp2(s - m).astype(BF16)
        pv = _dot(value_block(kv, j - 1), p[0:w]) + _dot(value_block(kv, j), p[w:2 * w])
        o = pv[0:HEAD_DIM] / (pv[HEAD_DIM:HEAD_DIM + 1] + jnp.exp2(sink - m))
        for g in range(group):
            outs[(kv * group + g, j)] = o[:, g * w:(g + 1) * w]
    ot = jnp.concatenate([jnp.concatenate([outs[(h, j)] for j in range(n_sub)], axis=1) for h in range(B_HEADS)],
                         axis=0)
    o_ref[...] = ot.T.astype(o_ref.dtype)


def _swa(qb, kb, vtb, sinks, *, batch, seq, n_sub=4):
    t, qw = qb.shape
    w = WINDOW
    nb = seq // w
    steps = nb // n_sub
    prev = lambda b, i: (b * nb + jnp.maximum(i * n_sub - 1, 0), 0)
    cur = lambda b, i: (b * steps + i, 0)
    sink_row = jnp.repeat(sinks.astype(F32), w).reshape(1, B_HEADS * w)
    return pl.pallas_call(
        _swa_kernel,
        grid=(batch, steps),
        in_specs=[pl.BlockSpec((n_sub * w, qw), cur),
                  pl.BlockSpec((w, LANES), prev),
                  pl.BlockSpec((n_sub * w, LANES), cur),
                  pl.BlockSpec((1, B_KV_HEADS, 1, V_ROWS, w),
                               lambda b, i: (b, 0, jnp.maximum(i * n_sub - 1, 0), 0, 0)),
                  pl.BlockSpec((1, B_KV_HEADS, n_sub, V_ROWS, w), lambda b, i: (b, 0, i, 0, 0)),
                  pl.BlockSpec((1, B_HEADS * w), lambda b, i: (0, 0))],
        out_specs=pl.BlockSpec((n_sub * w, qw), cur),
        out_shape=jax.ShapeDtypeStruct((t, qw), BF16),
        compiler_params=_params("parallel", "arbitrary"),
        name="swa",
    )(qb, kb, kb, vtb, vtb, sink_row)


def _diff_kernel(bound_ref, q_ref, k_ref, vt_ref, lam_ref, g_ref, o_ref, acc_scr, s_scr, *, blk, lam_init):
    qi = pl.program_id(2)
    n_heads = LANES // HEAD_DIM
    qt = q_ref[...].T
    feat = lax.broadcasted_iota(jnp.int32, (LANES, blk), 0)
    krow = lax.broadcasted_iota(jnp.int32, (blk, blk), 0)
    qcol = lax.broadcasted_iota(jnp.int32, (blk, blk), 1)

    lq = lam_ref[...]
    lam = (jnp.exp(jnp.sum(lq[0:1] * lq[1:2], axis=1, keepdims=True))
           - jnp.exp(jnp.sum(lq[2:3] * lq[3:4], axis=1, keepdims=True)) + lam_init)

    n_streams = 2 * n_heads
    qts = [jnp.where((feat >= st * C_HALF) & (feat < (st + 1) * C_HALF), qt, 0.0).astype(BF16)
           for st in range(n_streams)]
    acc_scr[...] = jnp.zeros_like(acc_scr)

    def keys(j):
        return k_ref[pl.ds(pl.multiple_of(j * blk, blk), blk), :]

    def attend(bound):
        k_first = keys(0)
        _prime_streams(s_scr, lambda st: _dot(k_first, qts[st]))

        def body(j, ms):
            kj = keys(j)
            kn = keys(j + 1)
            return _rolling_streams(
                n_streams, s_scr, lambda st: _dot(kj, qts[st]), lambda st: _dot(kn, qts[st]),
                lambda st, s: _flash_update(acc_scr, st, s, [[vt_ref[0, st // 2, j]]], ms[st], bound))

        ms = lax.fori_loop(0, qi, body, tuple(jnp.full((1, blk), NEG_INF, F32) for _ in range(n_streams)))
        k_own = keys(qi)
        _rolling_streams(
            n_streams, s_scr, lambda st: _dot(k_own, qts[st]), None,
            lambda st, s: _flash_update(acc_scr, st, jnp.where(krow <= qcol, s, NEG_INF),
                                        [[vt_ref[0, st // 2, qi]]], ms[st], bound))

    _with_score_bound(bound_ref, attend)

    outs = []
    for h in range(n_heads):
        maps = [acc_scr[2 * h + c, 0:HEAD_DIM, :] / acc_scr[2 * h + c, HEAD_DIM:HEAD_DIM + 1, :] for c in range(2)]
        o = maps[0] - lam * maps[1]
        ms_o = jnp.mean(o * o, axis=0, keepdims=True)
        outs.append((o * lax.rsqrt(ms_o + EPS) * g_ref[...]) * (1.0 - lam_init))
    o_ref[...] = jnp.concatenate(outs, axis=0).T.astype(o_ref.dtype)


def _diff(bound, qc, kc, vtc, lam_rows, subln, *, batch, seq, lam_init, blk):
    t, width = qc.shape
    n_pairs = width // LANES
    n_heads = LANES // HEAD_DIM
    n_q = seq // blk
    return pl.pallas_call(
        functools.partial(_diff_kernel, blk=blk, lam_init=lam_init),
        grid=(batch, n_pairs, n_q),
        in_specs=[
            pl.BlockSpec(memory_space=pltpu.SMEM),
            pl.BlockSpec((blk, LANES), lambda b, hp, i: (b * n_q + i, hp)),
            pl.BlockSpec((seq, LANES), lambda b, hp, i: (b, hp)),
            pl.BlockSpec((1, n_heads, n_q, V_ROWS, blk), lambda b, hp, i: (b, hp, 0, 0, 0)),
            pl.BlockSpec((4, C_HALF), lambda b, hp, i: (0, 0)),
            pl.BlockSpec((HEAD_DIM, 1), lambda b, hp, i: (0, 0)),
        ],
        out_specs=pl.BlockSpec((blk, LANES), lambda b, hp, i: (b * n_q + i, hp)),
        out_shape=jax.ShapeDtypeStruct((t, width), BF16),
        scratch_shapes=[pltpu.VMEM((2 * n_heads, V_ROWS, blk), F32),
                        pltpu.VMEM((1, blk, blk), F32)],
        compiler_params=_params("parallel", "parallel", "arbitrary"),
        name="diff",
    )(bound, qc, kc, vtc, lam_rows, subln.reshape(HEAD_DIM, 1))


def _merge_kernel(x_ref, ya_ref, yb_ref, yc_ref, ga_ref, gb_ref, gc_ref, wa_ref, wb_ref, wc_ref, wo_ref, o_ref):
    merged = (jax.nn.sigmoid(ga_ref[...].astype(F32)) * _dot(ya_ref[...], wa_ref[...])
              + jax.nn.sigmoid(gb_ref[...].astype(F32)) * _dot(yb_ref[...], wb_ref[...])
              + jax.nn.sigmoid(gc_ref[...].astype(F32)) * _dot(yc_ref[...], wc_ref[...]))
    o_ref[...] = x_ref[...] + _dot(merged.astype(BF16), wo_ref[...])


def _merge(x2, ya, yb, yc, gates, w_pa, w_pb, w_pc, w_out, *, tm=512):
    t, d = x2.shape
    rows = lambda i: (i, 0)
    return pl.pallas_call(
        _merge_kernel,
        grid=(t // tm,),
        in_specs=[
            pl.BlockSpec((tm, d), rows),
            pl.BlockSpec((tm, ya.shape[1]), rows),
            pl.BlockSpec((tm, yb.shape[1]), rows),
            pl.BlockSpec((tm, yc.shape[1]), rows),
            pl.BlockSpec((tm, d), lambda i: (i, 0)),
            pl.BlockSpec((tm, d), lambda i: (i, 1)),
            pl.BlockSpec((tm, d), lambda i: (i, 2)),
            _resident(w_pa.shape),
            _resident(w_pb.shape),
            _resident(w_pc.shape),
            _resident(w_out.shape),
        ],
        out_specs=pl.BlockSpec((tm, d), rows),
        out_shape=jax.ShapeDtypeStruct((t, d), F32),
        compiler_params=_params("parallel"),
        name="merge",
    )(x2, ya, yb, yc, gates, gates, gates, w_pa, w_pb, w_pc, w_out)


def _convglu_kernel(x_ref, xh_ref, g_ref, wu_ref, cw_ref, cb_ref, wd_ref, o_ref, h_scr, u_scr, act_scr,
                    *, tm, tf, tiles_per_seq):
    i = pl.program_id(0)
    halo = CONV_HALO
    ff = wd_ref.shape[0]
    n_chunks = ff // tf

    def normed(x):
        ms = jnp.mean(x * x, axis=-1, keepdims=True)
        return x * lax.rsqrt(ms + EPS) * g_ref[...]

    keep = jnp.where(i % tiles_per_seq == 0, 0.0, 1.0)
    h_scr[0:halo, :] = (normed(xh_ref[...]) * keep).astype(BF16)
    h_scr[halo:, :] = normed(x_ref[...]).astype(BF16)

    def up(c):
        h = h_scr[...]
        u_scr[c % 2, 0] = _dot(h, wu_ref[:, c * tf:(c + 1) * tf])
        u_scr[c % 2, 1] = _dot(h, wu_ref[:, ff + c * tf:ff + (c + 1) * tf])

    def conv(c, half):
        col0 = half * ff + c * tf
        y = cb_ref[:, col0:col0 + tf]
        for j in range(CONV_W):
            y = y + u_scr[c % 2, half, pl.ds(halo - (CONV_W - 1) + j, tm), :] * cw_ref[j:j + 1, col0:col0 + tf]
        return y

    up(0)
    for c in range(n_chunks):
        if c + 1 < n_chunks:
            up(c + 1)
        gate_u = conv(c, 0)
        val_u = conv(c, 1)
        act_scr[:, c * tf:(c + 1) * tf] = ((gate_u * jax.nn.sigmoid(gate_u)) * val_u).astype(BF16)

    o_ref[...] = x_ref[...] + _dot(act_scr[...], wd_ref[...])


def _convglu(x2, gain, w_up, conv_w, conv_b, w_down, *, seq, tm=512, tf=256):
    t, d = x2.shape
    ff = w_down.shape[0]
    halo = CONV_HALO
    tiles_per_seq = seq // tm
    halo_blocks = tm // halo
    return pl.pallas_call(
        functools.partial(_convglu_kernel, tm=tm, tf=tf, tiles_per_seq=tiles_per_seq),
        grid=(t // tm,),
        in_specs=[
            pl.BlockSpec((tm, d), lambda i: (i, 0)),
            pl.BlockSpec((halo, d), lambda i: (jnp.maximum(i * halo_blocks - 1, 0), 0)),
            _resident((1, d)),
            _resident(w_up.shape),
            _resident(conv_w.shape),
            _resident((1, 2 * ff)),
            _resident(w_down.shape),
        ],
        out_specs=pl.BlockSpec((tm, d), lambda i: (i, 0)),
        out_shape=jax.ShapeDtypeStruct((t, d), F32),
        scratch_shapes=[
            pltpu.VMEM((tm + halo, d), BF16),
            pltpu.VMEM((2, 2, tm + halo, tf), F32),
            pltpu.VMEM((tm, ff), BF16),
        ],
        compiler_params=_params("parallel"),
        name="convglu",
    )(x2, x2, gain.reshape(1, d), w_up, conv_w, conv_b.reshape(1, -1), w_down)


def _rope_tables(seq, dim):
    inv = 1.0 / (ROPE_THETA ** (jnp.arange(0, dim, 2, dtype=F32) / dim))
    ang = jnp.arange(seq, dtype=F32)[:, None] * inv[None, :]
    return jnp.cos(ang), jnp.sin(ang)


def kernel(x, attn_norm, w_in, qn_a, kn_a, qn_b, kn_b, sinks, qn_c, kn_c, lam_q1, lam_k1, lam_q2, lam_k2, subln,
           w_pa, w_pb, w_pc, w_out, mlp_norm, w_up, conv_w, conv_b, w_down):
    batch, seq, d = x.shape
    depth = w_in.shape[0]
    x2 = x.reshape(batch * seq, d)

    def rope_block(dim):
        cos, sin = _rope_tables(seq, dim)
        reps = LANES // dim
        return (jnp.tile(jnp.concatenate([cos, cos], axis=-1), (1, reps)),
                jnp.tile(jnp.concatenate([-sin, sin], axis=-1), (1, reps)))

    tables = rope_block(HEAD_DIM) + rope_block(C_HALF)

    segments = (("qa", A_W), ("ka", A_W), ("va", A_W), ("qb", B_QW), ("kb", B_KVW), ("vb", B_KVW),
                ("qc", C_W), ("kc", C_W), ("vc", C_W), ("gates", N_BRANCH * d))

    for i in range(depth):
        lam_init = 0.8 - 0.6 * float(np.exp(-0.3 * i))
        qk_gains = jnp.stack([jnp.tile(g, LANES // g.shape[0])
                              for g in (qn_a[i], kn_a[i], qn_b[i], kn_b[i], qn_c[i], kn_c[i])]).astype(F32)
        qa, ka, kmean, vta, qb, kb, vtb, qc, kc, vtc, gates = _inproj(
            x2, attn_norm[i], w_in[i].astype(BF16), qk_gains, tables, batch=batch, seq=seq, segments=segments)

        kmean = kmean.reshape(batch, seq // MOBA_BLOCK, A_W)
        ya = _moba(_score_bound(qn_a[i], kn_a[i], HEAD_DIM), qa, ka, vta, kmean, batch=batch, seq=seq)
        yb = _swa(qb, kb, vtb, sinks[i], batch=batch, seq=seq)
        lam_rows = jnp.stack([lam_q1[i], lam_k1[i], lam_q2[i], lam_k2[i]]).astype(F32)
        yc = _diff(_score_bound(qn_c[i], kn_c[i], C_HALF), qc, kc, vtc, lam_rows, subln[i], batch=batch, seq=seq, lam_init=lam_init, blk=DIFF_BLOCK)

        x2 = _merge(x2, ya, yb, yc, gates, w_pa[i].astype(BF16), w_pb[i].astype(BF16), w_pc[i].astype(BF16),
                    w_out[i].astype(BF16))
        x2 = _convglu(x2, mlp_norm[i], w_up[i].astype(BF16), conv_w[i], conv_b[i], w_down[i].astype(BF16), seq=seq)

    return x2.reshape(batch, seq, d)
```

```python
import functools

import numpy as np
import jax
import jax.numpy as jnp
from jax import lax
from jax.experimental import pallas as pl
from jax.experimental.pallas import tpu as pltpu

F32 = jnp.float32
BF16 = jnp.bfloat16
NEG_INF = float("-inf")

LANES = 128
VMEM_LIMIT = 48 * 1024 * 1024

EPS = 1e-6
HEAD_DIM = 64
ROPE_THETA = 10000.0
A_HEADS = 4
MOBA_BLOCK = 256
MOBA_TOPK = 3
B_HEADS = 8
B_KV_HEADS = 2
WINDOW = 128
C_HEADS = 4
C_HALF = HEAD_DIM // 2
N_BRANCH = 3
CONV_W = 3
CONV_HALO = 16
BF16_SUBLANES = 16
V_ROWS = HEAD_DIM + BF16_SUBLANES
DIFF_BLOCK = 512
DIFF_PARKED_KEYS = 256
LOG2E = 1.4426950408889634

A_W = A_HEADS * HEAD_DIM
B_QW = B_HEADS * HEAD_DIM
B_KVW = B_KV_HEADS * HEAD_DIM
C_W = C_HEADS * HEAD_DIM


def _dot(a, b):
    return jnp.dot(a, b, preferred_element_type=F32)


def _split3(a):
    a1 = a.astype(BF16)
    r = a - a1.astype(F32)
    a2 = r.astype(BF16)
    a3 = (r - a2.astype(F32)).astype(BF16)
    return a1, a2, a3


def _resident(shape):
    return pl.BlockSpec(shape, lambda *_: (0,) * len(shape), pipeline_mode=pl.Buffered(1))


def _params(*semantics):
    return pltpu.CompilerParams(dimension_semantics=semantics, vmem_limit_bytes=VMEM_LIMIT)


def _group_sum_matrix(w, hd):
    idx = np.arange(w) // hd
    return jnp.asarray((idx[:, None] == idx[None, :]).astype(np.float32), dtype=BF16)


MM_COLS = 256


def _inproj_kernel(x_ref, g_ref, w_ref, gains_ref, cos64_ref, sin64_ref, cos32_ref, sin32_ref, gs64_ref, gs32_ref,
                   qa_ref, ka_ref, km_ref, vta_ref, qb_ref, kb_ref, vtb_ref, qc_ref, kc_ref, vtc_ref, gates_ref,
                   h_scr, *, blocks, n_qkv):
    x = x_ref[...]
    tm = x.shape[0]
    ms = jnp.mean(x * x, axis=-1, keepdims=True)
    h_scr[...] = (x * lax.rsqrt(ms + EPS) * g_ref[...]).astype(BF16)
    lane = lax.broadcasted_iota(jnp.int32, (tm, LANES), 1)

    def norm_rope(y, gain_row, hd, scale):
        cos_ref, sin_ref, gs_ref = ((cos64_ref, sin64_ref, gs64_ref) if hd == HEAD_DIM
                                    else (cos32_ref, sin32_ref, gs32_ref))
        sq = y * y
        hi = sq.astype(BF16)
        lo = (sq - hi.astype(F32)).astype(BF16)
        msq = (_dot(hi, gs_ref[...]) + _dot(lo, gs_ref[...])) * (1.0 / hd)
        yn = y * lax.rsqrt(msq + EPS) * gains_ref[gain_row:gain_row + 1, :]
        half = hd // 2
        partner = jnp.where(lane % hd < half, pltpu.roll(yn, LANES - half, 1), pltpu.roll(yn, half, 1))
        out = yn * cos_ref[...] + partner * sin_ref[...]
        return out * scale if scale != 1.0 else out

    def store_vt(vt_ref, y, lb):
        tk = vt_ref.shape[-1]
        yt = y.T
        ones = jnp.ones((V_ROWS - HEAD_DIM, tk), F32)
        for hh in range(LANES // HEAD_DIM):
            for n in range(tm // tk):
                blk = yt[hh * HEAD_DIM:(hh + 1) * HEAD_DIM, n * tk:(n + 1) * tk]
                vt_ref[0, lb * (LANES // HEAD_DIM) + hh, n] = jnp.concatenate([blk, ones], axis=0).astype(BF16)

    q_scale = HEAD_DIM ** -0.5 * LOG2E
    c_scale = C_HALF ** -0.5 * LOG2E

    def epilogue(kind, lb, y):
        cols = slice(lb * LANES, (lb + 1) * LANES)
        if kind == "qa":
            qa_ref[:, cols] = norm_rope(y, 0, HEAD_DIM, 1.0)
        elif kind == "ka":
            yk = norm_rope(y, 1, HEAD_DIM, 1.0)
            ka_ref[:, cols] = yk.astype(BF16)
            for n in range(tm // MOBA_BLOCK):
                rows = yk[n * MOBA_BLOCK:(n + 1) * MOBA_BLOCK]
                km_ref[n, :, cols] = jnp.sum(rows, axis=0, keepdims=True) * (1.0 / MOBA_BLOCK)
        elif kind == "va":
            store_vt(vta_ref, y, lb)
        elif kind == "qb":
            qb_ref[:, cols] = norm_rope(y, 2, HEAD_DIM, q_scale)
        elif kind == "kb":
            kb_ref[:, cols] = norm_rope(y, 3, HEAD_DIM, 1.0).astype(BF16)
        elif kind == "vb":
            store_vt(vtb_ref, y, lb)
        elif kind == "qc":
            qc_ref[:, cols] = norm_rope(y, 4, C_HALF, c_scale)
        elif kind == "kc":
            kc_ref[:, cols] = norm_rope(y, 5, C_HALF, 1.0).astype(BF16)
        elif kind == "vc":
            store_vt(vtc_ref, y, lb)

    def chunk(c):
        return _dot(h_scr[...], w_ref[:, c * MM_COLS:(c + 1) * MM_COLS])

    n_chunks = w_ref.shape[1] // MM_COLS
    y_next = chunk(0)
    for c in range(n_chunks):
        y = y_next
        if c + 1 < n_chunks:
            y_next = chunk(c + 1)
        col0 = c * MM_COLS
        if col0 < n_qkv:
            for part in range(MM_COLS // LANES):
                kind, lb = blocks[col0 // LANES + part]
                epilogue(kind, lb, y[:, part * LANES:(part + 1) * LANES])
        else:
            gates_ref[:, col0 - n_qkv:col0 - n_qkv + MM_COLS] = y.astype(gates_ref.dtype)


def _inproj(x2, gain, w_bf, qk_gains, tables, *, batch, seq, segments, tm=512):
    t, d = x2.shape
    n = w_bf.shape[1]
    blocks = [(name, lb) for name, width in segments[:-1] for lb in range(width // LANES)]
    n_qkv = len(blocks) * LANES
    width = dict(segments)
    tps = seq // tm
    rows = lambda i: (i, 0)
    pos = lambda i: (i % tps, 0)

    def vt_spec(heads, tk):
        return pl.BlockSpec((1, heads, tm // tk, V_ROWS, tk), lambda i: (i // tps, 0, i % tps, 0, 0))

    def vt_shape(heads, tk):
        return jax.ShapeDtypeStruct((batch, heads, seq // tk, V_ROWS, tk), BF16)

    n_mb = tm // MOBA_BLOCK
    out_specs = [
        pl.BlockSpec((tm, width["qa"]), rows),
        pl.BlockSpec((tm, width["ka"]), rows),
        pl.BlockSpec((n_mb, 1, width["ka"]), lambda i: (i, 0, 0)),
        vt_spec(A_HEADS, MOBA_BLOCK),
        pl.BlockSpec((tm, width["qb"]), rows),
        pl.BlockSpec((tm, width["kb"]), rows),
        vt_spec(B_KV_HEADS, WINDOW),
        pl.BlockSpec((tm, width["qc"]), rows),
        pl.BlockSpec((tm, width["kc"]), rows),
        vt_spec(C_HEADS, DIFF_BLOCK),
        pl.BlockSpec((tm, width["gates"]), rows),
    ]
    out_shape = [
        jax.ShapeDtypeStruct((t, width["qa"]), F32),
        jax.ShapeDtypeStruct((t, width["ka"]), BF16),
        jax.ShapeDtypeStruct((t // MOBA_BLOCK, 1, width["ka"]), F32),
        vt_shape(A_HEADS, MOBA_BLOCK),
        jax.ShapeDtypeStruct((t, width["qb"]), F32),
        jax.ShapeDtypeStruct((t, width["kb"]), BF16),
        vt_shape(B_KV_HEADS, WINDOW),
        jax.ShapeDtypeStruct((t, width["qc"]), F32),
        jax.ShapeDtypeStruct((t, width["kc"]), BF16),
        vt_shape(C_HEADS, DIFF_BLOCK),
        jax.ShapeDtypeStruct((t, width["gates"]), BF16),
    ]
    table_spec = pl.BlockSpec((tm, LANES), pos)
    return pl.pallas_call(
        functools.partial(_inproj_kernel, blocks=blocks, n_qkv=n_qkv),
        grid=(t // tm,),
        in_specs=[pl.BlockSpec((tm, d), rows), _resident((1, d)), _resident((d, n)), _resident(qk_gains.shape),
                  table_spec, table_spec, table_spec, table_spec,
                  _resident((LANES, LANES)), _resident((LANES, LANES))],
        out_specs=out_specs,
        out_shape=out_shape,
        scratch_shapes=[pltpu.VMEM((tm, d), BF16)],
        compiler_params=_params("parallel"),
        name="inproj",
    )(x2, gain.reshape(1, d), w_bf, qk_gains, *tables, _group_sum_matrix(LANES, HEAD_DIM),
      _group_sum_matrix(LANES, C_HALF))


def _flash_update(acc_ref, stream, s, vt_groups, m, bound=None):
    if bound is None:
        m_new = jnp.maximum(m, jnp.max(s, axis=0, keepdims=True))
        shift = jnp.where(m_new == NEG_INF, 0.0, m_new)
        alpha = jnp.exp2(m - shift)
    else:
        m_new, shift, alpha = m, bound, None
    p = jnp.exp2(s - shift).astype(BF16)
    cw = p.shape[1] // len(vt_groups)
    pvs = []
    for g, vts in enumerate(vt_groups):
        tk = vts[0].shape[1]
        pv = _dot(vts[0], p[0:tk, g * cw:(g + 1) * cw])
        for n in range(1, len(vts)):
            pv = pv + _dot(vts[n], p[n * tk:(n + 1) * tk, g * cw:(g + 1) * cw])
        pvs.append(pv)
    pv = pvs[0] if len(pvs) == 1 else jnp.concatenate(pvs, axis=1)
    acc_ref[stream] = acc_ref[stream] + pv if alpha is None else alpha * acc_ref[stream] + pv
    return m_new


MAX_FIXED_SHIFT = 60.0


def _score_bound(q_gain, k_gain, hd):
    return (jnp.max(jnp.abs(q_gain)) * jnp.max(jnp.abs(k_gain)) * (hd ** 0.5 * LOG2E)).reshape(1).astype(F32)


def _with_score_bound(bound_ref, attend):
    bound = bound_ref[0]
    fixed = bound <= MAX_FIXED_SHIFT

    @pl.when(fixed)
    def _():
        attend(bound)

    @pl.when(jnp.logical_not(fixed))
    def _():
        attend(None)


def _rolling_streams(n_streams, n_keys, s_scr, scores, next_scores, update):
    parked = s_scr.shape[0]
    out = []
    s = s_scr[...]
    if parked < n_keys:
        s = jnp.concatenate([s, scores(0, slice(parked, None))], axis=0)
    for st in range(n_streams):
        if st + 1 < n_streams:
            s_next = scores(st + 1, slice(None))
        elif next_scores is not None:
            s_scr[...] = next_scores(0, slice(0, parked))
        out.append(update(st, s))
        if st + 1 < n_streams:
            s = s_next
    return tuple(out)


def _prime_streams(s_scr, scores):
    s_scr[...] = scores(0, slice(0, s_scr.shape[0]))


def _moba_kernel(bound_ref, q_ref, k_ref, vt_ref, km_ref, o_ref, acc_scr, sel_scr, s_scr, *, tq):
    qi = pl.program_id(1)
    n_heads = q_ref.shape[1] // HEAD_DIM
    n_blk = km_ref.shape[1]
    per_block = LANES // HEAD_DIM
    qt = q_ref[...].T
    feat = lax.broadcasted_iota(jnp.int32, (LANES, tq), 0)
    blk = lax.broadcasted_iota(jnp.int32, (n_blk, tq), 0)
    blk_f = blk.astype(F32)
    krow = lax.broadcasted_iota(jnp.int32, (tq, per_block * tq), 0)
    qcol = lax.broadcasted_iota(jnp.int32, (tq, per_block * tq), 1) % tq
    causal = krow <= qcol

    n_streams = n_heads // per_block

    def lane_block(x, b):
        return x[:, b * LANES:(b + 1) * LANES]

    qhs = []
    for h in range(n_heads):
        b, hh = divmod(h, per_block)
        qhs.append(jnp.where((feat >= hh * HEAD_DIM) & (feat < (hh + 1) * HEAD_DIM),
                             qt[b * LANES:(b + 1) * LANES], 0.0))
    qts = [jnp.concatenate([(qh * (HEAD_DIM ** -0.5 * LOG2E)).astype(BF16)
                            for qh in qhs[b * per_block:(b + 1) * per_block]], axis=1) for b in range(n_streams)]

    def pair_keys(t):
        return k_ref[pl.ds(pl.multiple_of(t * 2 * tq, 2 * tq), 2 * tq), :]

    k_first = pair_keys(0)
    _prime_streams(s_scr, lambda b, rows: _dot(lane_block(k_first[rows], b), qts[b]))

    sels = []
    for h in range(n_heads):
        b, hh = divmod(h, per_block)
        if hh == 0:
            km_parts = _split3(km_ref[0, :, b * LANES:(b + 1) * LANES])

        k1, k2, k3 = km_parts
        q1, q2, q3 = _split3(qhs[h])
        gate = (_dot(k1, q1) + _dot(k2, q1) + _dot(k1, q2) + _dot(k3, q1) + _dot(k2, q2) + _dot(k1, q3))
        g = jnp.where(blk < qi, gate, NEG_INF)
        sel = jnp.zeros((n_blk, tq), F32)
        for _ in range(MOBA_TOPK):
            mx = jnp.max(g, axis=0, keepdims=True)
            first = jnp.min(jnp.where((g == mx) & (mx > NEG_INF), blk_f, float(n_blk)), axis=0, keepdims=True)
            pick = blk_f == first
            sel = jnp.where(pick, 1.0, sel)
            g = jnp.where(pick, NEG_INF, g)
        sels.append(sel)
        if hh == per_block - 1:
            sel_scr[b] = jnp.concatenate(sels, axis=1)
            sels = []

    acc_scr[...] = jnp.zeros_like(acc_scr)

    def pair_values(b, t):
        return [[vt_ref[0, b * per_block + hh, 2 * t], vt_ref[0, b * per_block + hh, 2 * t + 1]]
                for hh in range(per_block)]

    def masked(s, first, second):
        return jnp.concatenate([jnp.where(first, s[0:tq], NEG_INF), jnp.where(second, s[tq:2 * tq], NEG_INF)], axis=0)

    last = qi // 2
    own_second = qi % 2 == 1

    def attend(bound):
        def body(t, ms):
            kj = pair_keys(t)
            kn = pair_keys(t + 1)

            def update(b, s):
                first = sel_scr[b, pl.ds(2 * t, 1), :] > 0.0
                second = sel_scr[b, pl.ds(2 * t + 1, 1), :] > 0.0
                return _flash_update(acc_scr, b, masked(s, first, second), pair_values(b, t), ms[b], bound)

            return _rolling_streams(n_streams, 2 * tq, s_scr,
                                    lambda b, rows: _dot(lane_block(kj[rows], b), qts[b]),
                                    lambda b, rows: _dot(lane_block(kn[rows], b), qts[b]), update)

        ms = lax.fori_loop(0, last, body,
                           tuple(jnp.full((1, per_block * tq), NEG_INF, F32) for _ in range(n_streams)))
        k_last = pair_keys(last)

        def update_last(b, s):
            picked = sel_scr[b, pl.ds(2 * last, 1), :] > 0.0
            first = (own_second & picked) | (jnp.logical_not(own_second) & causal)
            second = own_second & causal
            return _flash_update(acc_scr, b, masked(s, first, second), pair_values(b, last), ms[b], bound)

        _rolling_streams(n_streams, 2 * tq, s_scr, lambda b, rows: _dot(lane_block(k_last[rows], b), qts[b]), None,
                         update_last)

    _with_score_bound(bound_ref, attend)

    outs = []
    for b in range(n_streams):
        acc = acc_scr[b]
        o = acc[0:HEAD_DIM] / acc[HEAD_DIM:HEAD_DIM + 1]
        outs += [o[:, hh * tq:(hh + 1) * tq] for hh in range(per_block)]
    o_ref[...] = jnp.concatenate(outs, axis=0).T.astype(o_ref.dtype)


def _moba(bound, qa, ka, vta, kmean, *, batch, seq):
    tq = MOBA_BLOCK
    t, width = qa.shape
    n_heads = width // HEAD_DIM
    n_q = seq // tq
    n_streams = width // LANES
    cols = (LANES // HEAD_DIM) * tq
    return pl.pallas_call(
        functools.partial(_moba_kernel, tq=tq),
        grid=(batch, n_q),
        in_specs=[
            pl.BlockSpec(memory_space=pltpu.SMEM),
            pl.BlockSpec((tq, width), lambda b, i: (b * n_q + i, 0)),
            pl.BlockSpec((seq, width), lambda b, i: (b, 0)),
            pl.BlockSpec((1, n_heads, n_q, V_ROWS, tq), lambda b, i: (b, 0, 0, 0, 0)),
            pl.BlockSpec((1, n_q, width), lambda b, i: (b, 0, 0)),
        ],
        out_specs=pl.BlockSpec((tq, width), lambda b, i: (b * n_q + i, 0)),
        out_shape=jax.ShapeDtypeStruct((t, width), BF16),
        scratch_shapes=[pltpu.VMEM((n_streams, V_ROWS, cols), F32), pltpu.VMEM((n_streams, n_q, cols), F32),
                        pltpu.VMEM((2 * tq, cols), F32)],
        compiler_params=_params("parallel", "arbitrary"),
        name="moba",
    )(bound, qa, ka, vta, kmean)


def _swa_kernel(q_ref, kp_ref, kc_ref, vtp_ref, vtc_ref, sink_ref, o_ref):
    i = pl.program_id(1)
    w = WINDOW
    n_sub = q_ref.shape[0] // w
    group = B_HEADS // B_KV_HEADS
    qt = q_ref[...].T
    krow = lax.broadcasted_iota(jnp.int32, (2 * w, w), 0)
    qcol = lax.broadcasted_iota(jnp.int32, (2 * w, w), 1)
    rel = qcol + w - krow
    in_window = (rel >= 0) & (rel < w)
    zeros = jnp.zeros((HEAD_DIM, w), F32)

    def key_block(j):
        return kp_ref[...] if j < 0 else kc_ref[j * w:(j + 1) * w, :]

    def value_block(kv, j):
        return vtp_ref[0, kv, 0] if j < 0 else vtc_ref[0, kv, j]

    def scores(j, kv):
        cols = []
        for g in range(group):
            h = kv * group + g
            qh = qt[h * HEAD_DIM:(h + 1) * HEAD_DIM, j * w:(j + 1) * w]
            cols.append(jnp.concatenate([qh if n == kv else zeros for n in range(B_KV_HEADS)], axis=0))
        kband = jnp.concatenate([key_block(j - 1), key_block(j)], axis=0)
        return _dot(kband, jnp.concatenate(cols, axis=1).astype(BF16))

    jobs = [(j, kv) for j in range(n_sub) for kv in range(B_KV_HEADS)]
    all_scores = [scores(j, kv) for j, kv in jobs]
    outs = {}
    for (j, kv), s in zip(jobs, all_scores):
        ok = in_window & ((krow >= w) | (i > 0)) if j == 0 else in_window
        s = jnp.where(jnp.concatenate([ok] * group, axis=1), s, NEG_INF)
        sink = sink_ref[:, kv * group * w:(kv + 1) * group * w] * LOG2E
        m = jnp.maximum(jnp.max(s, axis=0, keepdims=True), sink)
        p = jnp.exp2(s - m).astype(BF16)
        pv = _dot(value_block(kv, j - 1), p[0:w]) + _dot(value_block(kv, j), p[w:2 * w])
        o = pv[0:HEAD_DIM] / (pv[HEAD_DIM:HEAD_DIM + 1] + jnp.exp2(sink - m))
        for g in range(group):
            outs[(kv * group + g, j)] = o[:, g * w:(g + 1) * w]
    ot = jnp.concatenate([jnp.concatenate([outs[(h, j)] for j in range(n_sub)], axis=1) for h in range(B_HEADS)],
                         axis=0)
    o_ref[...] = ot.T.astype(o_ref.dtype)


def _swa(qb, kb, vtb, sinks, *, batch, seq, n_sub=4):
    t, qw = qb.shape
    w = WINDOW
    nb = seq // w
    steps = nb // n_sub
    prev = lambda b, i: (b * nb + jnp.maximum(i * n_sub - 1, 0), 0)
    cur = lambda b, i: (b * steps + i, 0)
    sink_row = jnp.repeat(sinks.astype(F32), w).reshape(1, B_HEADS * w)
    return pl.pallas_call(
        _swa_kernel,
        grid=(batch, steps),
        in_specs=[pl.BlockSpec((n_sub * w, qw), cur),
                  pl.BlockSpec((w, LANES), prev),
                  pl.BlockSpec((n_sub * w, LANES), cur),
                  pl.BlockSpec((1, B_KV_HEADS, 1, V_ROWS, w),
                               lambda b, i: (b, 0, jnp.maximum(i * n_sub - 1, 0), 0, 0)),
                  pl.BlockSpec((1, B_KV_HEADS, n_sub, V_ROWS, w), lambda b, i: (b, 0, i, 0, 0)),
                  pl.BlockSpec((1, B_HEADS * w), lambda b, i: (0, 0))],
        out_specs=pl.BlockSpec((n_sub * w, qw), cur),
        out_shape=jax.ShapeDtypeStruct((t, qw), BF16),
        compiler_params=_params("parallel", "arbitrary"),
        name="swa",
    )(qb, kb, kb, vtb, vtb, sink_row)


def _diff_kernel(bound_ref, q_ref, k_ref, vt_ref, lam_ref, g_ref, o_ref, acc_scr, s_scr, *, blk, lam_init):
    qi = pl.program_id(2)
    n_heads = LANES // HEAD_DIM
    qt = q_ref[...].T
    feat = lax.broadcasted_iota(jnp.int32, (LANES, blk), 0)
    krow = lax.broadcasted_iota(jnp.int32, (blk, blk), 0)
    qcol = lax.broadcasted_iota(jnp.int32, (blk, blk), 1)

    lq = lam_ref[...]
    lam = (jnp.exp(jnp.sum(lq[0:1] * lq[1:2], axis=1, keepdims=True))
           - jnp.exp(jnp.sum(lq[2:3] * lq[3:4], axis=1, keepdims=True)) + lam_init)

    n_streams = 2 * n_heads
    qts = [jnp.where((feat >= st * C_HALF) & (feat < (st + 1) * C_HALF), qt, 0.0).astype(BF16)
           for st in range(n_streams)]
    acc_scr[...] = jnp.zeros_like(acc_scr)

    def keys(j):
        return k_ref[pl.ds(pl.multiple_of(j * blk, blk), blk), :]

    k_first = keys(0)
    _prime_streams(s_scr, lambda st, rows: _dot(k_first[rows], qts[st]))

    def attend(bound):
        def body(j, ms):
            kj = keys(j)
            kn = keys(j + 1)
            return _rolling_streams(
                n_streams, blk, s_scr,
                lambda st, rows: _dot(kj[rows], qts[st]), lambda st, rows: _dot(kn[rows], qts[st]),
                lambda st, s: _flash_update(acc_scr, st, s, [[vt_ref[0, st // 2, j]]], ms[st], bound))

        ms = lax.fori_loop(0, qi, body, tuple(jnp.full((1, blk), NEG_INF, F32) for _ in range(n_streams)))
        k_own = keys(qi)
        _rolling_streams(
            n_streams, blk, s_scr, lambda st, rows: _dot(k_own[rows], qts[st]), None,
            lambda st, s: _flash_update(acc_scr, st, jnp.where(krow <= qcol, s, NEG_INF),
                                        [[vt_ref[0, st // 2, qi]]], ms[st], bound))

    _with_score_bound(bound_ref, attend)

    outs = []
    for h in range(n_heads):
        maps = [acc_scr[2 * h + c, 0:HEAD_DIM, :] / acc_scr[2 * h + c, HEAD_DIM:HEAD_DIM + 1, :] for c in range(2)]
        o = maps[0] - lam * maps[1]
        ms_o = jnp.mean(o * o, axis=0, keepdims=True)
        outs.append((o * lax.rsqrt(ms_o + EPS) * g_ref[...]) * (1.0 - lam_init))
    o_ref[...] = jnp.concatenate(outs, axis=0).T.astype(o_ref.dtype)


def _diff(bound, qc, kc, vtc, lam_rows, subln, *, batch, seq, lam_init, blk):
    t, width = qc.shape
    n_pairs = width // LANES
    n_heads = LANES // HEAD_DIM
    n_q = seq // blk
    return pl.pallas_call(
        functools.partial(_diff_kernel, blk=blk, lam_init=lam_init),
        grid=(batch, n_pairs, n_q),
        in_specs=[
            pl.BlockSpec(memory_space=pltpu.SMEM),
            pl.BlockSpec((blk, LANES), lambda b, hp, i: (b * n_q + i, hp)),
            pl.BlockSpec((seq, LANES), lambda b, hp, i: (b, hp)),
            pl.BlockSpec((1, n_heads, n_q, V_ROWS, blk), lambda b, hp, i: (b, hp, 0, 0, 0)),
            pl.BlockSpec((4, C_HALF), lambda b, hp, i: (0, 0)),
            pl.BlockSpec((HEAD_DIM, 1), lambda b, hp, i: (0, 0)),
        ],
        out_specs=pl.BlockSpec((blk, LANES), lambda b, hp, i: (b * n_q + i, hp)),
        out_shape=jax.ShapeDtypeStruct((t, width), BF16),
        scratch_shapes=[pltpu.VMEM((2 * n_heads, V_ROWS, blk), F32),
                        pltpu.VMEM((DIFF_PARKED_KEYS, blk), F32)],
        compiler_params=_params("parallel", "parallel", "arbitrary"),
        name="diff",
    )(bound, qc, kc, vtc, lam_rows, subln.reshape(HEAD_DIM, 1))


def _merge_kernel(x_ref, ya_ref, yb_ref, yc_ref, ga_ref, gb_ref, gc_ref, wa_ref, wb_ref, wc_ref, wo_ref, o_ref):
    merged = (jax.nn.sigmoid(ga_ref[...].astype(F32)) * _dot(ya_ref[...], wa_ref[...])
              + jax.nn.sigmoid(gb_ref[...].astype(F32)) * _dot(yb_ref[...], wb_ref[...])
              + jax.nn.sigmoid(gc_ref[...].astype(F32)) * _dot(yc_ref[...], wc_ref[...]))
    o_ref[...] = x_ref[...] + _dot(merged.astype(BF16), wo_ref[...])


def _merge(x2, ya, yb, yc, gates, w_pa, w_pb, w_pc, w_out, *, tm=512):
    t, d = x2.shape
    rows = lambda i: (i, 0)
    return pl.pallas_call(
        _merge_kernel,
        grid=(t // tm,),
        in_specs=[
            pl.BlockSpec((tm, d), rows),
            pl.BlockSpec((tm, ya.shape[1]), rows),
            pl.BlockSpec((tm, yb.shape[1]), rows),
            pl.BlockSpec((tm, yc.shape[1]), rows),
            pl.BlockSpec((tm, d), lambda i: (i, 0)),
            pl.BlockSpec((tm, d), lambda i: (i, 1)),
            pl.BlockSpec((tm, d), lambda i: (i, 2)),
            _resident(w_pa.shape),
            _resident(w_pb.shape),
            _resident(w_pc.shape),
            _resident(w_out.shape),
        ],
        out_specs=pl.BlockSpec((tm, d), rows),
        out_shape=jax.ShapeDtypeStruct((t, d), F32),
        compiler_params=_params("parallel"),
        name="merge",
    )(x2, ya, yb, yc, gates, gates, gates, w_pa, w_pb, w_pc, w_out)


def _convglu_kernel(x_ref, xh_ref, g_ref, wu_ref, cw_ref, cb_ref, wd_ref, o_ref, h_scr, u_scr, act_scr,
                    *, tm, tf, tiles_per_seq):
    i = pl.program_id(0)
    halo = CONV_HALO
    ff = wd_ref.shape[0]
    n_chunks = ff // tf

    def normed(x):
        ms = jnp.mean(x * x, axis=-1, keepdims=True)
        return x * lax.rsqrt(ms + EPS) * g_ref[...]

    keep = jnp.where(i % tiles_per_seq == 0, 0.0, 1.0)
    h_scr[0:halo, :] = (normed(xh_ref[...]) * keep).astype(BF16)
    h_scr[halo:, :] = normed(x_ref[...]).astype(BF16)

    def up(c):
        h = h_scr[...]
        u_scr[c % 2, 0] = _dot(h, wu_ref[:, c * tf:(c + 1) * tf])
        u_scr[c % 2, 1] = _dot(h, wu_ref[:, ff + c * tf:ff + (c + 1) * tf])

    def conv(c, half):
        col0 = half * ff + c * tf
        y = cb_ref[:, col0:col0 + tf]
        for j in range(CONV_W):
            y = y + u_scr[c % 2, half, pl.ds(halo - (CONV_W - 1) + j, tm), :] * cw_ref[j:j + 1, col0:col0 + tf]
        return y

    up(0)
    for c in range(n_chunks):
        if c + 1 < n_chunks:
            up(c + 1)
        gate_u = conv(c, 0)
        val_u = conv(c, 1)
        act_scr[:, c * tf:(c + 1) * tf] = ((gate_u * jax.nn.sigmoid(gate_u)) * val_u).astype(BF16)

    o_ref[...] = x_ref[...] + _dot(act_scr[...], wd_ref[...])


def _convglu(x2, gain, w_up, conv_w, conv_b, w_down, *, seq, tm=512, tf=256):
    t, d = x2.shape
    ff = w_down.shape[0]
    halo = CONV_HALO
    tiles_per_seq = seq // tm
    halo_blocks = tm // halo
    return pl.pallas_call(
        functools.partial(_convglu_kernel, tm=tm, tf=tf, tiles_per_seq=tiles_per_seq),
        grid=(t // tm,),
        in_specs=[
            pl.BlockSpec((tm, d), lambda i: (i, 0)),
            pl.BlockSpec((halo, d), lambda i: (jnp.maximum(i * halo_blocks - 1, 0), 0)),
            _resident((1, d)),
            _resident(w_up.shape),
            _resident(conv_w.shape),
            _resident((1, 2 * ff)),
            _resident(w_down.shape),
        ],
        out_specs=pl.BlockSpec((tm, d), lambda i: (i, 0)),
        out_shape=jax.ShapeDtypeStruct((t, d), F32),
        scratch_shapes=[
            pltpu.VMEM((tm + halo, d), BF16),
            pltpu.VMEM((2, 2, tm + halo, tf), F32),
            pltpu.VMEM((tm, ff), BF16),
        ],
        compiler_params=_params("parallel"),
        name="convglu",
    )(x2, x2, gain.reshape(1, d), w_up, conv_w, conv_b.reshape(1, -1), w_down)


def _rope_tables(seq, dim):
    inv = 1.0 / (ROPE_THETA ** (jnp.arange(0, dim, 2, dtype=F32) / dim))
    ang = jnp.arange(seq, dtype=F32)[:, None] * inv[None, :]
    return jnp.cos(ang), jnp.sin(ang)


def kernel(x, attn_norm, w_in, qn_a, kn_a, qn_b, kn_b, sinks, qn_c, kn_c, lam_q1, lam_k1, lam_q2, lam_k2, subln,
           w_pa, w_pb, w_pc, w_out, mlp_norm, w_up, conv_w, conv_b, w_down):
    batch, seq, d = x.shape
    depth = w_in.shape[0]
    x2 = x.reshape(batch * seq, d)

    def rope_block(dim):
        cos, sin = _rope_tables(seq, dim)
        reps = LANES // dim
        return (jnp.tile(jnp.concatenate([cos, cos], axis=-1), (1, reps)),
                jnp.tile(jnp.concatenate([-sin, sin], axis=-1), (1, reps)))

    tables = rope_block(HEAD_DIM) + rope_block(C_HALF)

    segments = (("qa", A_W), ("ka", A_W), ("va", A_W), ("qb", B_QW), ("kb", B_KVW), ("vb", B_KVW),
                ("qc", C_W), ("kc", C_W), ("vc", C_W), ("gates", N_BRANCH * d))

    for i in range(depth):
        lam_init = 0.8 - 0.6 * float(np.exp(-0.3 * i))
        qk_gains = jnp.stack([jnp.tile(g, LANES // g.shape[0])
                              for g in (qn_a[i], kn_a[i], qn_b[i], kn_b[i], qn_c[i], kn_c[i])]).astype(F32)
        qa, ka, kmean, vta, qb, kb, vtb, qc, kc, vtc, gates = _inproj(
            x2, attn_norm[i], w_in[i].astype(BF16), qk_gains, tables, batch=batch, seq=seq, segments=segments)

        kmean = kmean.reshape(batch, seq // MOBA_BLOCK, A_W)
        ya = _moba(_score_bound(qn_a[i], kn_a[i], HEAD_DIM), qa, ka, vta, kmean, batch=batch, seq=seq)
        yb = _swa(qb, kb, vtb, sinks[i], batch=batch, seq=seq)
        lam_rows = jnp.stack([lam_q1[i], lam_k1[i], lam_q2[i], lam_k2[i]]).astype(F32)
        yc = _diff(_score_bound(qn_c[i], kn_c[i], C_HALF), qc, kc, vtc, lam_rows, subln[i], batch=batch, seq=seq, lam_init=lam_init, blk=DIFF_BLOCK)

        x2 = _merge(x2, ya, yb, yc, gates, w_pa[i].astype(BF16), w_pb[i].astype(BF16), w_pc[i].astype(BF16),
                    w_out[i].astype(BF16))
        x2 = _convglu(x2, mlp_norm[i], w_up[i].astype(BF16), conv_w[i], conv_b[i], w_down[i].astype(BF16), seq=seq)

    return x2.reshape(batch, seq, d)
```

```python
import functools

import numpy as np
import jax
import jax.numpy as jnp
from jax import lax
from jax.experimental import pallas as pl
from jax.experimental.pallas import tpu as pltpu

F32 = jnp.float32
BF16 = jnp.bfloat16
NEG_INF = float("-inf")

LANES = 128
VMEM_LIMIT = 48 * 1024 * 1024

EPS = 1e-6
HEAD_DIM = 64
ROPE_THETA = 10000.0
A_HEADS = 4
MOBA_BLOCK = 256
MOBA_TOPK = 3
B_HEADS = 8
B_KV_HEADS = 2
WINDOW = 128
C_HEADS = 4
C_HALF = HEAD_DIM // 2
N_BRANCH = 3
CONV_W = 3
CONV_HALO = 16
BF16_SUBLANES = 16
V_ROWS = HEAD_DIM + BF16_SUBLANES
DIFF_BLOCK = 512
DIFF_PARKED_KEYS = 256
LOG2E = 1.4426950408889634

A_W = A_HEADS * HEAD_DIM
B_QW = B_HEADS * HEAD_DIM
B_KVW = B_KV_HEADS * HEAD_DIM
C_W = C_HEADS * HEAD_DIM


def _dot(a, b):
    return jnp.dot(a, b, preferred_element_type=F32)


def _split3(a):
    a1 = a.astype(BF16)
    r = a - a1.astype(F32)
    a2 = r.astype(BF16)
    a3 = (r - a2.astype(F32)).astype(BF16)
    return a1, a2, a3


def _resident(shape):
    return pl.BlockSpec(shape, lambda *_: (0,) * len(shape), pipeline_mode=pl.Buffered(1))


def _params(*semantics):
    return pltpu.CompilerParams(dimension_semantics=semantics, vmem_limit_bytes=VMEM_LIMIT)


def _group_sum_matrix(w, hd):
    idx = np.arange(w) // hd
    same = (idx[:, None] == idx[None, :]).astype(np.float32)
    return jnp.asarray(np.concatenate([same, same], axis=0), dtype=BF16)


MM_COLS = 256


def _inproj_kernel(x_ref, g_ref, w_ref, gains_ref, cos64_ref, sin64_ref, cos32_ref, sin32_ref, gs64_ref, gs32_ref,
                   qa_ref, ka_ref, km_ref, vta_ref, qb_ref, kb_ref, vtb_ref, qc_ref, kc_ref, vtc_ref, gates_ref,
                   h_scr, *, blocks, n_qkv):
    x = x_ref[...]
    tm = x.shape[0]
    ms = jnp.mean(x * x, axis=-1, keepdims=True)
    h_scr[...] = (x * lax.rsqrt(ms + EPS) * g_ref[...]).astype(BF16)
    lane = lax.broadcasted_iota(jnp.int32, (tm, LANES), 1)

    def norm_rope(y, gain_row, hd, scale):
        cos_ref, sin_ref, gs_ref = ((cos64_ref, sin64_ref, gs64_ref) if hd == HEAD_DIM
                                    else (cos32_ref, sin32_ref, gs32_ref))
        sq = y * y
        hi = sq.astype(BF16)
        lo = (sq - hi.astype(F32)).astype(BF16)
        msq = _dot(jnp.concatenate([hi, lo], axis=1), gs_ref[...]) * (1.0 / hd)
        yn = y * lax.rsqrt(msq + EPS) * gains_ref[gain_row:gain_row + 1, :]
        half = hd // 2
        partner = jnp.where(lane % hd < half, pltpu.roll(yn, LANES - half, 1), pltpu.roll(yn, half, 1))
        out = yn * cos_ref[...] + partner * sin_ref[...]
        return out * scale if scale != 1.0 else out

    def store_vt(vt_ref, y, lb):
        tk = vt_ref.shape[-1]
        yt = y.T
        ones = jnp.ones((V_ROWS - HEAD_DIM, tk), F32)
        for hh in range(LANES // HEAD_DIM):
            for n in range(tm // tk):
                blk = yt[hh * HEAD_DIM:(hh + 1) * HEAD_DIM, n * tk:(n + 1) * tk]
                vt_ref[0, lb * (LANES // HEAD_DIM) + hh, n] = jnp.concatenate([blk, ones], axis=0).astype(BF16)

    q_scale = HEAD_DIM ** -0.5 * LOG2E
    c_scale = C_HALF ** -0.5 * LOG2E

    def epilogue(kind, lb, y):
        cols = slice(lb * LANES, (lb + 1) * LANES)
        if kind == "qa":
            qa_ref[:, cols] = norm_rope(y, 0, HEAD_DIM, 1.0)
        elif kind == "ka":
            yk = norm_rope(y, 1, HEAD_DIM, 1.0)
            ka_ref[:, cols] = yk.astype(BF16)
            for n in range(tm // MOBA_BLOCK):
                rows = yk[n * MOBA_BLOCK:(n + 1) * MOBA_BLOCK]
                km_ref[n, :, cols] = jnp.sum(rows, axis=0, keepdims=True) * (1.0 / MOBA_BLOCK)
        elif kind == "va":
            store_vt(vta_ref, y, lb)
        elif kind == "qb":
            qb_ref[:, cols] = norm_rope(y, 2, HEAD_DIM, q_scale)
        elif kind == "kb":
            kb_ref[:, cols] = norm_rope(y, 3, HEAD_DIM, 1.0).astype(BF16)
        elif kind == "vb":
            store_vt(vtb_ref, y, lb)
        elif kind == "qc":
            qc_ref[:, cols] = norm_rope(y, 4, C_HALF, c_scale)
        elif kind == "kc":
            kc_ref[:, cols] = norm_rope(y, 5, C_HALF, 1.0).astype(BF16)
        elif kind == "vc":
            store_vt(vtc_ref, y, lb)

    def chunk(c):
        return _dot(h_scr[...], w_ref[:, c * MM_COLS:(c + 1) * MM_COLS])

    n_chunks = w_ref.shape[1] // MM_COLS
    y_next = chunk(0)
    for c in range(n_chunks):
        y = y_next
        if c + 1 < n_chunks:
            y_next = chunk(c + 1)
        col0 = c * MM_COLS
        if col0 < n_qkv:
            for part in range(MM_COLS // LANES):
                kind, lb = blocks[col0 // LANES + part]
                epilogue(kind, lb, y[:, part * LANES:(part + 1) * LANES])
        else:
            gates_ref[:, col0 - n_qkv:col0 - n_qkv + MM_COLS] = y.astype(gates_ref.dtype)


def _inproj(x2, gain, w_bf, qk_gains, tables, *, batch, seq, segments, tm=512):
    t, d = x2.shape
    n = w_bf.shape[1]
    blocks = [(name, lb) for name, width in segments[:-1] for lb in range(width // LANES)]
    n_qkv = len(blocks) * LANES
    width = dict(segments)
    tps = seq // tm
    rows = lambda i: (i, 0)
    pos = lambda i: (i % tps, 0)

    def vt_spec(heads, tk):
        return pl.BlockSpec((1, heads, tm // tk, V_ROWS, tk), lambda i: (i // tps, 0, i % tps, 0, 0))

    def vt_shape(heads, tk):
        return jax.ShapeDtypeStruct((batch, heads, seq // tk, V_ROWS, tk), BF16)

    n_mb = tm // MOBA_BLOCK
    out_specs = [
        pl.BlockSpec((tm, width["qa"]), rows),
        pl.BlockSpec((tm, width["ka"]), rows),
        pl.BlockSpec((n_mb, 1, width["ka"]), lambda i: (i, 0, 0)),
        vt_spec(A_HEADS, MOBA_BLOCK),
        pl.BlockSpec((tm, width["qb"]), rows),
        pl.BlockSpec((tm, width["kb"]), rows),
        vt_spec(B_KV_HEADS, WINDOW),
        pl.BlockSpec((tm, width["qc"]), rows),
        pl.BlockSpec((tm, width["kc"]), rows),
        vt_spec(C_HEADS, DIFF_BLOCK),
        pl.BlockSpec((tm, width["gates"]), rows),
    ]
    out_shape = [
        jax.ShapeDtypeStruct((t, width["qa"]), F32),
        jax.ShapeDtypeStruct((t, width["ka"]), BF16),
        jax.ShapeDtypeStruct((t // MOBA_BLOCK, 1, width["ka"]), F32),
        vt_shape(A_HEADS, MOBA_BLOCK),
        jax.ShapeDtypeStruct((t, width["qb"]), F32),
        jax.ShapeDtypeStruct((t, width["kb"]), BF16),
        vt_shape(B_KV_HEADS, WINDOW),
        jax.ShapeDtypeStruct((t, width["qc"]), F32),
        jax.ShapeDtypeStruct((t, width["kc"]), BF16),
        vt_shape(C_HEADS, DIFF_BLOCK),
        jax.ShapeDtypeStruct((t, width["gates"]), BF16),
    ]
    table_spec = pl.BlockSpec((tm, LANES), pos)
    return pl.pallas_call(
        functools.partial(_inproj_kernel, blocks=blocks, n_qkv=n_qkv),
        grid=(t // tm,),
        in_specs=[pl.BlockSpec((tm, d), rows), _resident((1, d)), _resident((d, n)), _resident(qk_gains.shape),
                  table_spec, table_spec, table_spec, table_spec,
                  _resident((2 * LANES, LANES)), _resident((2 * LANES, LANES))],
        out_specs=out_specs,
        out_shape=out_shape,
        scratch_shapes=[pltpu.VMEM((tm, d), BF16)],
        compiler_params=_params("parallel"),
        name="inproj",
    )(x2, gain.reshape(1, d), w_bf, qk_gains, *tables, _group_sum_matrix(LANES, HEAD_DIM),
      _group_sum_matrix(LANES, C_HALF))


def _flash_update(acc_ref, stream, s, vt_groups, m, bound=None):
    if bound is None:
        m_new = jnp.maximum(m, jnp.max(s, axis=0, keepdims=True))
        shift = jnp.where(m_new == NEG_INF, 0.0, m_new)
        alpha = jnp.exp2(m - shift)
    else:
        m_new, shift, alpha = m, bound, None
    p = jnp.exp2(s - shift).astype(BF16)
    cw = p.shape[1] // len(vt_groups)
    pvs = []
    for g, vts in enumerate(vt_groups):
        tk = vts[0].shape[1]
        pv = _dot(vts[0], p[0:tk, g * cw:(g + 1) * cw])
        for n in range(1, len(vts)):
            pv = pv + _dot(vts[n], p[n * tk:(n + 1) * tk, g * cw:(g + 1) * cw])
        pvs.append(pv)
    pv = pvs[0] if len(pvs) == 1 else jnp.concatenate(pvs, axis=1)
    acc_ref[stream] = acc_ref[stream] + pv if alpha is None else alpha * acc_ref[stream] + pv
    return m_new


MAX_FIXED_SHIFT = 60.0


def _score_bound(q_gain, k_gain, hd):
    return (jnp.max(jnp.abs(q_gain)) * jnp.max(jnp.abs(k_gain)) * (hd ** 0.5 * LOG2E)).reshape(1).astype(F32)


def _with_score_bound(bound_ref, attend):
    bound = bound_ref[0]
    fixed = bound <= MAX_FIXED_SHIFT

    @pl.when(fixed)
    def _():
        attend(bound)

    @pl.when(jnp.logical_not(fixed))
    def _():
        attend(None)


def _rolling_streams(n_streams, n_keys, s_scr, scores, next_scores, update):
    parked = s_scr.shape[0]
    out = []
    s = s_scr[...]
    if parked < n_keys:
        s = jnp.concatenate([s, scores(0, slice(parked, None))], axis=0)
    for st in range(n_streams):
        if st + 1 < n_streams:
            s_next = scores(st + 1, slice(None))
        elif next_scores is not None:
            s_scr[...] = next_scores(0, slice(0, parked))
        out.append(update(st, s))
        if st + 1 < n_streams:
            s = s_next
    return tuple(out)


def _prime_streams(s_scr, scores):
    s_scr[...] = scores(0, slice(0, s_scr.shape[0]))


def _moba_kernel(bound_ref, q_ref, k_ref, vt_ref, km_ref, o_ref, acc_scr, sel_scr, s_scr, *, tq):
    qi = pl.program_id(1)
    n_heads = q_ref.shape[1] // HEAD_DIM
    n_blk = km_ref.shape[1]
    per_block = LANES // HEAD_DIM
    qt = q_ref[...].T
    feat = lax.broadcasted_iota(jnp.int32, (LANES, tq), 0)
    blk = lax.broadcasted_iota(jnp.int32, (n_blk, tq), 0)
    blk_f = blk.astype(F32)
    krow = lax.broadcasted_iota(jnp.int32, (tq, per_block * tq), 0)
    qcol = lax.broadcasted_iota(jnp.int32, (tq, per_block * tq), 1) % tq
    causal = krow <= qcol

    n_streams = n_heads // per_block

    def lane_block(x, b):
        return x[:, b * LANES:(b + 1) * LANES]

    qhs = []
    for h in range(n_heads):
        b, hh = divmod(h, per_block)
        qhs.append(jnp.where((feat >= hh * HEAD_DIM) & (feat < (hh + 1) * HEAD_DIM),
                             qt[b * LANES:(b + 1) * LANES], 0.0))
    qts = [jnp.concatenate([(qh * (HEAD_DIM ** -0.5 * LOG2E)).astype(BF16)
                            for qh in qhs[b * per_block:(b + 1) * per_block]], axis=1) for b in range(n_streams)]

    def pair_keys(t):
        return k_ref[pl.ds(pl.multiple_of(t * 2 * tq, 2 * tq), 2 * tq), :]

    k_first = pair_keys(0)
    _prime_streams(s_scr, lambda b, rows: _dot(lane_block(k_first[rows], b), qts[b]))

    sels = []
    for h in range(n_heads):
        b, hh = divmod(h, per_block)
        if hh == 0:
            km_parts = _split3(km_ref[0, :, b * LANES:(b + 1) * LANES])

        k1, k2, k3 = km_parts
        q1, q2, q3 = _split3(qhs[h])
        gate = (_dot(k1, q1) + _dot(k2, q1) + _dot(k1, q2) + _dot(k3, q1) + _dot(k2, q2) + _dot(k1, q3))
        g = jnp.where(blk < qi, gate, NEG_INF)
        sel = jnp.zeros((n_blk, tq), F32)
        for _ in range(MOBA_TOPK):
            mx = jnp.max(g, axis=0, keepdims=True)
            first = jnp.min(jnp.where((g == mx) & (mx > NEG_INF), blk_f, float(n_blk)), axis=0, keepdims=True)
            pick = blk_f == first
            sel = jnp.where(pick, 1.0, sel)
            g = jnp.where(pick, NEG_INF, g)
        sels.append(sel)
        if hh == per_block - 1:
            sel_scr[b] = jnp.concatenate(sels, axis=1)
            sels = []

    acc_scr[...] = jnp.zeros_like(acc_scr)

    def pair_values(b, t):
        return [[vt_ref[0, b * per_block + hh, 2 * t], vt_ref[0, b * per_block + hh, 2 * t + 1]]
                for hh in range(per_block)]

    def masked(s, first, second):
        return jnp.concatenate([jnp.where(first, s[0:tq], NEG_INF), jnp.where(second, s[tq:2 * tq], NEG_INF)], axis=0)

    last = qi // 2
    own_second = qi % 2 == 1

    def attend(bound):
        def body(t, ms):
            kj = pair_keys(t)
            kn = pair_keys(t + 1)

            def update(b, s):
                first = sel_scr[b, pl.ds(2 * t, 1), :] > 0.0
                second = sel_scr[b, pl.ds(2 * t + 1, 1), :] > 0.0
                return _flash_update(acc_scr, b, masked(s, first, second), pair_values(b, t), ms[b], bound)

            return _rolling_streams(n_streams, 2 * tq, s_scr,
                                    lambda b, rows: _dot(lane_block(kj[rows], b), qts[b]),
                                    lambda b, rows: _dot(lane_block(kn[rows], b), qts[b]), update)

        ms = lax.fori_loop(0, last, body,
                           tuple(jnp.full((1, per_block * tq), NEG_INF, F32) for _ in range(n_streams)))
        k_last = pair_keys(last)

        def update_last(b, s):
            picked = sel_scr[b, pl.ds(2 * last, 1), :] > 0.0
            first = (own_second & picked) | (jnp.logical_not(own_second) & causal)
            second = own_second & causal
            return _flash_update(acc_scr, b, masked(s, first, second), pair_values(b, last), ms[b], bound)

        _rolling_streams(n_streams, 2 * tq, s_scr, lambda b, rows: _dot(lane_block(k_last[rows], b), qts[b]), None,
                         update_last)

    _with_score_bound(bound_ref, attend)

    outs = []
    for b in range(n_streams):
        acc = acc_scr[b]
        o = acc[0:HEAD_DIM] / acc[HEAD_DIM:HEAD_DIM + 1]
        outs += [o[:, hh * tq:(hh + 1) * tq] for hh in range(per_block)]
    o_ref[...] = jnp.concatenate(outs, axis=0).T.astype(o_ref.dtype)


def _moba(bound, qa, ka, vta, kmean, *, batch, seq):
    tq = MOBA_BLOCK
    t, width = qa.shape
    n_heads = width // HEAD_DIM
    n_q = seq // tq
    n_streams = width // LANES
    cols = (LANES // HEAD_DIM) * tq
    return pl.pallas_call(
        functools.partial(_moba_kernel, tq=tq),
        grid=(batch, n_q),
        in_specs=[
            pl.BlockSpec(memory_space=pltpu.SMEM),
            pl.BlockSpec((tq, width), lambda b, i: (b * n_q + i, 0)),
            pl.BlockSpec((seq, width), lambda b, i: (b, 0)),
            pl.BlockSpec((1, n_heads, n_q, V_ROWS, tq), lambda b, i: (b, 0, 0, 0, 0)),
            pl.BlockSpec((1, n_q, width), lambda b, i: (b, 0, 0)),
        ],
        out_specs=pl.BlockSpec((tq, width), lambda b, i: (b * n_q + i, 0)),
        out_shape=jax.ShapeDtypeStruct((t, width), BF16),
        scratch_shapes=[pltpu.VMEM((n_streams, V_ROWS, cols), F32), pltpu.VMEM((n_streams, n_q, cols), F32),
                        pltpu.VMEM((2 * tq, cols), F32)],
        compiler_params=_params("parallel", "arbitrary"),
        name="moba",
    )(bound, qa, ka, vta, kmean)


def _swa_kernel(q_ref, kp_ref, kc_ref, vtp_ref, vtc_ref, sink_ref, o_ref):
    i = pl.program_id(1)
    w = WINDOW
    n_sub = q_ref.shape[0] // w
    group = B_HEADS // B_KV_HEADS
    qt = q_ref[...].T
    krow = lax.broadcasted_iota(jnp.int32, (2 * w, w), 0)
    qcol = lax.broadcasted_iota(jnp.int32, (2 * w, w), 1)
    rel = qcol + w - krow
    in_window = (rel >= 0) & (rel < w)
    zeros = jnp.zeros((HEAD_DIM, w), F32)

    def key_block(j):
        return kp_ref[...] if j < 0 else kc_ref[j * w:(j + 1) * w, :]

    def value_block(kv, j):
        return vtp_ref[0, kv, 0] if j < 0 else vtc_ref[0, kv, j]

    def scores(j, kv):
        cols = []
        for g in range(group):
            h = kv * group + g
            qh = qt[h * HEAD_DIM:(h + 1) * HEAD_DIM, j * w:(j + 1) * w]
            cols.append(jnp.concatenate([qh if n == kv else zeros for n in range(B_KV_HEADS)], axis=0))
        kband = jnp.concatenate([key_block(j - 1), key_block(j)], axis=0)
        return _dot(kband, jnp.concatenate(cols, axis=1).astype(BF16))

    jobs = [(j, kv) for j in range(n_sub) for kv in range(B_KV_HEADS)]
    all_scores = [scores(j, kv) for j, kv in jobs]
    outs = {}
    for (j, kv), s in zip(jobs, all_scores):
        ok = in_window & ((krow >= w) | (i > 0)) if j == 0 else in_window
        s = jnp.where(jnp.concatenate([ok] * group, axis=1), s, NEG_INF)
        sink = sink_ref[:, kv * group * w:(kv + 1) * group * w] * LOG2E
        m = jnp.maximum(jnp.max(s, axis=0, keepdims=True), sink)
        p = jnp.exp2(s - m).astype(BF16)
        pv = _dot(value_block(kv, j - 1), p[0:w]) + _dot(value_block(kv, j), p[w:2 * w])
        o = pv[0:HEAD_DIM] / (pv[HEAD_DIM:HEAD_DIM + 1] + jnp.exp2(sink - m))
        for g in range(group):
            outs[(kv * group + g, j)] = o[:, g * w:(g + 1) * w]
    ot = jnp.concatenate([jnp.concatenate([outs[(h, j)] for j in range(n_sub)], axis=1) for h in range(B_HEADS)],
                         axis=0)
    o_ref[...] = ot.T.astype(o_ref.dtype)


def _swa(qb, kb, vtb, sinks, *, batch, seq, n_sub=4):
    t, qw = qb.shape
    w = WINDOW
    nb = seq // w
    steps = nb // n_sub
    prev = lambda b, i: (b * nb + jnp.maximum(i * n_sub - 1, 0), 0)
    cur = lambda b, i: (b * steps + i, 0)
    sink_row = jnp.repeat(sinks.astype(F32), w).reshape(1, B_HEADS * w)
    return pl.pallas_call(
        _swa_kernel,
        grid=(batch, steps),
        in_specs=[pl.BlockSpec((n_sub * w, qw), cur),
                  pl.BlockSpec((w, LANES), prev),
                  pl.BlockSpec((n_sub * w, LANES), cur),
                  pl.BlockSpec((1, B_KV_HEADS, 1, V_ROWS, w),
                               lambda b, i: (b, 0, jnp.maximum(i * n_sub - 1, 0), 0, 0)),
                  pl.BlockSpec((1, B_KV_HEADS, n_sub, V_ROWS, w), lambda b, i: (b, 0, i, 0, 0)),
                  pl.BlockSpec((1, B_HEADS * w), lambda b, i: (0, 0))],
        out_specs=pl.BlockSpec((n_sub * w, qw), cur),
        out_shape=jax.ShapeDtypeStruct((t, qw), BF16),
        compiler_params=_params("parallel", "arbitrary"),
        name="swa",
    )(qb, kb, kb, vtb, vtb, sink_row)


def _diff_kernel(bound_ref, q_ref, k_ref, vt_ref, lam_ref, g_ref, o_ref, acc_scr, s_scr, *, blk, lam_init):
    qi = pl.program_id(2)
    n_heads = LANES // HEAD_DIM
    qt = q_ref[...].T
    feat = lax.broadcasted_iota(jnp.int32, (LANES, blk), 0)
    krow = lax.broadcasted_iota(jnp.int32, (blk, blk), 0)
    qcol = lax.broadcasted_iota(jnp.int32, (blk, blk), 1)

    lq = lam_ref[...]
    lam = (jnp.exp(jnp.sum(lq[0:1] * lq[1:2], axis=1, keepdims=True))
           - jnp.exp(jnp.sum(lq[2:3] * lq[3:4], axis=1, keepdims=True)) + lam_init)

    n_streams = 2 * n_heads
    qts = [jnp.where((feat >= st * C_HALF) & (feat < (st + 1) * C_HALF), qt, 0.0).astype(BF16)
           for st in range(n_streams)]
    acc_scr[...] = jnp.zeros_like(acc_scr)

    def keys(j):
        return k_ref[pl.ds(pl.multiple_of(j * blk, blk), blk), :]

    k_first = keys(0)
    _prime_streams(s_scr, lambda st, rows: _dot(k_first[rows], qts[st]))

    def attend(bound):
        def body(j, ms):
            kj = keys(j)
            kn = keys(j + 1)
            return _rolling_streams(
                n_streams, blk, s_scr,
                lambda st, rows: _dot(kj[rows], qts[st]), lambda st, rows: _dot(kn[rows], qts[st]),
                lambda st, s: _flash_update(acc_scr, st, s, [[vt_ref[0, st // 2, j]]], ms[st], bound))

        ms = lax.fori_loop(0, qi, body, tuple(jnp.full((1, blk), NEG_INF, F32) for _ in range(n_streams)))
        k_own = keys(qi)
        _rolling_streams(
            n_streams, blk, s_scr, lambda st, rows: _dot(k_own[rows], qts[st]), None,
            lambda st, s: _flash_update(acc_scr, st, jnp.where(krow <= qcol, s, NEG_INF),
                                        [[vt_ref[0, st // 2, qi]]], ms[st], bound))

    _with_score_bound(bound_ref, attend)

    outs = []
    for h in range(n_heads):
        maps = [acc_scr[2 * h + c, 0:HEAD_DIM, :] / acc_scr[2 * h + c, HEAD_DIM:HEAD_DIM + 1, :] for c in range(2)]
        o = maps[0] - lam * maps[1]
        ms_o = jnp.mean(o * o, axis=0, keepdims=True)
        outs.append((o * lax.rsqrt(ms_o + EPS) * g_ref[...]) * (1.0 - lam_init))
    o_ref[...] = jnp.concatenate(outs, axis=0).T.astype(o_ref.dtype)


def _diff(bound, qc, kc, vtc, lam_rows, subln, *, batch, seq, lam_init, blk):
    t, width = qc.shape
    n_pairs = width // LANES
    n_heads = LANES // HEAD_DIM
    n_q = seq // blk
    return pl.pallas_call(
        functools.partial(_diff_kernel, blk=blk, lam_init=lam_init),
        grid=(batch, n_pairs, n_q),
        in_specs=[
            pl.BlockSpec(memory_space=pltpu.SMEM),
            pl.BlockSpec((blk, LANES), lambda b, hp, i: (b * n_q + i, hp)),
            pl.BlockSpec((seq, LANES), lambda b, hp, i: (b, hp)),
            pl.BlockSpec((1, n_heads, n_q, V_ROWS, blk), lambda b, hp, i: (b, hp, 0, 0, 0)),
            pl.BlockSpec((4, C_HALF), lambda b, hp, i: (0, 0)),
            pl.BlockSpec((HEAD_DIM, 1), lambda b, hp, i: (0, 0)),
        ],
        out_specs=pl.BlockSpec((blk, LANES), lambda b, hp, i: (b * n_q + i, hp)),
        out_shape=jax.ShapeDtypeStruct((t, width), BF16),
        scratch_shapes=[pltpu.VMEM((2 * n_heads, V_ROWS, blk), F32),
                        pltpu.VMEM((DIFF_PARKED_KEYS, blk), F32)],
        compiler_params=_params("parallel", "parallel", "arbitrary"),
        name="diff",
    )(bound, qc, kc, vtc, lam_rows, subln.reshape(HEAD_DIM, 1))


def _merge_kernel(x_ref, ya_ref, yb_ref, yc_ref, ga_ref, gb_ref, gc_ref, wa_ref, wb_ref, wc_ref, wo_ref, o_ref):
    merged = (jax.nn.sigmoid(ga_ref[...].astype(F32)) * _dot(ya_ref[...], wa_ref[...])
              + jax.nn.sigmoid(gb_ref[...].astype(F32)) * _dot(yb_ref[...], wb_ref[...])
              + jax.nn.sigmoid(gc_ref[...].astype(F32)) * _dot(yc_ref[...], wc_ref[...]))
    o_ref[...] = x_ref[...] + _dot(merged.astype(BF16), wo_ref[...])


def _merge(x2, ya, yb, yc, gates, w_pa, w_pb, w_pc, w_out, *, tm=512):
    t, d = x2.shape
    rows = lambda i: (i, 0)
    return pl.pallas_call(
        _merge_kernel,
        grid=(t // tm,),
        in_specs=[
            pl.BlockSpec((tm, d), rows),
            pl.BlockSpec((tm, ya.shape[1]), rows),
            pl.BlockSpec((tm, yb.shape[1]), rows),
            pl.BlockSpec((tm, yc.shape[1]), rows),
            pl.BlockSpec((tm, d), lambda i: (i, 0)),
            pl.BlockSpec((tm, d), lambda i: (i, 1)),
            pl.BlockSpec((tm, d), lambda i: (i, 2)),
            _resident(w_pa.shape),
            _resident(w_pb.shape),
            _resident(w_pc.shape),
            _resident(w_out.shape),
        ],
        out_specs=pl.BlockSpec((tm, d), rows),
        out_shape=jax.ShapeDtypeStruct((t, d), F32),
        compiler_params=_params("parallel"),
        name="merge",
    )(x2, ya, yb, yc, gates, gates, gates, w_pa, w_pb, w_pc, w_out)


def _convglu_kernel(x_ref, xh_ref, g_ref, wu_ref, cw_ref, cb_ref, wd_ref, o_ref, h_scr, u_scr, act_scr,
                    *, tm, tf, tiles_per_seq):
    i = pl.program_id(0)
    halo = CONV_HALO
    ff = wd_ref.shape[0]
    n_chunks = ff // tf

    def normed(x):
        ms = jnp.mean(x * x, axis=-1, keepdims=True)
        return x * lax.rsqrt(ms + EPS) * g_ref[...]

    keep = jnp.where(i % tiles_per_seq == 0, 0.0, 1.0)
    h_scr[0:halo, :] = (normed(xh_ref[...]) * keep).astype(BF16)
    h_scr[halo:, :] = normed(x_ref[...]).astype(BF16)

    def up(c):
        h = h_scr[...]
        u_scr[c % 2, 0] = _dot(h, wu_ref[:, c * tf:(c + 1) * tf])
        u_scr[c % 2, 1] = _dot(h, wu_ref[:, ff + c * tf:ff + (c + 1) * tf])

    def conv(c, half):
        col0 = half * ff + c * tf
        y = cb_ref[:, col0:col0 + tf]
        u = u_scr[c % 2, half]
        for j in range(CONV_W):
            back = CONV_W - 1 - j
            shifted = u if back == 0 else pltpu.roll(u, back, 0)
            y = y + shifted[halo:, :] * cw_ref[j:j + 1, col0:col0 + tf]
        return y

    up(0)
    for c in range(n_chunks):
        if c + 1 < n_chunks:
            up(c + 1)
        gate_u = conv(c, 0)
        val_u = conv(c, 1)
        act_scr[:, c * tf:(c + 1) * tf] = ((gate_u * jax.nn.sigmoid(gate_u)) * val_u).astype(BF16)

    o_ref[...] = x_ref[...] + _dot(act_scr[...], wd_ref[...])


def _convglu(x2, gain, w_up, conv_w, conv_b, w_down, *, seq, tm=512, tf=256):
    t, d = x2.shape
    ff = w_down.shape[0]
    halo = CONV_HALO
    tiles_per_seq = seq // tm
    halo_blocks = tm // halo
    return pl.pallas_call(
        functools.partial(_convglu_kernel, tm=tm, tf=tf, tiles_per_seq=tiles_per_seq),
        grid=(t // tm,),
        in_specs=[
            pl.BlockSpec((tm, d), lambda i: (i, 0)),
            pl.BlockSpec((halo, d), lambda i: (jnp.maximum(i * halo_blocks - 1, 0), 0)),
            _resident((1, d)),
            _resident(w_up.shape),
            _resident(conv_w.shape),
            _resident((1, 2 * ff)),
            _resident(w_down.shape),
        ],
        out_specs=pl.BlockSpec((tm, d), lambda i: (i, 0)),
        out_shape=jax.ShapeDtypeStruct((t, d), F32),
        scratch_shapes=[
            pltpu.VMEM((tm + halo, d), BF16),
            pltpu.VMEM((2, 2, tm + halo, tf), F32),
            pltpu.VMEM((tm, ff), BF16),
        ],
        compiler_params=_params("parallel"),
        name="convglu",
    )(x2, x2, gain.reshape(1, d), w_up, conv_w, conv_b.reshape(1, -1), w_down)


def _rope_tables(seq, dim):
    inv = 1.0 / (ROPE_THETA ** (jnp.arange(0, dim, 2, dtype=F32) / dim))
    ang = jnp.arange(seq, dtype=F32)[:, None] * inv[None, :]
    return jnp.cos(ang), jnp.sin(ang)


def kernel(x, attn_norm, w_in, qn_a, kn_a, qn_b, kn_b, sinks, qn_c, kn_c, lam_q1, lam_k1, lam_q2, lam_k2, subln,
           w_pa, w_pb, w_pc, w_out, mlp_norm, w_up, conv_w, conv_b, w_down):
    batch, seq, d = x.shape
    depth = w_in.shape[0]
    x2 = x.reshape(batch * seq, d)

    def rope_block(dim):
        cos, sin = _rope_tables(seq, dim)
        reps = LANES // dim
        return (jnp.tile(jnp.concatenate([cos, cos], axis=-1), (1, reps)),
                jnp.tile(jnp.concatenate([-sin, sin], axis=-1), (1, reps)))

    tables = rope_block(HEAD_DIM) + rope_block(C_HALF)

    segments = (("qa", A_W), ("ka", A_W), ("va", A_W), ("qb", B_QW), ("kb", B_KVW), ("vb", B_KVW),
                ("qc", C_W), ("kc", C_W), ("vc", C_W), ("gates", N_BRANCH * d))

    for i in range(depth):
        lam_init = 0.8 - 0.6 * float(np.exp(-0.3 * i))
        qk_gains = jnp.stack([jnp.tile(g, LANES // g.shape[0])
                              for g in (qn_a[i], kn_a[i], qn_b[i], kn_b[i], qn_c[i], kn_c[i])]).astype(F32)
        qa, ka, kmean, vta, qb, kb, vtb, qc, kc, vtc, gates = _inproj(
            x2, attn_norm[i], w_in[i].astype(BF16), qk_gains, tables, batch=batch, seq=seq, segments=segments)

        kmean = kmean.reshape(batch, seq // MOBA_BLOCK, A_W)
        ya = _moba(_score_bound(qn_a[i], kn_a[i], HEAD_DIM), qa, ka, vta, kmean, batch=batch, seq=seq)
        yb = _swa(qb, kb, vtb, sinks[i], batch=batch, seq=seq)
        lam_rows = jnp.stack([lam_q1[i], lam_k1[i], lam_q2[i], lam_k2[i]]).astype(F32)
        yc = _diff(_score_bound(qn_c[i], kn_c[i], C_HALF), qc, kc, vtc, lam_rows, subln[i], batch=batch, seq=seq, lam_init=lam_init, blk=DIFF_BLOCK)

        x2 = _merge(x2, ya, yb, yc, gates, w_pa[i].astype(BF16), w_pb[i].astype(BF16), w_pc[i].astype(BF16),
                    w_out[i].astype(BF16))
        x2 = _convglu(x2, mlp_norm[i], w_up[i].astype(BF16), conv_w[i], conv_b[i], w_down[i].astype(BF16), seq=seq)

    return x2.reshape(batch, seq, d)
```

```python
import functools

import numpy as np
import jax
import jax.numpy as jnp
from jax import lax
from jax.experimental import pallas as pl
from jax.experimental.pallas import tpu as pltpu

F32 = jnp.float32
BF16 = jnp.bfloat16
NEG_INF = float("-inf")

LANES = 128
VMEM_LIMIT = 48 * 1024 * 1024

EPS = 1e-6
HEAD_DIM = 64
ROPE_THETA = 10000.0
A_HEADS = 4
MOBA_BLOCK = 256
MOBA_TOPK = 3
B_HEADS = 8
B_KV_HEADS = 2
WINDOW = 128
C_HEADS = 4
C_HALF = HEAD_DIM // 2
N_BRANCH = 3
CONV_W = 3
CONV_HALO = 16
BF16_SUBLANES = 16
V_ROWS = HEAD_DIM + BF16_SUBLANES
DIFF_BLOCK = 512
DIFF_PARKED_KEYS = 256
LOG2E = 1.4426950408889634

A_W = A_HEADS * HEAD_DIM
B_QW = B_HEADS * HEAD_DIM
B_KVW = B_KV_HEADS * HEAD_DIM
C_W = C_HEADS * HEAD_DIM


def _dot(a, b):
    return jnp.dot(a, b, preferred_element_type=F32)


def _split3(a):
    a1 = a.astype(BF16)
    r = a - a1.astype(F32)
    a2 = r.astype(BF16)
    a3 = (r - a2.astype(F32)).astype(BF16)
    return a1, a2, a3


def _resident(shape):
    return pl.BlockSpec(shape, lambda *_: (0,) * len(shape), pipeline_mode=pl.Buffered(1))


def _params(*semantics):
    return pltpu.CompilerParams(dimension_semantics=semantics, vmem_limit_bytes=VMEM_LIMIT)


def _group_sum_matrix(w, hd):
    idx = np.arange(w) // hd
    same = (idx[:, None] == idx[None, :]).astype(np.float32)
    return jnp.asarray(np.concatenate([same, same], axis=0), dtype=BF16)


MM_COLS = 256


def _inproj_kernel(x_ref, g_ref, w_ref, gains_ref, cos64_ref, sin64_ref, cos32_ref, sin32_ref, gs64_ref, gs32_ref,
                   qa_ref, ka_ref, km_ref, vta_ref, qb_ref, kb_ref, vtb_ref, qc_ref, kc_ref, vtc_ref, gates_ref,
                   h_scr, *, blocks, n_qkv):
    x = x_ref[...]
    tm = x.shape[0]
    ms = jnp.mean(x * x, axis=-1, keepdims=True)
    h_scr[...] = (x * lax.rsqrt(ms + EPS) * g_ref[...]).astype(BF16)
    lane = lax.broadcasted_iota(jnp.int32, (tm, LANES), 1)

    def norm_rope(y, gain_row, hd, scale):
        cos_ref, sin_ref, gs_ref = ((cos64_ref, sin64_ref, gs64_ref) if hd == HEAD_DIM
                                    else (cos32_ref, sin32_ref, gs32_ref))
        sq = y * y
        hi = sq.astype(BF16)
        lo = (sq - hi.astype(F32)).astype(BF16)
        msq = _dot(jnp.concatenate([hi, lo], axis=1), gs_ref[...]) * (1.0 / hd)
        yn = y * lax.rsqrt(msq + EPS) * gains_ref[gain_row:gain_row + 1, :]
        half = hd // 2
        partner = jnp.where(lane % hd < half, pltpu.roll(yn, LANES - half, 1), pltpu.roll(yn, half, 1))
        out = yn * cos_ref[...] + partner * sin_ref[...]
        return out * scale if scale != 1.0 else out

    def store_vt(vt_ref, y, lb):
        tk = vt_ref.shape[-1]
        yt = y.T
        ones = jnp.ones((V_ROWS - HEAD_DIM, tk), F32)
        for hh in range(LANES // HEAD_DIM):
            for n in range(tm // tk):
                blk = yt[hh * HEAD_DIM:(hh + 1) * HEAD_DIM, n * tk:(n + 1) * tk]
                vt_ref[0, lb * (LANES // HEAD_DIM) + hh, n] = jnp.concatenate([blk, ones], axis=0).astype(BF16)

    q_scale = HEAD_DIM ** -0.5 * LOG2E
    c_scale = C_HALF ** -0.5 * LOG2E

    def epilogue(kind, lb, y):
        cols = slice(lb * LANES, (lb + 1) * LANES)
        if kind == "qa":
            qa_ref[:, cols] = norm_rope(y, 0, HEAD_DIM, 1.0)
        elif kind == "ka":
            yk = norm_rope(y, 1, HEAD_DIM, 1.0)
            ka_ref[:, cols] = yk.astype(BF16)
            for n in range(tm // MOBA_BLOCK):
                rows = yk[n * MOBA_BLOCK:(n + 1) * MOBA_BLOCK]
                km_ref[n, :, cols] = jnp.sum(rows, axis=0, keepdims=True) * (1.0 / MOBA_BLOCK)
        elif kind == "va":
            store_vt(vta_ref, y, lb)
        elif kind == "qb":
            qb_ref[:, cols] = norm_rope(y, 2, HEAD_DIM, q_scale)
        elif kind == "kb":
            kb_ref[:, cols] = norm_rope(y, 3, HEAD_DIM, 1.0).astype(BF16)
        elif kind == "vb":
            store_vt(vtb_ref, y, lb)
        elif kind == "qc":
            qc_ref[:, cols] = norm_rope(y, 4, C_HALF, c_scale)
        elif kind == "kc":
            kc_ref[:, cols] = norm_rope(y, 5, C_HALF, 1.0).astype(BF16)
        elif kind == "vc":
            store_vt(vtc_ref, y, lb)

    def chunk(c):
        return _dot(h_scr[...], w_ref[:, c * MM_COLS:(c + 1) * MM_COLS].astype(BF16))

    n_chunks = w_ref.shape[1] // MM_COLS
    y_next = chunk(0)
    for c in range(n_chunks):
        y = y_next
        if c + 1 < n_chunks:
            y_next = chunk(c + 1)
        col0 = c * MM_COLS
        if col0 < n_qkv:
            for part in range(MM_COLS // LANES):
                kind, lb = blocks[col0 // LANES + part]
                epilogue(kind, lb, y[:, part * LANES:(part + 1) * LANES])
        else:
            gates_ref[:, col0 - n_qkv:col0 - n_qkv + MM_COLS] = y.astype(gates_ref.dtype)


def _inproj(x2, gain, w_bf, qk_gains, tables, *, batch, seq, segments, tm=512):
    t, d = x2.shape
    n = w_bf.shape[1]
    blocks = [(name, lb) for name, width in segments[:-1] for lb in range(width // LANES)]
    n_qkv = len(blocks) * LANES
    width = dict(segments)
    tps = seq // tm
    rows = lambda i: (i, 0)
    pos = lambda i: (i % tps, 0)

    def vt_spec(heads, tk):
        return pl.BlockSpec((1, heads, tm // tk, V_ROWS, tk), lambda i: (i // tps, 0, i % tps, 0, 0))

    def vt_shape(heads, tk):
        return jax.ShapeDtypeStruct((batch, heads, seq // tk, V_ROWS, tk), BF16)

    n_mb = tm // MOBA_BLOCK
    out_specs = [
        pl.BlockSpec((tm, width["qa"]), rows),
        pl.BlockSpec((tm, width["ka"]), rows),
        pl.BlockSpec((n_mb, 1, width["ka"]), lambda i: (i, 0, 0)),
        vt_spec(A_HEADS, MOBA_BLOCK),
        pl.BlockSpec((tm, width["qb"]), rows),
        pl.BlockSpec((tm, width["kb"]), rows),
        vt_spec(B_KV_HEADS, WINDOW),
        pl.BlockSpec((tm, width["qc"]), rows),
        pl.BlockSpec((tm, width["kc"]), rows),
        vt_spec(C_HEADS, DIFF_BLOCK),
        pl.BlockSpec((tm, width["gates"]), rows),
    ]
    out_shape = [
        jax.ShapeDtypeStruct((t, width["qa"]), F32),
        jax.ShapeDtypeStruct((t, width["ka"]), BF16),
        jax.ShapeDtypeStruct((t // MOBA_BLOCK, 1, width["ka"]), F32),
        vt_shape(A_HEADS, MOBA_BLOCK),
        jax.ShapeDtypeStruct((t, width["qb"]), F32),
        jax.ShapeDtypeStruct((t, width["kb"]), BF16),
        vt_shape(B_KV_HEADS, WINDOW),
        jax.ShapeDtypeStruct((t, width["qc"]), F32),
        jax.ShapeDtypeStruct((t, width["kc"]), BF16),
        vt_shape(C_HEADS, DIFF_BLOCK),
        jax.ShapeDtypeStruct((t, width["gates"]), BF16),
    ]
    table_spec = pl.BlockSpec((tm, LANES), pos)
    return pl.pallas_call(
        functools.partial(_inproj_kernel, blocks=blocks, n_qkv=n_qkv),
        grid=(t // tm,),
        in_specs=[pl.BlockSpec((tm, d), rows), _resident((1, d)), _resident((d, n)), _resident(qk_gains.shape),
                  table_spec, table_spec, table_spec, table_spec,
                  _resident((2 * LANES, LANES)), _resident((2 * LANES, LANES))],
        out_specs=out_specs,
        out_shape=out_shape,
        scratch_shapes=[pltpu.VMEM((tm, d), BF16)],
        compiler_params=_params("parallel"),
        name="inproj",
    )(x2, gain.reshape(1, d), w_bf, qk_gains, *tables, _group_sum_matrix(LANES, HEAD_DIM),
      _group_sum_matrix(LANES, C_HALF))


def _flash_update(acc_ref, stream, s, vt_groups, m, bound=None):
    if bound is None:
        m_new = jnp.maximum(m, jnp.max(s, axis=0, keepdims=True))
        shift = jnp.where(m_new == NEG_INF, 0.0, m_new)
        alpha = jnp.exp2(m - shift)
    else:
        m_new, shift, alpha = m, bound, None
    p = jnp.exp2(s - shift).astype(BF16)
    cw = p.shape[1] // len(vt_groups)
    pvs = []
    for g, vts in enumerate(vt_groups):
        tk = vts[0].shape[1]
        pv = _dot(vts[0], p[0:tk, g * cw:(g + 1) * cw])
        for n in range(1, len(vts)):
            pv = pv + _dot(vts[n], p[n * tk:(n + 1) * tk, g * cw:(g + 1) * cw])
        pvs.append(pv)
    pv = pvs[0] if len(pvs) == 1 else jnp.concatenate(pvs, axis=1)
    acc_ref[stream] = acc_ref[stream] + pv if alpha is None else alpha * acc_ref[stream] + pv
    return m_new


MAX_FIXED_SHIFT = 60.0


def _score_bound(q_gain, k_gain, hd):
    return (jnp.max(jnp.abs(q_gain)) * jnp.max(jnp.abs(k_gain)) * (hd ** 0.5 * LOG2E)).reshape(1).astype(F32)


def _with_score_bound(bound_ref, attend):
    bound = bound_ref[0]
    fixed = bound <= MAX_FIXED_SHIFT

    @pl.when(fixed)
    def _():
        attend(bound)

    @pl.when(jnp.logical_not(fixed))
    def _():
        attend(None)


def _rolling_streams(n_streams, n_keys, s_scr, scores, next_scores, update):
    parked = s_scr.shape[0]
    out = []
    s = s_scr[...]
    if parked < n_keys:
        s = jnp.concatenate([s, scores(0, slice(parked, None))], axis=0)
    for st in range(n_streams):
        if st + 1 < n_streams:
            s_next = scores(st + 1, slice(None))
        elif next_scores is not None:
            s_scr[...] = next_scores(0, slice(0, parked))
        out.append(update(st, s))
        if st + 1 < n_streams:
            s = s_next
    return tuple(out)


def _prime_streams(s_scr, scores):
    s_scr[...] = scores(0, slice(0, s_scr.shape[0]))


def _moba_kernel(bound_ref, q_ref, k_ref, vt_ref, km_ref, o_ref, acc_scr, sel_scr, s_scr, *, tq):
    qi = pl.program_id(1)
    n_heads = q_ref.shape[1] // HEAD_DIM
    n_blk = km_ref.shape[1]
    per_block = LANES // HEAD_DIM
    qt = q_ref[...].T
    feat = lax.broadcasted_iota(jnp.int32, (LANES, tq), 0)
    blk = lax.broadcasted_iota(jnp.int32, (n_blk, tq), 0)
    blk_f = blk.astype(F32)
    krow = lax.broadcasted_iota(jnp.int32, (tq, per_block * tq), 0)
    qcol = lax.broadcasted_iota(jnp.int32, (tq, per_block * tq), 1) % tq
    causal = krow <= qcol

    n_streams = n_heads // per_block

    def lane_block(x, b):
        return x[:, b * LANES:(b + 1) * LANES]

    qhs = []
    for h in range(n_heads):
        b, hh = divmod(h, per_block)
        qhs.append(jnp.where((feat >= hh * HEAD_DIM) & (feat < (hh + 1) * HEAD_DIM),
                             qt[b * LANES:(b + 1) * LANES], 0.0))
    qts = [jnp.concatenate([(qh * (HEAD_DIM ** -0.5 * LOG2E)).astype(BF16)
                            for qh in qhs[b * per_block:(b + 1) * per_block]], axis=1) for b in range(n_streams)]

    def pair_keys(t):
        return k_ref[pl.ds(pl.multiple_of(t * 2 * tq, 2 * tq), 2 * tq), :]

    k_first = pair_keys(0)
    _prime_streams(s_scr, lambda b, rows: _dot(lane_block(k_first[rows], b), qts[b]))

    sels = []
    for h in range(n_heads):
        b, hh = divmod(h, per_block)
        if hh == 0:
            km_parts = _split3(km_ref[0, :, b * LANES:(b + 1) * LANES])

        k1, k2, k3 = km_parts
        q1, q2, q3 = _split3(qhs[h])
        gate = (_dot(k1, q1) + _dot(k2, q1) + _dot(k1, q2) + _dot(k3, q1) + _dot(k2, q2) + _dot(k1, q3))
        g = jnp.where(blk < qi, gate, NEG_INF)
        sel = jnp.zeros((n_blk, tq), F32)
        for _ in range(MOBA_TOPK):
            mx = jnp.max(g, axis=0, keepdims=True)
            first = jnp.min(jnp.where((g == mx) & (mx > NEG_INF), blk_f, float(n_blk)), axis=0, keepdims=True)
            pick = blk_f == first
            sel = jnp.where(pick, 1.0, sel)
            g = jnp.where(pick, NEG_INF, g)
        sels.append(sel)
        if hh == per_block - 1:
            sel_scr[b] = jnp.concatenate(sels, axis=1)
            sels = []

    acc_scr[...] = jnp.zeros_like(acc_scr)

    def pair_values(b, t):
        return [[vt_ref[0, b * per_block + hh, 2 * t], vt_ref[0, b * per_block + hh, 2 * t + 1]]
                for hh in range(per_block)]

    def masked(s, first, second):
        return jnp.concatenate([jnp.where(first, s[0:tq], NEG_INF), jnp.where(second, s[tq:2 * tq], NEG_INF)], axis=0)

    last = qi // 2
    own_second = qi % 2 == 1

    def attend(bound):
        def body(t, ms):
            kj = pair_keys(t)
            kn = pair_keys(t + 1)

            def update(b, s):
                first = sel_scr[b, pl.ds(2 * t, 1), :] > 0.0
                second = sel_scr[b, pl.ds(2 * t + 1, 1), :] > 0.0
                return _flash_update(acc_scr, b, masked(s, first, second), pair_values(b, t), ms[b], bound)

            return _rolling_streams(n_streams, 2 * tq, s_scr,
                                    lambda b, rows: _dot(lane_block(kj[rows], b), qts[b]),
                                    lambda b, rows: _dot(lane_block(kn[rows], b), qts[b]), update)

        ms = lax.fori_loop(0, last, body,
                           tuple(jnp.full((1, per_block * tq), NEG_INF, F32) for _ in range(n_streams)))
        k_last = pair_keys(last)

        def update_last(b, s):
            picked = sel_scr[b, pl.ds(2 * last, 1), :] > 0.0
            first = (own_second & picked) | (jnp.logical_not(own_second) & causal)
            second = own_second & causal
            return _flash_update(acc_scr, b, masked(s, first, second), pair_values(b, last), ms[b], bound)

        _rolling_streams(n_streams, 2 * tq, s_scr, lambda b, rows: _dot(lane_block(k_last[rows], b), qts[b]), None,
                         update_last)

    _with_score_bound(bound_ref, attend)

    outs = []
    for b in range(n_streams):
        acc = acc_scr[b]
        o = acc[0:HEAD_DIM] / acc[HEAD_DIM:HEAD_DIM + 1]
        outs += [o[:, hh * tq:(hh + 1) * tq] for hh in range(per_block)]
    o_ref[...] = jnp.concatenate(outs, axis=0).T.astype(o_ref.dtype)


def _moba(bound, qa, ka, vta, kmean, *, batch, seq):
    tq = MOBA_BLOCK
    t, width = qa.shape
    n_heads = width // HEAD_DIM
    n_q = seq // tq
    n_streams = width // LANES
    cols = (LANES // HEAD_DIM) * tq
    return pl.pallas_call(
        functools.partial(_moba_kernel, tq=tq),
        grid=(batch, n_q),
        in_specs=[
            pl.BlockSpec(memory_space=pltpu.SMEM),
            pl.BlockSpec((tq, width), lambda b, i: (b * n_q + i, 0)),
            pl.BlockSpec((seq, width), lambda b, i: (b, 0)),
            pl.BlockSpec((1, n_heads, n_q, V_ROWS, tq), lambda b, i: (b, 0, 0, 0, 0)),
            pl.BlockSpec((1, n_q, width), lambda b, i: (b, 0, 0)),
        ],
        out_specs=pl.BlockSpec((tq, width), lambda b, i: (b * n_q + i, 0)),
        out_shape=jax.ShapeDtypeStruct((t, width), BF16),
        scratch_shapes=[pltpu.VMEM((n_streams, V_ROWS, cols), F32), pltpu.VMEM((n_streams, n_q, cols), F32),
                        pltpu.VMEM((2 * tq, cols), F32)],
        compiler_params=_params("parallel", "arbitrary"),
        name="moba",
    )(bound, qa, ka, vta, kmean)


def _swa_kernel(q_ref, kp_ref, kc_ref, vtp_ref, vtc_ref, sink_ref, o_ref):
    i = pl.program_id(1)
    w = WINDOW
    n_sub = q_ref.shape[0] // w
    group = B_HEADS // B_KV_HEADS
    qt = q_ref[...].T
    krow = lax.broadcasted_iota(jnp.int32, (2 * w, w), 0)
    qcol = lax.broadcasted_iota(jnp.int32, (2 * w, w), 1)
    rel = qcol + w - krow
    in_window = (rel >= 0) & (rel < w)
    zeros = jnp.zeros((HEAD_DIM, w), F32)

    def key_block(j):
        return kp_ref[...] if j < 0 else kc_ref[j * w:(j + 1) * w, :]

    def value_block(kv, j):
        return vtp_ref[0, kv, 0] if j < 0 else vtc_ref[0, kv, j]

    def scores(j, kv):
        cols = []
        for g in range(group):
            h = kv * group + g
            qh = qt[h * HEAD_DIM:(h + 1) * HEAD_DIM, j * w:(j + 1) * w]
            cols.append(jnp.concatenate([qh if n == kv else zeros for n in range(B_KV_HEADS)], axis=0))
        kband = jnp.concatenate([key_block(j - 1), key_block(j)], axis=0)
        return _dot(kband, jnp.concatenate(cols, axis=1).astype(BF16))

    jobs = [(j, kv) for j in range(n_sub) for kv in range(B_KV_HEADS)]
    all_scores = [scores(j, kv) for j, kv in jobs]
    outs = {}
    for (j, kv), s in zip(jobs, all_scores):
        ok = in_window & ((krow >= w) | (i > 0)) if j == 0 else in_window
        s = jnp.where(jnp.concatenate([ok] * group, axis=1), s, NEG_INF)
        sink = sink_ref[:, kv * group * w:(kv + 1) * group * w] * LOG2E
        m = jnp.maximum(jnp.max(s, axis=0, keepdims=True), sink)
        p = jnp.exp2(s - m).astype(BF16)
        pv = _dot(value_block(kv, j - 1), p[0:w]) + _dot(value_block(kv, j), p[w:2 * w])
        o = pv[0:HEAD_DIM] / (pv[HEAD_DIM:HEAD_DIM + 1] + jnp.exp2(sink - m))
        for g in range(group):
            outs[(kv * group + g, j)] = o[:, g * w:(g + 1) * w]
    ot = jnp.concatenate([jnp.concatenate([outs[(h, j)] for j in range(n_sub)], axis=1) for h in range(B_HEADS)],
                         axis=0)
    o_ref[...] = ot.T.astype(o_ref.dtype)


def _swa(qb, kb, vtb, sinks, *, batch, seq, n_sub=4):
    t, qw = qb.shape
    w = WINDOW
    nb = seq // w
    steps = nb // n_sub
    prev = lambda b, i: (b * nb + jnp.maximum(i * n_sub - 1, 0), 0)
    cur = lambda b, i: (b * steps + i, 0)
    sink_row = jnp.repeat(sinks.astype(F32), w).reshape(1, B_HEADS * w)
    return pl.pallas_call(
        _swa_kernel,
        grid=(batch, steps),
        in_specs=[pl.BlockSpec((n_sub * w, qw), cur),
                  pl.BlockSpec((w, LANES), prev),
                  pl.BlockSpec((n_sub * w, LANES), cur),
                  pl.BlockSpec((1, B_KV_HEADS, 1, V_ROWS, w),
                               lambda b, i: (b, 0, jnp.maximum(i * n_sub - 1, 0), 0, 0)),
                  pl.BlockSpec((1, B_KV_HEADS, n_sub, V_ROWS, w), lambda b, i: (b, 0, i, 0, 0)),
                  pl.BlockSpec((1, B_HEADS * w), lambda b, i: (0, 0))],
        out_specs=pl.BlockSpec((n_sub * w, qw), cur),
        out_shape=jax.ShapeDtypeStruct((t, qw), BF16),
        compiler_params=_params("parallel", "arbitrary"),
        name="swa",
    )(qb, kb, kb, vtb, vtb, sink_row)


def _diff_kernel(bound_ref, q_ref, k_ref, vt_ref, lam_ref, g_ref, o_ref, acc_scr, s_scr, *, blk, lam_init):
    qi = pl.program_id(2)
    n_heads = LANES // HEAD_DIM
    qt = q_ref[...].T
    feat = lax.broadcasted_iota(jnp.int32, (LANES, blk), 0)
    krow = lax.broadcasted_iota(jnp.int32, (blk, blk), 0)
    qcol = lax.broadcasted_iota(jnp.int32, (blk, blk), 1)

    lq = lam_ref[...]
    lam = (jnp.exp(jnp.sum(lq[0:1] * lq[1:2], axis=1, keepdims=True))
           - jnp.exp(jnp.sum(lq[2:3] * lq[3:4], axis=1, keepdims=True)) + lam_init)

    n_streams = 2 * n_heads
    qts = [jnp.where((feat >= st * C_HALF) & (feat < (st + 1) * C_HALF), qt, 0.0).astype(BF16)
           for st in range(n_streams)]
    acc_scr[...] = jnp.zeros_like(acc_scr)

    def keys(j):
        return k_ref[pl.ds(pl.multiple_of(j * blk, blk), blk), :]

    k_first = keys(0)
    _prime_streams(s_scr, lambda st, rows: _dot(k_first[rows], qts[st]))

    def attend(bound):
        def body(j, ms):
            kj = keys(j)
            kn = keys(j + 1)
            return _rolling_streams(
                n_streams, blk, s_scr,
                lambda st, rows: _dot(kj[rows], qts[st]), lambda st, rows: _dot(kn[rows], qts[st]),
                lambda st, s: _flash_update(acc_scr, st, s, [[vt_ref[0, st // 2, j]]], ms[st], bound))

        ms = lax.fori_loop(0, qi, body, tuple(jnp.full((1, blk), NEG_INF, F32) for _ in range(n_streams)))
        k_own = keys(qi)
        _rolling_streams(
            n_streams, blk, s_scr, lambda st, rows: _dot(k_own[rows], qts[st]), None,
            lambda st, s: _flash_update(acc_scr, st, jnp.where(krow <= qcol, s, NEG_INF),
                                        [[vt_ref[0, st // 2, qi]]], ms[st], bound))

    _with_score_bound(bound_ref, attend)

    outs = []
    for h in range(n_heads):
        maps = [acc_scr[2 * h + c, 0:HEAD_DIM, :] / acc_scr[2 * h + c, HEAD_DIM:HEAD_DIM + 1, :] for c in range(2)]
        o = maps[0] - lam * maps[1]
        ms_o = jnp.mean(o * o, axis=0, keepdims=True)
        outs.append((o * lax.rsqrt(ms_o + EPS) * g_ref[...]) * (1.0 - lam_init))
    o_ref[...] = jnp.concatenate(outs, axis=0).T.astype(o_ref.dtype)


def _diff(bound, qc, kc, vtc, lam_rows, subln, *, batch, seq, lam_init, blk):
    t, width = qc.shape
    n_pairs = width // LANES
    n_heads = LANES // HEAD_DIM
    n_q = seq // blk
    return pl.pallas_call(
        functools.partial(_diff_kernel, blk=blk, lam_init=lam_init),
        grid=(batch, n_pairs, n_q),
        in_specs=[
            pl.BlockSpec(memory_space=pltpu.SMEM),
            pl.BlockSpec((blk, LANES), lambda b, hp, i: (b * n_q + i, hp)),
            pl.BlockSpec((seq, LANES), lambda b, hp, i: (b, hp)),
            pl.BlockSpec((1, n_heads, n_q, V_ROWS, blk), lambda b, hp, i: (b, hp, 0, 0, 0)),
            pl.BlockSpec((4, C_HALF), lambda b, hp, i: (0, 0)),
            pl.BlockSpec((HEAD_DIM, 1), lambda b, hp, i: (0, 0)),
        ],
        out_specs=pl.BlockSpec((blk, LANES), lambda b, hp, i: (b * n_q + i, hp)),
        out_shape=jax.ShapeDtypeStruct((t, width), BF16),
        scratch_shapes=[pltpu.VMEM((2 * n_heads, V_ROWS, blk), F32),
                        pltpu.VMEM((DIFF_PARKED_KEYS, blk), F32)],
        compiler_params=_params("parallel", "parallel", "arbitrary"),
        name="diff",
    )(bound, qc, kc, vtc, lam_rows, subln.reshape(HEAD_DIM, 1))


def _merge_kernel(x_ref, ya_ref, yb_ref, yc_ref, ga_ref, gb_ref, gc_ref, wa_ref, wb_ref, wc_ref, wo_ref, o_ref):
    merged = (jax.nn.sigmoid(ga_ref[...].astype(F32)) * _dot(ya_ref[...], wa_ref[...].astype(BF16))
              + jax.nn.sigmoid(gb_ref[...].astype(F32)) * _dot(yb_ref[...], wb_ref[...].astype(BF16))
              + jax.nn.sigmoid(gc_ref[...].astype(F32)) * _dot(yc_ref[...], wc_ref[...].astype(BF16)))
    o_ref[...] = x_ref[...] + _dot(merged.astype(BF16), wo_ref[...].astype(BF16))


def _merge(x2, ya, yb, yc, gates, w_pa, w_pb, w_pc, w_out, *, tm=512):
    t, d = x2.shape
    rows = lambda i: (i, 0)
    return pl.pallas_call(
        _merge_kernel,
        grid=(t // tm,),
        in_specs=[
            pl.BlockSpec((tm, d), rows),
            pl.BlockSpec((tm, ya.shape[1]), rows),
            pl.BlockSpec((tm, yb.shape[1]), rows),
            pl.BlockSpec((tm, yc.shape[1]), rows),
            pl.BlockSpec((tm, d), lambda i: (i, 0)),
            pl.BlockSpec((tm, d), lambda i: (i, 1)),
            pl.BlockSpec((tm, d), lambda i: (i, 2)),
            _resident(w_pa.shape),
            _resident(w_pb.shape),
            _resident(w_pc.shape),
            _resident(w_out.shape),
        ],
        out_specs=pl.BlockSpec((tm, d), rows),
        out_shape=jax.ShapeDtypeStruct((t, d), F32),
        compiler_params=_params("parallel"),
        name="merge",
    )(x2, ya, yb, yc, gates, gates, gates, w_pa, w_pb, w_pc, w_out)


def _convglu_kernel(x_ref, xh_ref, g_ref, wu_ref, cw_ref, cb_ref, wd_ref, o_ref, h_scr, u_scr, act_scr,
                    *, tm, tf, tiles_per_seq):
    i = pl.program_id(0)
    halo = CONV_HALO
    ff = wd_ref.shape[0]
    n_chunks = ff // tf

    def normed(x):
        ms = jnp.mean(x * x, axis=-1, keepdims=True)
        return x * lax.rsqrt(ms + EPS) * g_ref[...]

    keep = jnp.where(i % tiles_per_seq == 0, 0.0, 1.0)
    h_scr[0:halo, :] = (normed(xh_ref[...]) * keep).astype(BF16)
    h_scr[halo:, :] = normed(x_ref[...]).astype(BF16)

    def up(c):
        h = h_scr[...]
        u_scr[c % 2, 0] = _dot(h, wu_ref[:, c * tf:(c + 1) * tf])
        u_scr[c % 2, 1] = _dot(h, wu_ref[:, ff + c * tf:ff + (c + 1) * tf])

    def conv(c, half):
        col0 = half * ff + c * tf
        y = cb_ref[:, col0:col0 + tf]
        u = u_scr[c % 2, half]
        for j in range(CONV_W):
            back = CONV_W - 1 - j
            shifted = u if back == 0 else pltpu.roll(u, back, 0)
            y = y + shifted[halo:, :] * cw_ref[j:j + 1, col0:col0 + tf]
        return y

    up(0)
    for c in range(n_chunks):
        if c + 1 < n_chunks:
            up(c + 1)
        gate_u = conv(c, 0)
        val_u = conv(c, 1)
        act_scr[:, c * tf:(c + 1) * tf] = ((gate_u * jax.nn.sigmoid(gate_u)) * val_u).astype(BF16)

    o_ref[...] = x_ref[...] + _dot(act_scr[...], wd_ref[...].astype(BF16))


def _convglu(x2, gain, w_up, conv_w, conv_b, w_down, *, seq, tm=512, tf=256):
    t, d = x2.shape
    ff = w_down.shape[0]
    halo = CONV_HALO
    tiles_per_seq = seq // tm
    halo_blocks = tm // halo
    return pl.pallas_call(
        functools.partial(_convglu_kernel, tm=tm, tf=tf, tiles_per_seq=tiles_per_seq),
        grid=(t // tm,),
        in_specs=[
            pl.BlockSpec((tm, d), lambda i: (i, 0)),
            pl.BlockSpec((halo, d), lambda i: (jnp.maximum(i * halo_blocks - 1, 0), 0)),
            _resident((1, d)),
            _resident(w_up.shape),
            _resident(conv_w.shape),
            _resident((1, 2 * ff)),
            _resident(w_down.shape),
        ],
        out_specs=pl.BlockSpec((tm, d), lambda i: (i, 0)),
        out_shape=jax.ShapeDtypeStruct((t, d), F32),
        scratch_shapes=[
            pltpu.VMEM((tm + halo, d), BF16),
            pltpu.VMEM((2, 2, tm + halo, tf), F32),
            pltpu.VMEM((tm, ff), BF16),
        ],
        compiler_params=_params("parallel"),
        name="convglu",
    )(x2, x2, gain.reshape(1, d), w_up, conv_w, conv_b.reshape(1, -1), w_down)


def _rope_tables(seq, dim):
    inv = 1.0 / (ROPE_THETA ** (jnp.arange(0, dim, 2, dtype=F32) / dim))
    ang = jnp.arange(seq, dtype=F32)[:, None] * inv[None, :]
    return jnp.cos(ang), jnp.sin(ang)


def kernel(x, attn_norm, w_in, qn_a, kn_a, qn_b, kn_b, sinks, qn_c, kn_c, lam_q1, lam_k1, lam_q2, lam_k2, subln,
           w_pa, w_pb, w_pc, w_out, mlp_norm, w_up, conv_w, conv_b, w_down):
    batch, seq, d = x.shape
    depth = w_in.shape[0]
    x2 = x.reshape(batch * seq, d)

    def rope_block(dim):
        cos, sin = _rope_tables(seq, dim)
        reps = LANES // dim
        return (jnp.tile(jnp.concatenate([cos, cos], axis=-1), (1, reps)),
                jnp.tile(jnp.concatenate([-sin, sin], axis=-1), (1, reps)))

    tables = rope_block(HEAD_DIM) + rope_block(C_HALF)

    segments = (("qa", A_W), ("ka", A_W), ("va", A_W), ("qb", B_QW), ("kb", B_KVW), ("vb", B_KVW),
                ("qc", C_W), ("kc", C_W), ("vc", C_W), ("gates", N_BRANCH * d))

    for i in range(depth):
        lam_init = 0.8 - 0.6 * float(np.exp(-0.3 * i))
        qk_gains = jnp.stack([jnp.tile(g, LANES // g.shape[0])
                              for g in (qn_a[i], kn_a[i], qn_b[i], kn_b[i], qn_c[i], kn_c[i])]).astype(F32)
        qa, ka, kmean, vta, qb, kb, vtb, qc, kc, vtc, gates = _inproj(
            x2, attn_norm[i], w_in[i], qk_gains, tables, batch=batch, seq=seq, segments=segments)

        kmean = kmean.reshape(batch, seq // MOBA_BLOCK, A_W)
        ya = _moba(_score_bound(qn_a[i], kn_a[i], HEAD_DIM), qa, ka, vta, kmean, batch=batch, seq=seq)
        yb = _swa(qb, kb, vtb, sinks[i], batch=batch, seq=seq)
        lam_rows = jnp.stack([lam_q1[i], lam_k1[i], lam_q2[i], lam_k2[i]]).astype(F32)
        yc = _diff(_score_bound(qn_c[i], kn_c[i], C_HALF), qc, kc, vtc, lam_rows, subln[i], batch=batch, seq=seq, lam_init=lam_init, blk=DIFF_BLOCK)

        x2 = _merge(x2, ya, yb, yc, gates, w_pa[i], w_pb[i], w_pc[i],
                    w_out[i])
        x2 = _convglu(x2, mlp_norm[i], w_up[i].astype(BF16), conv_w[i], conv_b[i], w_down[i], seq=seq)

    return x2.reshape(batch, seq, d)
```

```python
import functools

import numpy as np
import jax
import jax.numpy as jnp
from jax import lax
from jax.experimental import pallas as pl
from jax.experimental.pallas import tpu as pltpu

F32 = jnp.float32
BF16 = jnp.bfloat16
NEG_INF = float("-inf")

LANES = 128
VMEM_LIMIT = 48 * 1024 * 1024

EPS = 1e-6
HEAD_DIM = 64
ROPE_THETA = 10000.0
A_HEADS = 4
MOBA_BLOCK = 256
MOBA_TOPK = 3
B_HEADS = 8
B_KV_HEADS = 2
WINDOW = 128
C_HEADS = 4
C_HALF = HEAD_DIM // 2
N_BRANCH = 3
CONV_W = 3
CONV_HALO = 16
BF16_SUBLANES = 16
V_ROWS = HEAD_DIM + BF16_SUBLANES
DIFF_BLOCK = 512
DIFF_PARKED_KEYS = 256
LOG2E = 1.4426950408889634

A_W = A_HEADS * HEAD_DIM
B_QW = B_HEADS * HEAD_DIM
B_KVW = B_KV_HEADS * HEAD_DIM
C_W = C_HEADS * HEAD_DIM


def _dot(a, b):
    return jnp.dot(a, b, preferred_element_type=F32)


def _split3(a):
    a1 = a.astype(BF16)
    r = a - a1.astype(F32)
    a2 = r.astype(BF16)
    a3 = (r - a2.astype(F32)).astype(BF16)
    return a1, a2, a3


def _resident(shape):
    return pl.BlockSpec(shape, lambda *_: (0,) * len(shape), pipeline_mode=pl.Buffered(1))


def _resident_layer(stacked, layer):
    tail = stacked.shape[1:]
    return pl.BlockSpec((None,) + tail, lambda *_: (layer,) + (0,) * len(tail), pipeline_mode=pl.Buffered(1))


def _params(*semantics):
    return pltpu.CompilerParams(dimension_semantics=semantics, vmem_limit_bytes=VMEM_LIMIT)


def _group_sum_matrix(w, hd):
    idx = np.arange(w) // hd
    same = (idx[:, None] == idx[None, :]).astype(np.float32)
    return jnp.asarray(np.concatenate([same, same], axis=0), dtype=BF16)


MM_COLS = 256


def _inproj_kernel(x_ref, g_ref, w_ref, gains_ref, cos64_ref, sin64_ref, cos32_ref, sin32_ref, gs64_ref, gs32_ref,
                   qa_ref, ka_ref, km_ref, vta_ref, qb_ref, kb_ref, vtb_ref, qc_ref, kc_ref, vtc_ref, gates_ref,
                   h_scr, *, blocks, n_qkv):
    x = x_ref[...]
    tm = x.shape[0]
    ms = jnp.mean(x * x, axis=-1, keepdims=True)
    h_scr[...] = (x * lax.rsqrt(ms + EPS) * g_ref[...]).astype(BF16)
    lane = lax.broadcasted_iota(jnp.int32, (tm, LANES), 1)

    def norm_rope(y, gain_row, hd, scale):
        cos_ref, sin_ref, gs_ref = ((cos64_ref, sin64_ref, gs64_ref) if hd == HEAD_DIM
                                    else (cos32_ref, sin32_ref, gs32_ref))
        sq = y * y
        hi = sq.astype(BF16)
        lo = (sq - hi.astype(F32)).astype(BF16)
        msq = _dot(jnp.concatenate([hi, lo], axis=1), gs_ref[...]) * (1.0 / hd)
        yn = y * lax.rsqrt(msq + EPS) * gains_ref[gain_row:gain_row + 1, :]
        half = hd // 2
        partner = jnp.where(lane % hd < half, pltpu.roll(yn, LANES - half, 1), pltpu.roll(yn, half, 1))
        out = yn * cos_ref[...] + partner * sin_ref[...]
        return out * scale if scale != 1.0 else out

    def store_vt(vt_ref, y, lb):
        tk = vt_ref.shape[-1]
        yt = y.T
        ones = jnp.ones((V_ROWS - HEAD_DIM, tk), F32)
        for hh in range(LANES // HEAD_DIM):
            for n in range(tm // tk):
                blk = yt[hh * HEAD_DIM:(hh + 1) * HEAD_DIM, n * tk:(n + 1) * tk]
                vt_ref[0, lb * (LANES // HEAD_DIM) + hh, n] = jnp.concatenate([blk, ones], axis=0).astype(BF16)

    q_scale = HEAD_DIM ** -0.5 * LOG2E
    c_scale = C_HALF ** -0.5 * LOG2E

    def epilogue(kind, lb, y):
        cols = slice(lb * LANES, (lb + 1) * LANES)
        if kind == "qa":
            qa_ref[:, cols] = norm_rope(y, 0, HEAD_DIM, 1.0)
        elif kind == "ka":
            yk = norm_rope(y, 1, HEAD_DIM, 1.0)
            ka_ref[:, cols] = yk.astype(BF16)
            for n in range(tm // MOBA_BLOCK):
                rows = yk[n * MOBA_BLOCK:(n + 1) * MOBA_BLOCK]
                km_ref[n, :, cols] = jnp.sum(rows, axis=0, keepdims=True) * (1.0 / MOBA_BLOCK)
        elif kind == "va":
            store_vt(vta_ref, y, lb)
        elif kind == "qb":
            qb_ref[:, cols] = norm_rope(y, 2, HEAD_DIM, q_scale)
        elif kind == "kb":
            kb_ref[:, cols] = norm_rope(y, 3, HEAD_DIM, 1.0).astype(BF16)
        elif kind == "vb":
            store_vt(vtb_ref, y, lb)
        elif kind == "qc":
            qc_ref[:, cols] = norm_rope(y, 4, C_HALF, c_scale)
        elif kind == "kc":
            kc_ref[:, cols] = norm_rope(y, 5, C_HALF, 1.0).astype(BF16)
        elif kind == "vc":
            store_vt(vtc_ref, y, lb)

    def chunk(c):
        return _dot(h_scr[...], w_ref[:, c * MM_COLS:(c + 1) * MM_COLS].astype(BF16))

    n_chunks = w_ref.shape[1] // MM_COLS
    y_next = chunk(0)
    for c in range(n_chunks):
        y = y_next
        if c + 1 < n_chunks:
            y_next = chunk(c + 1)
        col0 = c * MM_COLS
        if col0 < n_qkv:
            for part in range(MM_COLS // LANES):
                kind, lb = blocks[col0 // LANES + part]
                epilogue(kind, lb, y[:, part * LANES:(part + 1) * LANES])
        else:
            gates_ref[:, col0 - n_qkv:col0 - n_qkv + MM_COLS] = y.astype(gates_ref.dtype)


def _inproj(x2, gain, w_in, layer, qk_gains, tables, *, batch, seq, segments, tm=512):
    t, d = x2.shape
    n = w_in.shape[2]
    blocks = [(name, lb) for name, width in segments[:-1] for lb in range(width // LANES)]
    n_qkv = len(blocks) * LANES
    width = dict(segments)
    tps = seq // tm
    rows = lambda i: (i, 0)
    pos = lambda i: (i % tps, 0)

    def vt_spec(heads, tk):
        return pl.BlockSpec((1, heads, tm // tk, V_ROWS, tk), lambda i: (i // tps, 0, i % tps, 0, 0))

    def vt_shape(heads, tk):
        return jax.ShapeDtypeStruct((batch, heads, seq // tk, V_ROWS, tk), BF16)

    n_mb = tm // MOBA_BLOCK
    out_specs = [
        pl.BlockSpec((tm, width["qa"]), rows),
        pl.BlockSpec((tm, width["ka"]), rows),
        pl.BlockSpec((n_mb, 1, width["ka"]), lambda i: (i, 0, 0)),
        vt_spec(A_HEADS, MOBA_BLOCK),
        pl.BlockSpec((tm, width["qb"]), rows),
        pl.BlockSpec((tm, width["kb"]), rows),
        vt_spec(B_KV_HEADS, WINDOW),
        pl.BlockSpec((tm, width["qc"]), rows),
        pl.BlockSpec((tm, width["kc"]), rows),
        vt_spec(C_HEADS, DIFF_BLOCK),
        pl.BlockSpec((tm, width["gates"]), rows),
    ]
    out_shape = [
        jax.ShapeDtypeStruct((t, width["qa"]), F32),
        jax.ShapeDtypeStruct((t, width["ka"]), BF16),
        jax.ShapeDtypeStruct((t // MOBA_BLOCK, 1, width["ka"]), F32),
        vt_shape(A_HEADS, MOBA_BLOCK),
        jax.ShapeDtypeStruct((t, width["qb"]), F32),
        jax.ShapeDtypeStruct((t, width["kb"]), BF16),
        vt_shape(B_KV_HEADS, WINDOW),
        jax.ShapeDtypeStruct((t, width["qc"]), F32),
        jax.ShapeDtypeStruct((t, width["kc"]), BF16),
        vt_shape(C_HEADS, DIFF_BLOCK),
        jax.ShapeDtypeStruct((t, width["gates"]), BF16),
    ]
    table_spec = pl.BlockSpec((tm, LANES), pos)
    return pl.pallas_call(
        functools.partial(_inproj_kernel, blocks=blocks, n_qkv=n_qkv),
        grid=(t // tm,),
        in_specs=[pl.BlockSpec((tm, d), rows), _resident((1, d)), _resident_layer(w_in, layer),
                  _resident(qk_gains.shape),
                  table_spec, table_spec, table_spec, table_spec,
                  _resident((2 * LANES, LANES)), _resident((2 * LANES, LANES))],
        out_specs=out_specs,
        out_shape=out_shape,
        scratch_shapes=[pltpu.VMEM((tm, d), BF16)],
        compiler_params=_params("parallel"),
        name="inproj",
    )(x2, gain.reshape(1, d), w_in, qk_gains, *tables, _group_sum_matrix(LANES, HEAD_DIM),
      _group_sum_matrix(LANES, C_HALF))


def _flash_update(acc_ref, stream, s, vt_groups, m, bound=None):
    if bound is None:
        m_new = jnp.maximum(m, jnp.max(s, axis=0, keepdims=True))
        shift = jnp.where(m_new == NEG_INF, 0.0, m_new)
        alpha = jnp.exp2(m - shift)
    else:
        m_new, shift, alpha = m, bound, None
    p = jnp.exp2(s - shift).astype(BF16)
    cw = p.shape[1] // len(vt_groups)
    pvs = []
    for g, vts in enumerate(vt_groups):
        tk = vts[0].shape[1]
        pv = _dot(vts[0], p[0:tk, g * cw:(g + 1) * cw])
        for n in range(1, len(vts)):
            pv = pv + _dot(vts[n], p[n * tk:(n + 1) * tk, g * cw:(g + 1) * cw])
        pvs.append(pv)
    pv = pvs[0] if len(pvs) == 1 else jnp.concatenate(pvs, axis=1)
    acc_ref[stream] = acc_ref[stream] + pv if alpha is None else alpha * acc_ref[stream] + pv
    return m_new


MAX_FIXED_SHIFT = 60.0


def _score_bound(q_gain, k_gain, hd):
    return (jnp.max(jnp.abs(q_gain)) * jnp.max(jnp.abs(k_gain)) * (hd ** 0.5 * LOG2E)).reshape(1).astype(F32)


def _with_score_bound(bound_ref, attend):
    bound = bound_ref[0]
    fixed = bound <= MAX_FIXED_SHIFT

    @pl.when(fixed)
    def _():
        attend(bound)

    @pl.when(jnp.logical_not(fixed))
    def _():
        attend(None)


def _rolling_streams(n_streams, n_keys, s_scr, scores, next_scores, update):
    parked = s_scr.shape[0]
    out = []
    s = s_scr[...]
    if parked < n_keys:
        s = jnp.concatenate([s, scores(0, slice(parked, None))], axis=0)
    for st in range(n_streams):
        if st + 1 < n_streams:
            s_next = scores(st + 1, slice(None))
        elif next_scores is not None:
            s_scr[...] = next_scores(0, slice(0, parked))
        out.append(update(st, s))
        if st + 1 < n_streams:
            s = s_next
    return tuple(out)


def _prime_streams(s_scr, scores):
    s_scr[...] = scores(0, slice(0, s_scr.shape[0]))


def _moba_kernel(bound_ref, q_ref, k_ref, vt_ref, km_ref, o_ref, acc_scr, sel_scr, s_scr, *, tq):
    qi = pl.program_id(1)
    n_heads = q_ref.shape[1] // HEAD_DIM
    n_blk = km_ref.shape[1]
    per_block = LANES // HEAD_DIM
    qt = q_ref[...].T
    feat = lax.broadcasted_iota(jnp.int32, (LANES, tq), 0)
    blk = lax.broadcasted_iota(jnp.int32, (n_blk, tq), 0)
    blk_f = blk.astype(F32)
    krow = lax.broadcasted_iota(jnp.int32, (tq, per_block * tq), 0)
    qcol = lax.broadcasted_iota(jnp.int32, (tq, per_block * tq), 1) % tq
    causal = krow <= qcol

    n_streams = n_heads // per_block

    def lane_block(x, b):
        return x[:, b * LANES:(b + 1) * LANES]

    qhs = []
    for h in range(n_heads):
        b, hh = divmod(h, per_block)
        qhs.append(jnp.where((feat >= hh * HEAD_DIM) & (feat < (hh + 1) * HEAD_DIM),
                             qt[b * LANES:(b + 1) * LANES], 0.0))
    qts = [jnp.concatenate([(qh * (HEAD_DIM ** -0.5 * LOG2E)).astype(BF16)
                            for qh in qhs[b * per_block:(b + 1) * per_block]], axis=1) for b in range(n_streams)]

    def pair_keys(t):
        return k_ref[pl.ds(pl.multiple_of(t * 2 * tq, 2 * tq), 2 * tq), :]

    k_first = pair_keys(0)
    _prime_streams(s_scr, lambda b, rows: _dot(lane_block(k_first[rows], b), qts[b]))

    sels = []
    for h in range(n_heads):
        b, hh = divmod(h, per_block)
        if hh == 0:
            km_parts = _split3(km_ref[0, :, b * LANES:(b + 1) * LANES])

        k1, k2, k3 = km_parts
        q1, q2, q3 = _split3(qhs[h])
        gate = (_dot(k1, q1) + _dot(k2, q1) + _dot(k1, q2) + _dot(k3, q1) + _dot(k2, q2) + _dot(k1, q3))
        g = jnp.where(blk < qi, gate, NEG_INF)
        sel = jnp.zeros((n_blk, tq), F32)
        for _ in range(MOBA_TOPK):
            mx = jnp.max(g, axis=0, keepdims=True)
            first = jnp.min(jnp.where((g == mx) & (mx > NEG_INF), blk_f, float(n_blk)), axis=0, keepdims=True)
            pick = blk_f == first
            sel = jnp.where(pick, 1.0, sel)
            g = jnp.where(pick, NEG_INF, g)
        sels.append(sel)
        if hh == per_block - 1:
            sel_scr[b] = jnp.concatenate(sels, axis=1)
            sels = []

    acc_scr[...] = jnp.zeros_like(acc_scr)

    def pair_values(b, t):
        return [[vt_ref[0, b * per_block + hh, 2 * t], vt_ref[0, b * per_block + hh, 2 * t + 1]]
                for hh in range(per_block)]

    def masked(s, first, second):
        return jnp.concatenate([jnp.where(first, s[0:tq], NEG_INF), jnp.where(second, s[tq:2 * tq], NEG_INF)], axis=0)

    last = qi // 2
    own_second = qi % 2 == 1

    def attend(bound):
        def body(t, ms):
            kj = pair_keys(t)
            kn = pair_keys(t + 1)

            def update(b, s):
                first = sel_scr[b, pl.ds(2 * t, 1), :] > 0.0
                second = sel_scr[b, pl.ds(2 * t + 1, 1), :] > 0.0
                return _flash_update(acc_scr, b, masked(s, first, second), pair_values(b, t), ms[b], bound)

            return _rolling_streams(n_streams, 2 * tq, s_scr,
                                    lambda b, rows: _dot(lane_block(kj[rows], b), qts[b]),
                                    lambda b, rows: _dot(lane_block(kn[rows], b), qts[b]), update)

        ms = lax.fori_loop(0, last, body,
                           tuple(jnp.full((1, per_block * tq), NEG_INF, F32) for _ in range(n_streams)))
        k_last = pair_keys(last)

        def update_last(b, s):
            picked = sel_scr[b, pl.ds(2 * last, 1), :] > 0.0
            first = (own_second & picked) | (jnp.logical_not(own_second) & causal)
            second = own_second & causal
            return _flash_update(acc_scr, b, masked(s, first, second), pair_values(b, last), ms[b], bound)

        _rolling_streams(n_streams, 2 * tq, s_scr, lambda b, rows: _dot(lane_block(k_last[rows], b), qts[b]), None,
                         update_last)

    _with_score_bound(bound_ref, attend)

    outs = []
    for b in range(n_streams):
        acc = acc_scr[b]
        o = acc[0:HEAD_DIM] / acc[HEAD_DIM:HEAD_DIM + 1]
        outs += [o[:, hh * tq:(hh + 1) * tq] for hh in range(per_block)]
    o_ref[...] = jnp.concatenate(outs, axis=0).T.astype(o_ref.dtype)


def _moba(bound, qa, ka, vta, kmean, *, batch, seq):
    tq = MOBA_BLOCK
    t, width = qa.shape
    n_heads = width // HEAD_DIM
    n_q = seq // tq
    n_streams = width // LANES
    cols = (LANES // HEAD_DIM) * tq
    return pl.pallas_call(
        functools.partial(_moba_kernel, tq=tq),
        grid=(batch, n_q),
        in_specs=[
            pl.BlockSpec(memory_space=pltpu.SMEM),
            pl.BlockSpec((tq, width), lambda b, i: (b * n_q + i, 0)),
            pl.BlockSpec((seq, width), lambda b, i: (b, 0)),
            pl.BlockSpec((1, n_heads, n_q, V_ROWS, tq), lambda b, i: (b, 0, 0, 0, 0)),
            pl.BlockSpec((1, n_q, width), lambda b, i: (b, 0, 0)),
        ],
        out_specs=pl.BlockSpec((tq, width), lambda b, i: (b * n_q + i, 0)),
        out_shape=jax.ShapeDtypeStruct((t, width), BF16),
        scratch_shapes=[pltpu.VMEM((n_streams, V_ROWS, cols), F32), pltpu.VMEM((n_streams, n_q, cols), F32),
                        pltpu.VMEM((2 * tq, cols), F32)],
        compiler_params=_params("parallel", "arbitrary"),
        name="moba",
    )(bound, qa, ka, vta, kmean)


def _swa_kernel(q_ref, kp_ref, kc_ref, vtp_ref, vtc_ref, sink_ref, o_ref):
    i = pl.program_id(1)
    w = WINDOW
    n_sub = q_ref.shape[0] // w
    group = B_HEADS // B_KV_HEADS
    qt = q_ref[...].T
    krow = lax.broadcasted_iota(jnp.int32, (2 * w, w), 0)
    qcol = lax.broadcasted_iota(jnp.int32, (2 * w, w), 1)
    rel = qcol + w - krow
    in_window = (rel >= 0) & (rel < w)
    zeros = jnp.zeros((HEAD_DIM, w), F32)

    def key_block(j):
        return kp_ref[...] if j < 0 else kc_ref[j * w:(j + 1) * w, :]

    def value_block(kv, j):
        return vtp_ref[0, kv, 0] if j < 0 else vtc_ref[0, kv, j]

    def scores(j, kv):
        cols = []
        for g in range(group):
            h = kv * group + g
            qh = qt[h * HEAD_DIM:(h + 1) * HEAD_DIM, j * w:(j + 1) * w]
            cols.append(jnp.concatenate([qh if n == kv else zeros for n in range(B_KV_HEADS)], axis=0))
        kband = jnp.concatenate([key_block(j - 1), key_block(j)], axis=0)
        return _dot(kband, jnp.concatenate(cols, axis=1).astype(BF16))

    jobs = [(j, kv) for j in range(n_sub) for kv in range(B_KV_HEADS)]
    all_scores = [scores(j, kv) for j, kv in jobs]
    outs = {}
    for (j, kv), s in zip(jobs, all_scores):
        ok = in_window & ((krow >= w) | (i > 0)) if j == 0 else in_window
        s = jnp.where(jnp.concatenate([ok] * group, axis=1), s, NEG_INF)
        sink = sink_ref[:, kv * group * w:(kv + 1) * group * w] * LOG2E
        m = jnp.maximum(jnp.max(s, axis=0, keepdims=True), sink)
        p = jnp.exp2(s - m).astype(BF16)
        pv = _dot(value_block(kv, j - 1), p[0:w]) + _dot(value_block(kv, j), p[w:2 * w])
        o = pv[0:HEAD_DIM] / (pv[HEAD_DIM:HEAD_DIM + 1] + jnp.exp2(sink - m))
        for g in range(group):
            outs[(kv * group + g, j)] = o[:, g * w:(g + 1) * w]
    ot = jnp.concatenate([jnp.concatenate([outs[(h, j)] for j in range(n_sub)], axis=1) for h in range(B_HEADS)],
                         axis=0)
    o_ref[...] = ot.T.astype(o_ref.dtype)


def _swa(qb, kb, vtb, sinks, *, batch, seq, n_sub=4):
    t, qw = qb.shape
    w = WINDOW
    nb = seq // w
    steps = nb // n_sub
    prev = lambda b, i: (b * nb + jnp.maximum(i * n_sub - 1, 0), 0)
    cur = lambda b, i: (b * steps + i, 0)
    sink_row = jnp.repeat(sinks.astype(F32), w).reshape(1, B_HEADS * w)
    return pl.pallas_call(
        _swa_kernel,
        grid=(batch, steps),
        in_specs=[pl.BlockSpec((n_sub * w, qw), cur),
                  pl.BlockSpec((w, LANES), prev),
                  pl.BlockSpec((n_sub * w, LANES), cur),
                  pl.BlockSpec((1, B_KV_HEADS, 1, V_ROWS, w),
                               lambda b, i: (b, 0, jnp.maximum(i * n_sub - 1, 0), 0, 0)),
                  pl.BlockSpec((1, B_KV_HEADS, n_sub, V_ROWS, w), lambda b, i: (b, 0, i, 0, 0)),
                  pl.BlockSpec((1, B_HEADS * w), lambda b, i: (0, 0))],
        out_specs=pl.BlockSpec((n_sub * w, qw), cur),
        out_shape=jax.ShapeDtypeStruct((t, qw), BF16),
        compiler_params=_params("parallel", "arbitrary"),
        name="swa",
    )(qb, kb, kb, vtb, vtb, sink_row)


def _diff_kernel(bound_ref, q_ref, k_ref, vt_ref, lam_ref, g_ref, o_ref, acc_scr, s_scr, *, blk, lam_init):
    qi = pl.program_id(2)
    n_heads = LANES // HEAD_DIM
    qt = q_ref[...].T
    feat = lax.broadcasted_iota(jnp.int32, (LANES, blk), 0)
    krow = lax.broadcasted_iota(jnp.int32, (blk, blk), 0)
    qcol = lax.broadcasted_iota(jnp.int32, (blk, blk), 1)

    lq = lam_ref[...]
    lam = (jnp.exp(jnp.sum(lq[0:1] * lq[1:2], axis=1, keepdims=True))
           - jnp.exp(jnp.sum(lq[2:3] * lq[3:4], axis=1, keepdims=True)) + lam_init)

    n_streams = 2 * n_heads
    qts = [jnp.where((feat >= st * C_HALF) & (feat < (st + 1) * C_HALF), qt, 0.0).astype(BF16)
           for st in range(n_streams)]
    acc_scr[...] = jnp.zeros_like(acc_scr)

    def keys(j):
        return k_ref[pl.ds(pl.multiple_of(j * blk, blk), blk), :]

    k_first = keys(0)
    _prime_streams(s_scr, lambda st, rows: _dot(k_first[rows], qts[st]))

    def attend(bound):
        def body(j, ms):
            kj = keys(j)
            kn = keys(j + 1)
            return _rolling_streams(
                n_streams, blk, s_scr,
                lambda st, rows: _dot(kj[rows], qts[st]), lambda st, rows: _dot(kn[rows], qts[st]),
                lambda st, s: _flash_update(acc_scr, st, s, [[vt_ref[0, st // 2, j]]], ms[st], bound))

        ms = lax.fori_loop(0, qi, body, tuple(jnp.full((1, blk), NEG_INF, F32) for _ in range(n_streams)))
        k_own = keys(qi)
        _rolling_streams(
            n_streams, blk, s_scr, lambda st, rows: _dot(k_own[rows], qts[st]), None,
            lambda st, s: _flash_update(acc_scr, st, jnp.where(krow <= qcol, s, NEG_INF),
                                        [[vt_ref[0, st // 2, qi]]], ms[st], bound))

    _with_score_bound(bound_ref, attend)

    outs = []
    for h in range(n_heads):
        maps = [acc_scr[2 * h + c, 0:HEAD_DIM, :] / acc_scr[2 * h + c, HEAD_DIM:HEAD_DIM + 1, :] for c in range(2)]
        o = maps[0] - lam * maps[1]
        ms_o = jnp.mean(o * o, axis=0, keepdims=True)
        outs.append((o * lax.rsqrt(ms_o + EPS) * g_ref[...]) * (1.0 - lam_init))
    o_ref[...] = jnp.concatenate(outs, axis=0).T.astype(o_ref.dtype)


def _diff(bound, qc, kc, vtc, lam_rows, subln, *, batch, seq, lam_init, blk):
    t, width = qc.shape
    n_pairs = width // LANES
    n_heads = LANES // HEAD_DIM
    n_q = seq // blk
    return pl.pallas_call(
        functools.partial(_diff_kernel, blk=blk, lam_init=lam_init),
        grid=(batch, n_pairs, n_q),
        in_specs=[
            pl.BlockSpec(memory_space=pltpu.SMEM),
            pl.BlockSpec((blk, LANES), lambda b, hp, i: (b * n_q + i, hp)),
            pl.BlockSpec((seq, LANES), lambda b, hp, i: (b, hp)),
            pl.BlockSpec((1, n_heads, n_q, V_ROWS, blk), lambda b, hp, i: (b, hp, 0, 0, 0)),
            pl.BlockSpec((4, C_HALF), lambda b, hp, i: (0, 0)),
            pl.BlockSpec((HEAD_DIM, 1), lambda b, hp, i: (0, 0)),
        ],
        out_specs=pl.BlockSpec((blk, LANES), lambda b, hp, i: (b * n_q + i, hp)),
        out_shape=jax.ShapeDtypeStruct((t, width), BF16),
        scratch_shapes=[pltpu.VMEM((2 * n_heads, V_ROWS, blk), F32),
                        pltpu.VMEM((DIFF_PARKED_KEYS, blk), F32)],
        compiler_params=_params("parallel", "parallel", "arbitrary"),
        name="diff",
    )(bound, qc, kc, vtc, lam_rows, subln.reshape(HEAD_DIM, 1))


def _merge_kernel(x_ref, ya_ref, yb_ref, yc_ref, ga_ref, gb_ref, gc_ref, wa_ref, wb_ref, wc_ref, wo_ref, o_ref):
    merged = (jax.nn.sigmoid(ga_ref[...].astype(F32)) * _dot(ya_ref[...], wa_ref[...].astype(BF16))
              + jax.nn.sigmoid(gb_ref[...].astype(F32)) * _dot(yb_ref[...], wb_ref[...].astype(BF16))
              + jax.nn.sigmoid(gc_ref[...].astype(F32)) * _dot(yc_ref[...], wc_ref[...].astype(BF16)))
    o_ref[...] = x_ref[...] + _dot(merged.astype(BF16), wo_ref[...].astype(BF16))


def _merge(x2, ya, yb, yc, gates, w_pa, w_pb, w_pc, w_out, layer, *, tm=512):
    t, d = x2.shape
    rows = lambda i: (i, 0)
    return pl.pallas_call(
        _merge_kernel,
        grid=(t // tm,),
        in_specs=[
            pl.BlockSpec((tm, d), rows),
            pl.BlockSpec((tm, ya.shape[1]), rows),
            pl.BlockSpec((tm, yb.shape[1]), rows),
            pl.BlockSpec((tm, yc.shape[1]), rows),
            pl.BlockSpec((tm, d), lambda i: (i, 0)),
            pl.BlockSpec((tm, d), lambda i: (i, 1)),
            pl.BlockSpec((tm, d), lambda i: (i, 2)),
            _resident_layer(w_pa, layer),
            _resident_layer(w_pb, layer),
            _resident_layer(w_pc, layer),
            _resident_layer(w_out, layer),
        ],
        out_specs=pl.BlockSpec((tm, d), rows),
        out_shape=jax.ShapeDtypeStruct((t, d), F32),
        compiler_params=_params("parallel"),
        name="merge",
    )(x2, ya, yb, yc, gates, gates, gates, w_pa, w_pb, w_pc, w_out)


def _convglu_kernel(x_ref, xh_ref, g_ref, wu_ref, cw_ref, cb_ref, wd_ref, o_ref, h_scr, u_scr, act_scr,
                    *, tm, tf, tiles_per_seq):
    i = pl.program_id(0)
    halo = CONV_HALO
    ff = wd_ref.shape[0]
    n_chunks = ff // tf

    def normed(x):
        ms = jnp.mean(x * x, axis=-1, keepdims=True)
        return x * lax.rsqrt(ms + EPS) * g_ref[...]

    keep = jnp.where(i % tiles_per_seq == 0, 0.0, 1.0)
    h_scr[0:halo, :] = (normed(xh_ref[...]) * keep).astype(BF16)
    h_scr[halo:, :] = normed(x_ref[...]).astype(BF16)

    def up(c):
        h = h_scr[...]
        u_scr[c % 2, 0] = _dot(h, wu_ref[:, c * tf:(c + 1) * tf])
        u_scr[c % 2, 1] = _dot(h, wu_ref[:, ff + c * tf:ff + (c + 1) * tf])

    def conv(c, half):
        col0 = half * ff + c * tf
        y = cb_ref[:, col0:col0 + tf]
        u = u_scr[c % 2, half]
        for j in range(CONV_W):
            back = CONV_W - 1 - j
            shifted = u if back == 0 else pltpu.roll(u, back, 0)
            y = y + shifted[halo:, :] * cw_ref[j:j + 1, col0:col0 + tf]
        return y

    up(0)
    for c in range(n_chunks):
        if c + 1 < n_chunks:
            up(c + 1)
        gate_u = conv(c, 0)
        val_u = conv(c, 1)
        act_scr[:, c * tf:(c + 1) * tf] = ((gate_u * jax.nn.sigmoid(gate_u)) * val_u).astype(BF16)

    o_ref[...] = x_ref[...] + _dot(act_scr[...], wd_ref[...].astype(BF16))


def _convglu(x2, gain, w_up, conv_w, conv_b, w_down, layer, *, seq, tm=512, tf=256):
    t, d = x2.shape
    ff = w_down.shape[1]
    halo = CONV_HALO
    tiles_per_seq = seq // tm
    halo_blocks = tm // halo
    return pl.pallas_call(
        functools.partial(_convglu_kernel, tm=tm, tf=tf, tiles_per_seq=tiles_per_seq),
        grid=(t // tm,),
        in_specs=[
            pl.BlockSpec((tm, d), lambda i: (i, 0)),
            pl.BlockSpec((halo, d), lambda i: (jnp.maximum(i * halo_blocks - 1, 0), 0)),
            _resident((1, d)),
            _resident_layer(w_up, layer),
            _resident(conv_w.shape),
            _resident((1, 2 * ff)),
            _resident_layer(w_down, layer),
        ],
        out_specs=pl.BlockSpec((tm, d), lambda i: (i, 0)),
        out_shape=jax.ShapeDtypeStruct((t, d), F32),
        scratch_shapes=[
            pltpu.VMEM((tm + halo, d), BF16),
            pltpu.VMEM((2, 2, tm + halo, tf), F32),
            pltpu.VMEM((tm, ff), BF16),
        ],
        compiler_params=_params("parallel"),
        name="convglu",
    )(x2, x2, gain.reshape(1, d), w_up, conv_w, conv_b.reshape(1, -1), w_down)


def _rope_tables(seq, dim):
    inv = 1.0 / (ROPE_THETA ** (jnp.arange(0, dim, 2, dtype=F32) / dim))
    ang = jnp.arange(seq, dtype=F32)[:, None] * inv[None, :]
    return jnp.cos(ang), jnp.sin(ang)


def kernel(x, attn_norm, w_in, qn_a, kn_a, qn_b, kn_b, sinks, qn_c, kn_c, lam_q1, lam_k1, lam_q2, lam_k2, subln,
           w_pa, w_pb, w_pc, w_out, mlp_norm, w_up, conv_w, conv_b, w_down):
    batch, seq, d = x.shape
    depth = w_in.shape[0]
    x2 = x.reshape(batch * seq, d)

    def rope_block(dim):
        cos, sin = _rope_tables(seq, dim)
        reps = LANES // dim
        return (jnp.tile(jnp.concatenate([cos, cos], axis=-1), (1, reps)),
                jnp.tile(jnp.concatenate([-sin, sin], axis=-1), (1, reps)))

    tables = rope_block(HEAD_DIM) + rope_block(C_HALF)

    segments = (("qa", A_W), ("ka", A_W), ("va", A_W), ("qb", B_QW), ("kb", B_KVW), ("vb", B_KVW),
                ("qc", C_W), ("kc", C_W), ("vc", C_W), ("gates", N_BRANCH * d))

    w_up_bf = w_up.astype(BF16)

    for i in range(depth):
        lam_init = 0.8 - 0.6 * float(np.exp(-0.3 * i))
        qk_gains = jnp.stack([jnp.tile(g, LANES // g.shape[0])
                              for g in (qn_a[i], kn_a[i], qn_b[i], kn_b[i], qn_c[i], kn_c[i])]).astype(F32)
        qa, ka, kmean, vta, qb, kb, vtb, qc, kc, vtc, gates = _inproj(
            x2, attn_norm[i], w_in, i, qk_gains, tables, batch=batch, seq=seq, segments=segments)

        kmean = kmean.reshape(batch, seq // MOBA_BLOCK, A_W)
        ya = _moba(_score_bound(qn_a[i], kn_a[i], HEAD_DIM), qa, ka, vta, kmean, batch=batch, seq=seq)
        yb = _swa(qb, kb, vtb, sinks[i], batch=batch, seq=seq)
        lam_rows = jnp.stack([lam_q1[i], lam_k1[i], lam_q2[i], lam_k2[i]]).astype(F32)
        yc = _diff(_score_bound(qn_c[i], kn_c[i], C_HALF), qc, kc, vtc, lam_rows, subln[i], batch=batch, seq=seq, lam_init=lam_init, blk=DIFF_BLOCK)

        x2 = _merge(x2, ya, yb, yc, gates, w_pa, w_pb, w_pc, w_out, i)
        x2 = _convglu(x2, mlp_norm[i], w_up_bf, conv_w[i], conv_b[i], w_down, i, seq=seq)

    return x2.reshape(batch, seq, d)
```

```python
import functools

import numpy as np
import jax
import jax.numpy as jnp
from jax import lax
from jax.experimental import pallas as pl
from jax.experimental.pallas import tpu as pltpu

F32 = jnp.float32
BF16 = jnp.bfloat16
NEG_INF = float("-inf")

LANES = 128
VMEM_LIMIT = 48 * 1024 * 1024

EPS = 1e-6
HEAD_DIM = 64
ROPE_THETA = 10000.0
A_HEADS = 4
MOBA_BLOCK = 256
MOBA_TOPK = 3
B_HEADS = 8
B_KV_HEADS = 2
WINDOW = 128
C_HEADS = 4
C_HALF = HEAD_DIM // 2
N_BRANCH = 3
CONV_W = 3
CONV_HALO = 16
BF16_SUBLANES = 16
V_ROWS = HEAD_DIM + BF16_SUBLANES
DIFF_BLOCK = 512
DIFF_PARKED_KEYS = 256
LOG2E = 1.4426950408889634

A_W = A_HEADS * HEAD_DIM
B_QW = B_HEADS * HEAD_DIM
B_KVW = B_KV_HEADS * HEAD_DIM
C_W = C_HEADS * HEAD_DIM


def _dot(a, b):
    return jnp.dot(a, b, preferred_element_type=F32)


def _split3(a):
    a1 = a.astype(BF16)
    r = a - a1.astype(F32)
    a2 = r.astype(BF16)
    a3 = (r - a2.astype(F32)).astype(BF16)
    return a1, a2, a3


def _resident(shape):
    return pl.BlockSpec(shape, lambda *_: (0,) * len(shape), pipeline_mode=pl.Buffered(1))


def _resident_layer(stacked, layer):
    tail = stacked.shape[1:]
    return pl.BlockSpec((None,) + tail, lambda *_: (layer,) + (0,) * len(tail), pipeline_mode=pl.Buffered(1))


def _params(*semantics):
    return pltpu.CompilerParams(dimension_semantics=semantics, vmem_limit_bytes=VMEM_LIMIT)


def _group_sum_matrix(w, hd):
    idx = np.arange(w) // hd
    same = (idx[:, None] == idx[None, :]).astype(np.float32)
    return jnp.asarray(np.concatenate([same, same], axis=0), dtype=BF16)


MM_COLS = 256


def _inproj_kernel(x_ref, g_ref, w_ref, gains_ref, cos64_ref, sin64_ref, cos32_ref, sin32_ref, gs64_ref, gs32_ref,
                   qa_ref, ka_ref, km_ref, vta_ref, qb_ref, kb_ref, vtb_ref, qc_ref, kc_ref, vtc_ref, gates_ref,
                   h_scr, *, blocks, n_qkv):
    tm = x_ref.shape[0]

    def normalise(rows):
        x = x_ref[rows, :]
        ms = jnp.mean(x * x, axis=-1, keepdims=True)
        h_scr[rows, :] = (x * lax.rsqrt(ms + EPS) * g_ref[...]).astype(BF16)

    top, bottom = slice(0, tm // 2), slice(tm // 2, tm)
    normalise(top)
    first_top = _dot(h_scr[top, :], w_ref[:, 0:MM_COLS].astype(BF16))
    normalise(bottom)
    first_chunk = jnp.concatenate([first_top, _dot(h_scr[bottom, :], w_ref[:, 0:MM_COLS].astype(BF16))], axis=0)
    lane = lax.broadcasted_iota(jnp.int32, (tm, LANES), 1)

    def norm_rope(y, gain_row, hd, scale):
        cos_ref, sin_ref, gs_ref = ((cos64_ref, sin64_ref, gs64_ref) if hd == HEAD_DIM
                                    else (cos32_ref, sin32_ref, gs32_ref))
        sq = y * y
        hi = sq.astype(BF16)
        lo = (sq - hi.astype(F32)).astype(BF16)
        msq = _dot(jnp.concatenate([hi, lo], axis=1), gs_ref[...]) * (1.0 / hd)
        yn = y * lax.rsqrt(msq + EPS) * gains_ref[gain_row:gain_row + 1, :]
        half = hd // 2
        partner = jnp.where(lane % hd < half, pltpu.roll(yn, LANES - half, 1), pltpu.roll(yn, half, 1))
        out = yn * cos_ref[...] + partner * sin_ref[...]
        return out * scale if scale != 1.0 else out

    def store_vt(vt_ref, y, lb):
        tk = vt_ref.shape[-1]
        yt = y.T
        ones = jnp.ones((V_ROWS - HEAD_DIM, tk), F32)
        for hh in range(LANES // HEAD_DIM):
            for n in range(tm // tk):
                blk = yt[hh * HEAD_DIM:(hh + 1) * HEAD_DIM, n * tk:(n + 1) * tk]
                vt_ref[0, lb * (LANES // HEAD_DIM) + hh, n] = jnp.concatenate([blk, ones], axis=0).astype(BF16)

    q_scale = HEAD_DIM ** -0.5 * LOG2E
    c_scale = C_HALF ** -0.5 * LOG2E

    def epilogue(kind, lb, y):
        cols = slice(lb * LANES, (lb + 1) * LANES)
        if kind == "qa":
            qa_ref[:, cols] = norm_rope(y, 0, HEAD_DIM, 1.0)
        elif kind == "ka":
            yk = norm_rope(y, 1, HEAD_DIM, 1.0)
            ka_ref[:, cols] = yk.astype(BF16)
            for n in range(tm // MOBA_BLOCK):
                rows = yk[n * MOBA_BLOCK:(n + 1) * MOBA_BLOCK]
                km_ref[n, :, cols] = jnp.sum(rows, axis=0, keepdims=True) * (1.0 / MOBA_BLOCK)
        elif kind == "va":
            store_vt(vta_ref, y, lb)
        elif kind == "qb":
            qb_ref[:, cols] = norm_rope(y, 2, HEAD_DIM, q_scale)
        elif kind == "kb":
            kb_ref[:, cols] = norm_rope(y, 3, HEAD_DIM, 1.0).astype(BF16)
        elif kind == "vb":
            store_vt(vtb_ref, y, lb)
        elif kind == "qc":
            qc_ref[:, cols] = norm_rope(y, 4, C_HALF, c_scale)
        elif kind == "kc":
            kc_ref[:, cols] = norm_rope(y, 5, C_HALF, 1.0).astype(BF16)
        elif kind == "vc":
            store_vt(vtc_ref, y, lb)

    def chunk(c):
        return _dot(h_scr[...], w_ref[:, c * MM_COLS:(c + 1) * MM_COLS].astype(BF16))

    n_chunks = w_ref.shape[1] // MM_COLS
    y_next = first_chunk
    for c in range(n_chunks):
        y = y_next
        if c + 1 < n_chunks:
            y_next = chunk(c + 1)
        col0 = c * MM_COLS
        if col0 < n_qkv:
            for part in range(MM_COLS // LANES):
                kind, lb = blocks[col0 // LANES + part]
                epilogue(kind, lb, y[:, part * LANES:(part + 1) * LANES])
        else:
            gates_ref[:, col0 - n_qkv:col0 - n_qkv + MM_COLS] = y.astype(gates_ref.dtype)


def _inproj(x2, gain, w_in, layer, qk_gains, tables, *, batch, seq, segments, tm=512):
    t, d = x2.shape
    n = w_in.shape[2]
    blocks = [(name, lb) for name, width in segments[:-1] for lb in range(width // LANES)]
    n_qkv = len(blocks) * LANES
    width = dict(segments)
    tps = seq // tm
    rows = lambda i: (i, 0)
    pos = lambda i: (i % tps, 0)

    def vt_spec(heads, tk):
        return pl.BlockSpec((1, heads, tm // tk, V_ROWS, tk), lambda i: (i // tps, 0, i % tps, 0, 0))

    def vt_shape(heads, tk):
        return jax.ShapeDtypeStruct((batch, heads, seq // tk, V_ROWS, tk), BF16)

    n_mb = tm // MOBA_BLOCK
    out_specs = [
        pl.BlockSpec((tm, width["qa"]), rows),
        pl.BlockSpec((tm, width["ka"]), rows),
        pl.BlockSpec((n_mb, 1, width["ka"]), lambda i: (i, 0, 0)),
        vt_spec(A_HEADS, MOBA_BLOCK),
        pl.BlockSpec((tm, width["qb"]), rows),
        pl.BlockSpec((tm, width["kb"]), rows),
        vt_spec(B_KV_HEADS, WINDOW),
        pl.BlockSpec((tm, width["qc"]), rows),
        pl.BlockSpec((tm, width["kc"]), rows),
        vt_spec(C_HEADS, DIFF_BLOCK),
        pl.BlockSpec((tm, width["gates"]), rows),
    ]
    out_shape = [
        jax.ShapeDtypeStruct((t, width["qa"]), F32),
        jax.ShapeDtypeStruct((t, width["ka"]), BF16),
        jax.ShapeDtypeStruct((t // MOBA_BLOCK, 1, width["ka"]), F32),
        vt_shape(A_HEADS, MOBA_BLOCK),
        jax.ShapeDtypeStruct((t, width["qb"]), F32),
        jax.ShapeDtypeStruct((t, width["kb"]), BF16),
        vt_shape(B_KV_HEADS, WINDOW),
        jax.ShapeDtypeStruct((t, width["qc"]), F32),
        jax.ShapeDtypeStruct((t, width["kc"]), BF16),
        vt_shape(C_HEADS, DIFF_BLOCK),
        jax.ShapeDtypeStruct((t, width["gates"]), BF16),
    ]
    table_spec = pl.BlockSpec((tm, LANES), pos)
    return pl.pallas_call(
        functools.partial(_inproj_kernel, blocks=blocks, n_qkv=n_qkv),
        grid=(t // tm,),
        in_specs=[pl.BlockSpec((tm, d), rows), _resident((1, d)), _resident_layer(w_in, layer),
                  _resident(qk_gains.shape),
                  table_spec, table_spec, table_spec, table_spec,
                  _resident((2 * LANES, LANES)), _resident((2 * LANES, LANES))],
        out_specs=out_specs,
        out_shape=out_shape,
        scratch_shapes=[pltpu.VMEM((tm, d), BF16)],
        compiler_params=_params("parallel"),
        name="inproj",
    )(x2, gain.reshape(1, d), w_in, qk_gains, *tables, _group_sum_matrix(LANES, HEAD_DIM),
      _group_sum_matrix(LANES, C_HALF))


def _flash_update(acc_ref, stream, s, vt_groups, m, bound=None):
    if bound is None:
        m_new = jnp.maximum(m, jnp.max(s, axis=0, keepdims=True))
        shift = jnp.where(m_new == NEG_INF, 0.0, m_new)
        alpha = jnp.exp2(m - shift)
    else:
        m_new, shift, alpha = m, bound, None
    p = jnp.exp2(s - shift).astype(BF16)
    cw = p.shape[1] // len(vt_groups)
    pvs = []
    for g, vts in enumerate(vt_groups):
        tk = vts[0].shape[1]
        pv = _dot(vts[0], p[0:tk, g * cw:(g + 1) * cw])
        for n in range(1, len(vts)):
            pv = pv + _dot(vts[n], p[n * tk:(n + 1) * tk, g * cw:(g + 1) * cw])
        pvs.append(pv)
    pv = pvs[0] if len(pvs) == 1 else jnp.concatenate(pvs, axis=1)
    acc_ref[stream] = acc_ref[stream] + pv if alpha is None else alpha * acc_ref[stream] + pv
    return m_new


MAX_FIXED_SHIFT = 60.0


def _score_bound(q_gain, k_gain, hd):
    return (jnp.max(jnp.abs(q_gain)) * jnp.max(jnp.abs(k_gain)) * (hd ** 0.5 * LOG2E)).reshape(1).astype(F32)


def _with_score_bound(bound_ref, attend):
    bound = bound_ref[0]
    fixed = bound <= MAX_FIXED_SHIFT

    @pl.when(fixed)
    def _():
        attend(bound)

    @pl.when(jnp.logical_not(fixed))
    def _():
        attend(None)


def _rolling_streams(n_streams, n_keys, s_scr, scores, next_scores, update):
    parked = s_scr.shape[0]
    out = []
    s = s_scr[...]
    if parked < n_keys:
        s = jnp.concatenate([s, scores(0, slice(parked, None))], axis=0)
    for st in range(n_streams):
        if st + 1 < n_streams:
            s_next = scores(st + 1, slice(None))
        elif next_scores is not None:
            s_scr[...] = next_scores(0, slice(0, parked))
        out.append(update(st, s))
        if st + 1 < n_streams:
            s = s_next
    return tuple(out)


def _prime_streams(s_scr, scores):
    s_scr[...] = scores(0, slice(0, s_scr.shape[0]))


def _moba_kernel(bound_ref, q_ref, k_ref, vt_ref, km_ref, o_ref, acc_scr, sel_scr, s_scr, *, tq):
    qi = pl.program_id(1)
    n_heads = q_ref.shape[1] // HEAD_DIM
    n_blk = km_ref.shape[1]
    per_block = LANES // HEAD_DIM
    qt = q_ref[...].T
    feat = lax.broadcasted_iota(jnp.int32, (LANES, tq), 0)
    blk = lax.broadcasted_iota(jnp.int32, (n_blk, tq), 0)
    blk_f = blk.astype(F32)
    krow = lax.broadcasted_iota(jnp.int32, (tq, per_block * tq), 0)
    qcol = lax.broadcasted_iota(jnp.int32, (tq, per_block * tq), 1) % tq
    causal = krow <= qcol

    n_streams = n_heads // per_block

    def lane_block(x, b):
        return x[:, b * LANES:(b + 1) * LANES]

    qhs = []
    for h in range(n_heads):
        b, hh = divmod(h, per_block)
        qhs.append(jnp.where((feat >= hh * HEAD_DIM) & (feat < (hh + 1) * HEAD_DIM),
                             qt[b * LANES:(b + 1) * LANES], 0.0))
    qts = [jnp.concatenate([(qh * (HEAD_DIM ** -0.5 * LOG2E)).astype(BF16)
                            for qh in qhs[b * per_block:(b + 1) * per_block]], axis=1) for b in range(n_streams)]

    def pair_keys(t):
        return k_ref[pl.ds(pl.multiple_of(t * 2 * tq, 2 * tq), 2 * tq), :]

    k_first = pair_keys(0)
    _prime_streams(s_scr, lambda b, rows: _dot(lane_block(k_first[rows], b), qts[b]))

    sels = []
    for h in range(n_heads):
        b, hh = divmod(h, per_block)
        if hh == 0:
            km_parts = _split3(km_ref[0, :, b * LANES:(b + 1) * LANES])

        k1, k2, k3 = km_parts
        q1, q2, q3 = _split3(qhs[h])
        gate = (_dot(k1, q1) + _dot(k2, q1) + _dot(k1, q2) + _dot(k3, q1) + _dot(k2, q2) + _dot(k1, q3))
        g = jnp.where(blk < qi, gate, NEG_INF)
        sel = jnp.zeros((n_blk, tq), F32)
        for _ in range(MOBA_TOPK):
            mx = jnp.max(g, axis=0, keepdims=True)
            first = jnp.min(jnp.where((g == mx) & (mx > NEG_INF), blk_f, float(n_blk)), axis=0, keepdims=True)
            pick = blk_f == first
            sel = jnp.where(pick, 1.0, sel)
            g = jnp.where(pick, NEG_INF, g)
        sels.append(sel)
        if hh == per_block - 1:
            sel_scr[b] = jnp.concatenate(sels, axis=1)
            sels = []

    acc_scr[...] = jnp.zeros_like(acc_scr)

    def pair_values(b, t):
        return [[vt_ref[0, b * per_block + hh, 2 * t], vt_ref[0, b * per_block + hh, 2 * t + 1]]
                for hh in range(per_block)]

    def masked(s, first, second):
        return jnp.concatenate([jnp.where(first, s[0:tq], NEG_INF), jnp.where(second, s[tq:2 * tq], NEG_INF)], axis=0)

    last = qi // 2
    own_second = qi % 2 == 1

    def attend(bound):
        def body(t, ms):
            kj = pair_keys(t)
            kn = pair_keys(t + 1)

            def update(b, s):
                first = sel_scr[b, pl.ds(2 * t, 1), :] > 0.0
                second = sel_scr[b, pl.ds(2 * t + 1, 1), :] > 0.0
                return _flash_update(acc_scr, b, masked(s, first, second), pair_values(b, t), ms[b], bound)

            return _rolling_streams(n_streams, 2 * tq, s_scr,
                                    lambda b, rows: _dot(lane_block(kj[rows], b), qts[b]),
                                    lambda b, rows: _dot(lane_block(kn[rows], b), qts[b]), update)

        ms = lax.fori_loop(0, last, body,
                           tuple(jnp.full((1, per_block * tq), NEG_INF, F32) for _ in range(n_streams)))
        k_last = pair_keys(last)

        def update_last(b, s):
            picked = sel_scr[b, pl.ds(2 * last, 1), :] > 0.0
            first = (own_second & picked) | (jnp.logical_not(own_second) & causal)
            second = own_second & causal
            return _flash_update(acc_scr, b, masked(s, first, second), pair_values(b, last), ms[b], bound)

        _rolling_streams(n_streams, 2 * tq, s_scr, lambda b, rows: _dot(lane_block(k_last[rows], b), qts[b]), None,
                         update_last)

    _with_score_bound(bound_ref, attend)

    outs = []
    for b in range(n_streams):
        acc = acc_scr[b]
        o = acc[0:HEAD_DIM] / acc[HEAD_DIM:HEAD_DIM + 1]
        outs += [o[:, hh * tq:(hh + 1) * tq] for hh in range(per_block)]
    o_ref[...] = jnp.concatenate(outs, axis=0).T.astype(o_ref.dtype)


def _moba(bound, qa, ka, vta, kmean, *, batch, seq):
    tq = MOBA_BLOCK
    t, width = qa.shape
    n_heads = width // HEAD_DIM
    n_q = seq // tq
    n_streams = width // LANES
    cols = (LANES // HEAD_DIM) * tq
    return pl.pallas_call(
        functools.partial(_moba_kernel, tq=tq),
        grid=(batch, n_q),
        in_specs=[
            pl.BlockSpec(memory_space=pltpu.SMEM),
            pl.BlockSpec((tq, width), lambda b, i: (b * n_q + i, 0)),
            pl.BlockSpec((seq, width), lambda b, i: (b, 0)),
            pl.BlockSpec((1, n_heads, n_q, V_ROWS, tq), lambda b, i: (b, 0, 0, 0, 0)),
            pl.BlockSpec((1, n_q, width), lambda b, i: (b, 0, 0)),
        ],
        out_specs=pl.BlockSpec((tq, width), lambda b, i: (b * n_q + i, 0)),
        out_shape=jax.ShapeDtypeStruct((t, width), BF16),
        scratch_shapes=[pltpu.VMEM((n_streams, V_ROWS, cols), F32), pltpu.VMEM((n_streams, n_q, cols), F32),
                        pltpu.VMEM((2 * tq, cols), F32)],
        compiler_params=_params("parallel", "arbitrary"),
        name="moba",
    )(bound, qa, ka, vta, kmean)


def _swa_kernel(bound_ref, q_ref, kp_ref, kc_ref, vtp_ref, vtc_ref, sink_ref, o_ref):
    i = pl.program_id(1)
    w = WINDOW
    n_sub = q_ref.shape[0] // w
    group = B_HEADS // B_KV_HEADS
    qt = q_ref[...].T
    krow = lax.broadcasted_iota(jnp.int32, (2 * w, w), 0)
    qcol = lax.broadcasted_iota(jnp.int32, (2 * w, w), 1)
    rel = qcol + w - krow
    in_window = (rel >= 0) & (rel < w)
    zeros = jnp.zeros((HEAD_DIM, w), F32)

    def key_block(j):
        return kp_ref[...] if j < 0 else kc_ref[j * w:(j + 1) * w, :]

    def value_block(kv, j):
        return vtp_ref[0, kv, 0] if j < 0 else vtc_ref[0, kv, j]

    def scores(j, kv):
        cols = []
        for g in range(group):
            h = kv * group + g
            qh = qt[h * HEAD_DIM:(h + 1) * HEAD_DIM, j * w:(j + 1) * w]
            cols.append(jnp.concatenate([qh if n == kv else zeros for n in range(B_KV_HEADS)], axis=0))
        kband = jnp.concatenate([key_block(j - 1), key_block(j)], axis=0)
        return _dot(kband, jnp.concatenate(cols, axis=1).astype(BF16))

    def attend(bound):
        jobs = [(j, kv) for j in range(n_sub) for kv in range(B_KV_HEADS)]
        all_scores = [scores(j, kv) for j, kv in jobs]
        outs = {}
        for (j, kv), s in zip(jobs, all_scores):
            ok = in_window & ((krow >= w) | (i > 0)) if j == 0 else in_window
            s = jnp.where(jnp.concatenate([ok] * group, axis=1), s, NEG_INF)
            sink = sink_ref[:, kv * group * w:(kv + 1) * group * w] * LOG2E
            m = jnp.maximum(jnp.max(s, axis=0, keepdims=True), sink) if bound is None else bound
            p = jnp.exp2(s - m).astype(BF16)
            pv = _dot(value_block(kv, j - 1), p[0:w]) + _dot(value_block(kv, j), p[w:2 * w])
            o = pv[0:HEAD_DIM] / (pv[HEAD_DIM:HEAD_DIM + 1] + jnp.exp2(sink - m))
            for g in range(group):
                outs[(kv * group + g, j)] = o[:, g * w:(g + 1) * w]
        ot = jnp.concatenate([jnp.concatenate([outs[(h, j)] for j in range(n_sub)], axis=1)
                              for h in range(B_HEADS)], axis=0)
        o_ref[...] = ot.T.astype(o_ref.dtype)

    _with_score_bound(bound_ref, attend)


def _swa(bound, qb, kb, vtb, sinks, *, batch, seq, n_sub=4):
    t, qw = qb.shape
    w = WINDOW
    nb = seq // w
    steps = nb // n_sub
    prev = lambda b, i: (b * nb + jnp.maximum(i * n_sub - 1, 0), 0)
    cur = lambda b, i: (b * steps + i, 0)
    sink_row = jnp.repeat(sinks.astype(F32), w).reshape(1, B_HEADS * w)
    return pl.pallas_call(
        _swa_kernel,
        grid=(batch, steps),
        in_specs=[pl.BlockSpec(memory_space=pltpu.SMEM),
                  pl.BlockSpec((n_sub * w, qw), cur),
                  pl.BlockSpec((w, LANES), prev),
                  pl.BlockSpec((n_sub * w, LANES), cur),
                  pl.BlockSpec((1, B_KV_HEADS, 1, V_ROWS, w),
                               lambda b, i: (b, 0, jnp.maximum(i * n_sub - 1, 0), 0, 0)),
                  pl.BlockSpec((1, B_KV_HEADS, n_sub, V_ROWS, w), lambda b, i: (b, 0, i, 0, 0)),
                  pl.BlockSpec((1, B_HEADS * w), lambda b, i: (0, 0))],
        out_specs=pl.BlockSpec((n_sub * w, qw), cur),
        out_shape=jax.ShapeDtypeStruct((t, qw), BF16),
        compiler_params=_params("parallel", "arbitrary"),
        name="swa",
    )(bound, qb, kb, kb, vtb, vtb, sink_row)


def _diff_kernel(bound_ref, q_ref, k_ref, vt_ref, lam_ref, g_ref, o_ref, acc_scr, s_scr, *, blk, lam_init):
    qi = pl.program_id(2)
    n_heads = LANES // HEAD_DIM
    qt = q_ref[...].T
    feat = lax.broadcasted_iota(jnp.int32, (LANES, blk), 0)
    krow = lax.broadcasted_iota(jnp.int32, (blk, blk), 0)
    qcol = lax.broadcasted_iota(jnp.int32, (blk, blk), 1)

    lq = lam_ref[...]
    lam = (jnp.exp(jnp.sum(lq[0:1] * lq[1:2], axis=1, keepdims=True))
           - jnp.exp(jnp.sum(lq[2:3] * lq[3:4], axis=1, keepdims=True)) + lam_init)

    n_streams = 2 * n_heads
    qts = [jnp.where((feat >= st * C_HALF) & (feat < (st + 1) * C_HALF), qt, 0.0).astype(BF16)
           for st in range(n_streams)]
    acc_scr[...] = jnp.zeros_like(acc_scr)

    def keys(j):
        return k_ref[pl.ds(pl.multiple_of(j * blk, blk), blk), :]

    k_first = keys(0)
    _prime_streams(s_scr, lambda st, rows: _dot(k_first[rows], qts[st]))

    def attend(bound):
        def body(j, ms):
            kj = keys(j)
            kn = keys(j + 1)
            return _rolling_streams(
                n_streams, blk, s_scr,
                lambda st, rows: _dot(kj[rows], qts[st]), lambda st, rows: _dot(kn[rows], qts[st]),
                lambda st, s: _flash_update(acc_scr, st, s, [[vt_ref[0, st // 2, j]]], ms[st], bound))

        ms = lax.fori_loop(0, qi, body, tuple(jnp.full((1, blk), NEG_INF, F32) for _ in range(n_streams)))
        k_own = keys(qi)
        _rolling_streams(
            n_streams, blk, s_scr, lambda st, rows: _dot(k_own[rows], qts[st]), None,
            lambda st, s: _flash_update(acc_scr, st, jnp.where(krow <= qcol, s, NEG_INF),
                                        [[vt_ref[0, st // 2, qi]]], ms[st], bound))

    _with_score_bound(bound_ref, attend)

    outs = []
    for h in range(n_heads):
        maps = [acc_scr[2 * h + c, 0:HEAD_DIM, :] / acc_scr[2 * h + c, HEAD_DIM:HEAD_DIM + 1, :] for c in range(2)]
        o = maps[0] - lam * maps[1]
        ms_o = jnp.mean(o * o, axis=0, keepdims=True)
        outs.append((o * lax.rsqrt(ms_o + EPS) * g_ref[...]) * (1.0 - lam_init))
    o_ref[...] = jnp.concatenate(outs, axis=0).T.astype(o_ref.dtype)


def _diff(bound, qc, kc, vtc, lam_rows, subln, *, batch, seq, lam_init, blk):
    t, width = qc.shape
    n_pairs = width // LANES
    n_heads = LANES // HEAD_DIM
    n_q = seq // blk
    return pl.pallas_call(
        functools.partial(_diff_kernel, blk=blk, lam_init=lam_init),
        grid=(batch, n_pairs, n_q),
        in_specs=[
            pl.BlockSpec(memory_space=pltpu.SMEM),
            pl.BlockSpec((blk, LANES), lambda b, hp, i: (b * n_q + i, hp)),
            pl.BlockSpec((seq, LANES), lambda b, hp, i: (b, hp)),
            pl.BlockSpec((1, n_heads, n_q, V_ROWS, blk), lambda b, hp, i: (b, hp, 0, 0, 0)),
            pl.BlockSpec((4, C_HALF), lambda b, hp, i: (0, 0)),
            pl.BlockSpec((HEAD_DIM, 1), lambda b, hp, i: (0, 0)),
        ],
        out_specs=pl.BlockSpec((blk, LANES), lambda b, hp, i: (b * n_q + i, hp)),
        out_shape=jax.ShapeDtypeStruct((t, width), BF16),
        scratch_shapes=[pltpu.VMEM((2 * n_heads, V_ROWS, blk), F32),
                        pltpu.VMEM((DIFF_PARKED_KEYS, blk), F32)],
        compiler_params=_params("parallel", "parallel", "arbitrary"),
        name="diff",
    )(bound, qc, kc, vtc, lam_rows, subln.reshape(HEAD_DIM, 1))


def _merge_kernel(x_ref, ya_ref, yb_ref, yc_ref, ga_ref, gb_ref, gc_ref, wa_ref, wb_ref, wc_ref, wo_ref, o_ref):
    merged = (jax.nn.sigmoid(ga_ref[...].astype(F32)) * _dot(ya_ref[...], wa_ref[...].astype(BF16))
              + jax.nn.sigmoid(gb_ref[...].astype(F32)) * _dot(yb_ref[...], wb_ref[...].astype(BF16))
              + jax.nn.sigmoid(gc_ref[...].astype(F32)) * _dot(yc_ref[...], wc_ref[...].astype(BF16)))
    o_ref[...] = x_ref[...] + _dot(merged.astype(BF16), wo_ref[...].astype(BF16))


def _merge(x2, ya, yb, yc, gates, w_pa, w_pb, w_pc, w_out, layer, *, tm=512):
    t, d = x2.shape
    rows = lambda i: (i, 0)
    return pl.pallas_call(
        _merge_kernel,
        grid=(t // tm,),
        in_specs=[
            pl.BlockSpec((tm, d), rows),
            pl.BlockSpec((tm, ya.shape[1]), rows),
            pl.BlockSpec((tm, yb.shape[1]), rows),
            pl.BlockSpec((tm, yc.shape[1]), rows),
            pl.BlockSpec((tm, d), lambda i: (i, 0)),
            pl.BlockSpec((tm, d), lambda i: (i, 1)),
            pl.BlockSpec((tm, d), lambda i: (i, 2)),
            _resident_layer(w_pa, layer),
            _resident_layer(w_pb, layer),
            _resident_layer(w_pc, layer),
            _resident_layer(w_out, layer),
        ],
        out_specs=pl.BlockSpec((tm, d), rows),
        out_shape=jax.ShapeDtypeStruct((t, d), F32),
        compiler_params=_params("parallel"),
        name="merge",
    )(x2, ya, yb, yc, gates, gates, gates, w_pa, w_pb, w_pc, w_out)


def _convglu_kernel(x_ref, xh_ref, g_ref, wu_ref, cw_ref, cb_ref, wd_ref, o_ref, h_scr, u_scr, act_scr,
                    *, tm, tf, tiles_per_seq):
    i = pl.program_id(0)
    halo = CONV_HALO
    ff = wd_ref.shape[0]
    n_chunks = ff // tf

    def normed(x):
        ms = jnp.mean(x * x, axis=-1, keepdims=True)
        return x * lax.rsqrt(ms + EPS) * g_ref[...]

    keep = jnp.where(i % tiles_per_seq == 0, 0.0, 1.0)
    h_scr[0:halo, :] = (normed(xh_ref[...]) * keep).astype(BF16)

    def up(c, rows=slice(None)):
        h = h_scr[rows]
        u_scr[c % 2, 0, rows] = _dot(h, wu_ref[:, c * tf:(c + 1) * tf])
        u_scr[c % 2, 1, rows] = _dot(h, wu_ref[:, ff + c * tf:ff + (c + 1) * tf])

    split = halo + tm // 2
    h_scr[halo:split, :] = normed(x_ref[0:tm // 2, :]).astype(BF16)
    up(0, slice(0, split))
    h_scr[split:, :] = normed(x_ref[tm // 2:, :]).astype(BF16)
    up(0, slice(split, None))

    def conv(c, half):
        col0 = half * ff + c * tf
        y = cb_ref[:, col0:col0 + tf]
        u = u_scr[c % 2, half]
        for j in range(CONV_W):
            back = CONV_W - 1 - j
            shifted = u if back == 0 else pltpu.roll(u, back, 0)
            y = y + shifted[halo:, :] * cw_ref[j:j + 1, col0:col0 + tf]
        return y

    for c in range(n_chunks):
        if c + 1 < n_chunks:
            up(c + 1)
        gate_u = conv(c, 0)
        val_u = conv(c, 1)
        act_scr[:, c * tf:(c + 1) * tf] = ((gate_u * jax.nn.sigmoid(gate_u)) * val_u).astype(BF16)

    o_ref[...] = x_ref[...] + _dot(act_scr[...], wd_ref[...].astype(BF16))


def _convglu(x2, gain, w_up, conv_w, conv_b, w_down, layer, *, seq, tm=512, tf=256):
    t, d = x2.shape
    ff = w_down.shape[1]
    halo = CONV_HALO
    tiles_per_seq = seq // tm
    halo_blocks = tm // halo
    return pl.pallas_call(
        functools.partial(_convglu_kernel, tm=tm, tf=tf, tiles_per_seq=tiles_per_seq),
        grid=(t // tm,),
        in_specs=[
            pl.BlockSpec((tm, d), lambda i: (i, 0)),
            pl.BlockSpec((halo, d), lambda i: (jnp.maximum(i * halo_blocks - 1, 0), 0)),
            _resident((1, d)),
            _resident_layer(w_up, layer),
            _resident(conv_w.shape),
            _resident((1, 2 * ff)),
            _resident_layer(w_down, layer),
        ],
        out_specs=pl.BlockSpec((tm, d), lambda i: (i, 0)),
        out_shape=jax.ShapeDtypeStruct((t, d), F32),
        scratch_shapes=[
            pltpu.VMEM((tm + halo, d), BF16),
            pltpu.VMEM((2, 2, tm + halo, tf), F32),
            pltpu.VMEM((tm, ff), BF16),
        ],
        compiler_params=_params("parallel"),
        name="convglu",
    )(x2, x2, gain.reshape(1, d), w_up, conv_w, conv_b.reshape(1, -1), w_down)


def _rope_tables(seq, dim):
    inv = 1.0 / (ROPE_THETA ** (jnp.arange(0, dim, 2, dtype=F32) / dim))
    ang = jnp.arange(seq, dtype=F32)[:, None] * inv[None, :]
    return jnp.cos(ang), jnp.sin(ang)


def kernel(x, attn_norm, w_in, qn_a, kn_a, qn_b, kn_b, sinks, qn_c, kn_c, lam_q1, lam_k1, lam_q2, lam_k2, subln,
           w_pa, w_pb, w_pc, w_out, mlp_norm, w_up, conv_w, conv_b, w_down):
    batch, seq, d = x.shape
    depth = w_in.shape[0]
    x2 = x.reshape(batch * seq, d)

    def rope_block(dim):
        cos, sin = _rope_tables(seq, dim)
        reps = LANES // dim
        return (jnp.tile(jnp.concatenate([cos, cos], axis=-1), (1, reps)),
                jnp.tile(jnp.concatenate([-sin, sin], axis=-1), (1, reps)))

    tables = rope_block(HEAD_DIM) + rope_block(C_HALF)

    segments = (("qa", A_W), ("ka", A_W), ("va", A_W), ("qb", B_QW), ("kb", B_KVW), ("vb", B_KVW),
                ("qc", C_W), ("kc", C_W), ("vc", C_W), ("gates", N_BRANCH * d))

    w_up_bf = w_up.astype(BF16)

    for i in range(depth):
        lam_init = 0.8 - 0.6 * float(np.exp(-0.3 * i))
        qk_gains = jnp.stack([jnp.tile(g, LANES // g.shape[0])
                              for g in (qn_a[i], kn_a[i], qn_b[i], kn_b[i], qn_c[i], kn_c[i])]).astype(F32)
        qa, ka, kmean, vta, qb, kb, vtb, qc, kc, vtc, gates = _inproj(
            x2, attn_norm[i], w_in, i, qk_gains, tables, batch=batch, seq=seq, segments=segments)

        kmean = kmean.reshape(batch, seq // MOBA_BLOCK, A_W)
        ya = _moba(_score_bound(qn_a[i], kn_a[i], HEAD_DIM), qa, ka, vta, kmean, batch=batch, seq=seq)
        yb = _swa(_score_bound(qn_b[i], kn_b[i], HEAD_DIM), qb, kb, vtb, sinks[i], batch=batch, seq=seq)
        lam_rows = jnp.stack([lam_q1[i], lam_k1[i], lam_q2[i], lam_k2[i]]).astype(F32)
        yc = _diff(_score_bound(qn_c[i], kn_c[i], C_HALF), qc, kc, vtc, lam_rows, subln[i], batch=batch, seq=seq, lam_init=lam_init, blk=DIFF_BLOCK)

        x2 = _merge(x2, ya, yb, yc, gates, w_pa, w_pb, w_pc, w_out, i)
        x2 = _convglu(x2, mlp_norm[i], w_up_bf, conv_w[i], conv_b[i], w_down, i, seq=seq)

    return x2.reshape(batch, seq, d)
```

```python
import functools

import numpy as np
import jax
import jax.numpy as jnp
from jax import lax
from jax.experimental import pallas as pl
from jax.experimental.pallas import tpu as pltpu

F32 = jnp.float32
BF16 = jnp.bfloat16
NEG_INF = float("-inf")

LANES = 128
VMEM_LIMIT = 48 * 1024 * 1024

EPS = 1e-6
HEAD_DIM = 64
ROPE_THETA = 10000.0
A_HEADS = 4
MOBA_BLOCK = 256
MOBA_TOPK = 3
B_HEADS = 8
B_KV_HEADS = 2
WINDOW = 128
C_HEADS = 4
C_HALF = HEAD_DIM // 2
N_BRANCH = 3
CONV_W = 3
CONV_HALO = 16
BF16_SUBLANES = 16
V_ROWS = HEAD_DIM + BF16_SUBLANES
DIFF_BLOCK = 512
DIFF_PARKED_KEYS = 256
LOG2E = 1.4426950408889634

A_W = A_HEADS * HEAD_DIM
B_QW = B_HEADS * HEAD_DIM
B_KVW = B_KV_HEADS * HEAD_DIM
C_W = C_HEADS * HEAD_DIM


def _dot(a, b):
    return jnp.dot(a, b, preferred_element_type=F32)


def _split3(a):
    a1 = a.astype(BF16)
    r = a - a1.astype(F32)
    a2 = r.astype(BF16)
    a3 = (r - a2.astype(F32)).astype(BF16)
    return a1, a2, a3


def _resident(shape):
    return pl.BlockSpec(shape, lambda *_: (0,) * len(shape), pipeline_mode=pl.Buffered(1))


def _resident_layer(stacked, layer):
    tail = stacked.shape[1:]
    return pl.BlockSpec((None,) + tail, lambda *_: (layer,) + (0,) * len(tail), pipeline_mode=pl.Buffered(1))


def _params(*semantics):
    return pltpu.CompilerParams(dimension_semantics=semantics, vmem_limit_bytes=VMEM_LIMIT)


def _group_sum_matrix(w, hd):
    idx = np.arange(w) // hd
    same = (idx[:, None] == idx[None, :]).astype(np.float32)
    return jnp.asarray(np.concatenate([same, same], axis=0), dtype=BF16)


MM_COLS = 256


def _inproj_kernel(x_ref, g_ref, w_ref, gains_ref, cos64_ref, sin64_ref, cos32_ref, sin32_ref, gs64_ref, gs32_ref,
                   qa_ref, ka_ref, km_ref, vta_ref, qb_ref, kb_ref, vtb_ref, qc_ref, kc_ref, vtc_ref, gates_ref,
                   h_scr, *, blocks, n_qkv):
    tm = x_ref.shape[0]

    def normalise(rows):
        x = x_ref[rows, :]
        ms = jnp.mean(x * x, axis=-1, keepdims=True)
        h_scr[rows, :] = (x * lax.rsqrt(ms + EPS) * g_ref[...]).astype(BF16)

    top, bottom = slice(0, tm // 2), slice(tm // 2, tm)
    normalise(top)
    first_top = _dot(h_scr[top, :], w_ref[:, 0:MM_COLS].astype(BF16))
    normalise(bottom)
    first_chunk = jnp.concatenate([first_top, _dot(h_scr[bottom, :], w_ref[:, 0:MM_COLS].astype(BF16))], axis=0)
    lane = lax.broadcasted_iota(jnp.int32, (tm, LANES), 1)

    def norm_rope(y, gain_row, hd, scale):
        cos_ref, sin_ref, gs_ref = ((cos64_ref, sin64_ref, gs64_ref) if hd == HEAD_DIM
                                    else (cos32_ref, sin32_ref, gs32_ref))
        sq = y * y
        hi = sq.astype(BF16)
        lo = (sq - hi.astype(F32)).astype(BF16)
        msq = _dot(jnp.concatenate([hi, lo], axis=1), gs_ref[...]) * (1.0 / hd)
        yn = y * lax.rsqrt(msq + EPS) * gains_ref[gain_row:gain_row + 1, :]
        half = hd // 2
        partner = jnp.where(lane % hd < half, pltpu.roll(yn, LANES - half, 1), pltpu.roll(yn, half, 1))
        out = yn * cos_ref[...] + partner * sin_ref[...]
        return out * scale if scale != 1.0 else out

    def store_vt(vt_ref, y, lb):
        tk = vt_ref.shape[-1]
        yt = y.T
        ones = jnp.ones((V_ROWS - HEAD_DIM, tk), F32)
        for hh in range(LANES // HEAD_DIM):
            for n in range(tm // tk):
                blk = yt[hh * HEAD_DIM:(hh + 1) * HEAD_DIM, n * tk:(n + 1) * tk]
                vt_ref[0, lb * (LANES // HEAD_DIM) + hh, n] = jnp.concatenate([blk, ones], axis=0).astype(BF16)

    q_scale = HEAD_DIM ** -0.5 * LOG2E
    c_scale = C_HALF ** -0.5 * LOG2E

    def epilogue(kind, lb, y):
        cols = slice(lb * LANES, (lb + 1) * LANES)
        if kind == "qa":
            qa_ref[:, cols] = norm_rope(y, 0, HEAD_DIM, 1.0)
        elif kind == "ka":
            yk = norm_rope(y, 1, HEAD_DIM, 1.0)
            ka_ref[:, cols] = yk.astype(BF16)
            for n in range(tm // MOBA_BLOCK):
                rows = yk[n * MOBA_BLOCK:(n + 1) * MOBA_BLOCK]
                km_ref[n, :, cols] = jnp.sum(rows, axis=0, keepdims=True) * (1.0 / MOBA_BLOCK)
        elif kind == "va":
            store_vt(vta_ref, y, lb)
        elif kind == "qb":
            qb_ref[:, cols] = norm_rope(y, 2, HEAD_DIM, q_scale)
        elif kind == "kb":
            kb_ref[:, cols] = norm_rope(y, 3, HEAD_DIM, 1.0).astype(BF16)
        elif kind == "vb":
            store_vt(vtb_ref, y, lb)
        elif kind == "qc":
            qc_ref[:, cols] = norm_rope(y, 4, C_HALF, c_scale)
        elif kind == "kc":
            kc_ref[:, cols] = norm_rope(y, 5, C_HALF, 1.0).astype(BF16)
        elif kind == "vc":
            store_vt(vtc_ref, y, lb)

    def chunk(c):
        return _dot(h_scr[...], w_ref[:, c * MM_COLS:(c + 1) * MM_COLS].astype(BF16))

    n_chunks = w_ref.shape[1] // MM_COLS
    y_next = first_chunk
    for c in range(n_chunks):
        y = y_next
        if c + 1 < n_chunks:
            y_next = chunk(c + 1)
        col0 = c * MM_COLS
        if col0 < n_qkv:
            for part in range(MM_COLS // LANES):
                kind, lb = blocks[col0 // LANES + part]
                epilogue(kind, lb, y[:, part * LANES:(part + 1) * LANES])
        else:
            gates_ref[:, col0 - n_qkv:col0 - n_qkv + MM_COLS] = y.astype(gates_ref.dtype)


def _inproj(x2, gain, w_in, layer, qk_gains, tables, *, batch, seq, segments, tm=512):
    t, d = x2.shape
    n = w_in.shape[2]
    blocks = [(name, lb) for name, width in segments[:-1] for lb in range(width // LANES)]
    n_qkv = len(blocks) * LANES
    width = dict(segments)
    tps = seq // tm
    rows = lambda i: (i, 0)
    pos = lambda i: (i % tps, 0)

    def vt_spec(heads, tk):
        return pl.BlockSpec((1, heads, tm // tk, V_ROWS, tk), lambda i: (i // tps, 0, i % tps, 0, 0))

    def vt_shape(heads, tk):
        return jax.ShapeDtypeStruct((batch, heads, seq // tk, V_ROWS, tk), BF16)

    n_mb = tm // MOBA_BLOCK
    out_specs = [
        pl.BlockSpec((tm, width["qa"]), rows),
        pl.BlockSpec((tm, width["ka"]), rows),
        pl.BlockSpec((n_mb, 1, width["ka"]), lambda i: (i, 0, 0)),
        vt_spec(A_HEADS, MOBA_BLOCK),
        pl.BlockSpec((tm, width["qb"]), rows),
        pl.BlockSpec((tm, width["kb"]), rows),
        vt_spec(B_KV_HEADS, WINDOW),
        pl.BlockSpec((tm, width["qc"]), rows),
        pl.BlockSpec((tm, width["kc"]), rows),
        vt_spec(C_HEADS, DIFF_BLOCK),
        pl.BlockSpec((tm, width["gates"]), rows),
    ]
    out_shape = [
        jax.ShapeDtypeStruct((t, width["qa"]), F32),
        jax.ShapeDtypeStruct((t, width["ka"]), BF16),
        jax.ShapeDtypeStruct((t // MOBA_BLOCK, 1, width["ka"]), F32),
        vt_shape(A_HEADS, MOBA_BLOCK),
        jax.ShapeDtypeStruct((t, width["qb"]), F32),
        jax.ShapeDtypeStruct((t, width["kb"]), BF16),
        vt_shape(B_KV_HEADS, WINDOW),
        jax.ShapeDtypeStruct((t, width["qc"]), F32),
        jax.ShapeDtypeStruct((t, width["kc"]), BF16),
        vt_shape(C_HEADS, DIFF_BLOCK),
        jax.ShapeDtypeStruct((t, width["gates"]), BF16),
    ]
    table_spec = pl.BlockSpec((tm, LANES), pos)
    return pl.pallas_call(
        functools.partial(_inproj_kernel, blocks=blocks, n_qkv=n_qkv),
        grid=(t // tm,),
        in_specs=[pl.BlockSpec((tm, d), rows), _resident((1, d)), _resident_layer(w_in, layer),
                  _resident(qk_gains.shape),
                  table_spec, table_spec, table_spec, table_spec,
                  _resident((2 * LANES, LANES)), _resident((2 * LANES, LANES))],
        out_specs=out_specs,
        out_shape=out_shape,
        scratch_shapes=[pltpu.VMEM((tm, d), BF16)],
        compiler_params=_params("parallel"),
        name="inproj",
    )(x2, gain.reshape(1, d), w_in, qk_gains, *tables, _group_sum_matrix(LANES, HEAD_DIM),
      _group_sum_matrix(LANES, C_HALF))


def _flash_update(acc_ref, stream, s, vt_groups, m, bound=None):
    if bound is None:
        m_new = jnp.maximum(m, jnp.max(s, axis=0, keepdims=True))
        shift = jnp.where(m_new == NEG_INF, 0.0, m_new)
        alpha = jnp.exp2(m - shift)
    else:
        m_new, shift, alpha = m, bound, None
    p = jnp.exp2(s - shift).astype(BF16)
    cw = p.shape[1] // len(vt_groups)
    pvs = []
    for g, vts in enumerate(vt_groups):
        tk = vts[0].shape[1]
        pv = _dot(vts[0], p[0:tk, g * cw:(g + 1) * cw])
        for n in range(1, len(vts)):
            pv = pv + _dot(vts[n], p[n * tk:(n + 1) * tk, g * cw:(g + 1) * cw])
        pvs.append(pv)
    pv = pvs[0] if len(pvs) == 1 else jnp.concatenate(pvs, axis=1)
    acc_ref[stream] = acc_ref[stream] + pv if alpha is None else alpha * acc_ref[stream] + pv
    return m_new


MAX_FIXED_SHIFT = 60.0


def _score_bound(q_gain, k_gain, hd):
    return (jnp.max(jnp.abs(q_gain)) * jnp.max(jnp.abs(k_gain)) * (hd ** 0.5 * LOG2E)).reshape(1).astype(F32)


def _with_score_bound(bound_ref, attend):
    bound = bound_ref[0]
    fixed = bound <= MAX_FIXED_SHIFT

    @pl.when(fixed)
    def _():
        attend(bound)

    @pl.when(jnp.logical_not(fixed))
    def _():
        attend(None)


def _rolling_streams(n_streams, n_keys, s_scr, scores, next_scores, update):
    parked = s_scr.shape[0]
    out = []
    s = s_scr[...]
    if parked < n_keys:
        s = jnp.concatenate([s, scores(0, slice(parked, None))], axis=0)
    for st in range(n_streams):
        if st + 1 < n_streams:
            s_next = scores(st + 1, slice(None))
        elif next_scores is not None:
            s_scr[...] = next_scores(0, slice(0, parked))
        out.append(update(st, s))
        if st + 1 < n_streams:
            s = s_next
    return tuple(out)


def _prime_streams(s_scr, scores):
    s_scr[...] = scores(0, slice(0, s_scr.shape[0]))


def _moba_kernel(bound_ref, q_ref, k_ref, vt_ref, km_ref, o_ref, acc_scr, sel_scr, s_scr, *, tq):
    qi = pl.program_id(1)
    n_heads = q_ref.shape[1] // HEAD_DIM
    n_blk = km_ref.shape[1]
    per_block = LANES // HEAD_DIM
    qt = q_ref[...].T
    feat = lax.broadcasted_iota(jnp.int32, (LANES, tq), 0)
    blk = lax.broadcasted_iota(jnp.int32, (n_blk, tq), 0)
    blk_f = blk.astype(F32)
    krow = lax.broadcasted_iota(jnp.int32, (tq, per_block * tq), 0)
    qcol = lax.broadcasted_iota(jnp.int32, (tq, per_block * tq), 1) % tq
    causal = krow <= qcol

    n_streams = n_heads // per_block

    def lane_block(x, b):
        return x[:, b * LANES:(b + 1) * LANES]

    qhs = []
    for h in range(n_heads):
        b, hh = divmod(h, per_block)
        qhs.append(jnp.where((feat >= hh * HEAD_DIM) & (feat < (hh + 1) * HEAD_DIM),
                             qt[b * LANES:(b + 1) * LANES], 0.0))
    qts = [jnp.concatenate([(qh * (HEAD_DIM ** -0.5 * LOG2E)).astype(BF16)
                            for qh in qhs[b * per_block:(b + 1) * per_block]], axis=1) for b in range(n_streams)]

    def pair_keys(t):
        return k_ref[pl.ds(pl.multiple_of(t * 2 * tq, 2 * tq), 2 * tq), :]

    k_first = pair_keys(0)
    _prime_streams(s_scr, lambda b, rows: _dot(lane_block(k_first[rows], b), qts[b]))

    sels = []
    for h in range(n_heads):
        b, hh = divmod(h, per_block)
        if hh == 0:
            km_parts = _split3(km_ref[0, :, b * LANES:(b + 1) * LANES])

        k1, k2, k3 = km_parts
        q1, q2, q3 = _split3(qhs[h])
        gate = (_dot(k1, q1) + _dot(k2, q1) + _dot(k1, q2) + _dot(k3, q1) + _dot(k2, q2) + _dot(k1, q3))
        g = jnp.where(blk < qi, gate, NEG_INF)
        sel = jnp.zeros((n_blk, tq), F32)
        for _ in range(MOBA_TOPK):
            mx = jnp.max(g, axis=0, keepdims=True)
            first = jnp.min(jnp.where((g == mx) & (mx > NEG_INF), blk_f, float(n_blk)), axis=0, keepdims=True)
            pick = blk_f == first
            sel = jnp.where(pick, 1.0, sel)
            g = jnp.where(pick, NEG_INF, g)
        sels.append(sel)
        if hh == per_block - 1:
            sel_scr[b] = jnp.concatenate(sels, axis=1)
            sels = []

    acc_scr[...] = jnp.zeros_like(acc_scr)

    def pair_values(b, t):
        return [[vt_ref[0, b * per_block + hh, 2 * t], vt_ref[0, b * per_block + hh, 2 * t + 1]]
                for hh in range(per_block)]

    def masked(s, first, second):
        return jnp.concatenate([jnp.where(first, s[0:tq], NEG_INF), jnp.where(second, s[tq:2 * tq], NEG_INF)], axis=0)

    last = qi // 2
    own_second = qi % 2 == 1

    def attend(bound):
        def body(t, ms):
            kj = pair_keys(t)
            kn = pair_keys(t + 1)

            def update(b, s):
                first = sel_scr[b, pl.ds(2 * t, 1), :] > 0.0
                second = sel_scr[b, pl.ds(2 * t + 1, 1), :] > 0.0
                return _flash_update(acc_scr, b, masked(s, first, second), pair_values(b, t), ms[b], bound)

            return _rolling_streams(n_streams, 2 * tq, s_scr,
                                    lambda b, rows: _dot(lane_block(kj[rows], b), qts[b]),
                                    lambda b, rows: _dot(lane_block(kn[rows], b), qts[b]), update)

        ms = lax.fori_loop(0, last, body,
                           tuple(jnp.full((1, per_block * tq), NEG_INF, F32) for _ in range(n_streams)))
        k_last = pair_keys(last)

        def update_last(b, s):
            picked = sel_scr[b, pl.ds(2 * last, 1), :] > 0.0
            first = (own_second & picked) | (jnp.logical_not(own_second) & causal)
            second = own_second & causal
            return _flash_update(acc_scr, b, masked(s, first, second), pair_values(b, last), ms[b], bound)

        _rolling_streams(n_streams, 2 * tq, s_scr, lambda b, rows: _dot(lane_block(k_last[rows], b), qts[b]), None,
                         update_last)

    _with_score_bound(bound_ref, attend)

    outs = []
    for b in range(n_streams):
        acc = acc_scr[b]
        o = acc[0:HEAD_DIM] / acc[HEAD_DIM:HEAD_DIM + 1]
        outs += [o[:, hh * tq:(hh + 1) * tq] for hh in range(per_block)]
    o_ref[...] = jnp.concatenate(outs, axis=0).T.astype(o_ref.dtype)


def _moba(bound, qa, ka, vta, kmean, *, batch, seq):
    tq = MOBA_BLOCK
    t, width = qa.shape
    n_heads = width // HEAD_DIM
    n_q = seq // tq
    n_streams = width // LANES
    cols = (LANES // HEAD_DIM) * tq
    return pl.pallas_call(
        functools.partial(_moba_kernel, tq=tq),
        grid=(batch, n_q),
        in_specs=[
            pl.BlockSpec(memory_space=pltpu.SMEM),
            pl.BlockSpec((tq, width), lambda b, i: (b * n_q + i, 0)),
            pl.BlockSpec((seq, width), lambda b, i: (b, 0)),
            pl.BlockSpec((1, n_heads, n_q, V_ROWS, tq), lambda b, i: (b, 0, 0, 0, 0)),
            pl.BlockSpec((1, n_q, width), lambda b, i: (b, 0, 0)),
        ],
        out_specs=pl.BlockSpec((tq, width), lambda b, i: (b * n_q + i, 0)),
        out_shape=jax.ShapeDtypeStruct((t, width), BF16),
        scratch_shapes=[pltpu.VMEM((n_streams, V_ROWS, cols), F32), pltpu.VMEM((n_streams, n_q, cols), F32),
                        pltpu.VMEM((2 * tq, cols), F32)],
        compiler_params=_params("parallel", "arbitrary"),
        name="moba",
    )(bound, qa, ka, vta, kmean)


def _swa_kernel(bound_ref, q_ref, kp_ref, kc_ref, vtp_ref, vtc_ref, sink_ref, o_ref):
    i = pl.program_id(1)
    w = WINDOW
    n_sub = q_ref.shape[0] // w
    group = B_HEADS // B_KV_HEADS
    qt = q_ref[...].T
    krow = lax.broadcasted_iota(jnp.int32, (2 * w, w), 0)
    qcol = lax.broadcasted_iota(jnp.int32, (2 * w, w), 1)
    rel = qcol + w - krow
    in_window = (rel >= 0) & (rel < w)
    zeros = jnp.zeros((HEAD_DIM, w), F32)

    def key_block(j):
        return kp_ref[...] if j < 0 else kc_ref[j * w:(j + 1) * w, :]

    def value_block(kv, j):
        return vtp_ref[0, kv, 0] if j < 0 else vtc_ref[0, kv, j]

    def scores(j, kv):
        cols = []
        for g in range(group):
            h = kv * group + g
            qh = qt[h * HEAD_DIM:(h + 1) * HEAD_DIM, j * w:(j + 1) * w]
            cols.append(jnp.concatenate([qh if n == kv else zeros for n in range(B_KV_HEADS)], axis=0))
        kband = jnp.concatenate([key_block(j - 1), key_block(j)], axis=0)
        return _dot(kband, jnp.concatenate(cols, axis=1).astype(BF16))

    def attend(bound):
        jobs = [(j, kv) for j in range(n_sub) for kv in range(B_KV_HEADS)]
        all_scores = [scores(j, kv) for j, kv in jobs]
        outs = {}
        for (j, kv), s in zip(jobs, all_scores):
            ok = in_window & ((krow >= w) | (i > 0)) if j == 0 else in_window
            s = jnp.where(jnp.concatenate([ok] * group, axis=1), s, NEG_INF)
            sink = sink_ref[:, kv * group * w:(kv + 1) * group * w] * LOG2E
            m = jnp.maximum(jnp.max(s, axis=0, keepdims=True), sink) if bound is None else bound
            p = jnp.exp2(s - m).astype(BF16)
            pv = _dot(value_block(kv, j - 1), p[0:w]) + _dot(value_block(kv, j), p[w:2 * w])
            o = pv[0:HEAD_DIM] / (pv[HEAD_DIM:HEAD_DIM + 1] + jnp.exp2(sink - m))
            for g in range(group):
                outs[(kv * group + g, j)] = o[:, g * w:(g + 1) * w]
        ot = jnp.concatenate([jnp.concatenate([outs[(h, j)] for j in range(n_sub)], axis=1)
                              for h in range(B_HEADS)], axis=0)
        o_ref[...] = ot.T.astype(o_ref.dtype)

    _with_score_bound(bound_ref, attend)


def _swa(bound, qb, kb, vtb, sinks, *, batch, seq, n_sub=4):
    t, qw = qb.shape
    w = WINDOW
    nb = seq // w
    steps = nb // n_sub
    prev = lambda b, i: (b * nb + jnp.maximum(i * n_sub - 1, 0), 0)
    cur = lambda b, i: (b * steps + i, 0)
    sink_row = jnp.repeat(sinks.astype(F32), w).reshape(1, B_HEADS * w)
    return pl.pallas_call(
        _swa_kernel,
        grid=(batch, steps),
        in_specs=[pl.BlockSpec(memory_space=pltpu.SMEM),
                  pl.BlockSpec((n_sub * w, qw), cur),
                  pl.BlockSpec((w, LANES), prev),
                  pl.BlockSpec((n_sub * w, LANES), cur),
                  pl.BlockSpec((1, B_KV_HEADS, 1, V_ROWS, w),
                               lambda b, i: (b, 0, jnp.maximum(i * n_sub - 1, 0), 0, 0)),
                  pl.BlockSpec((1, B_KV_HEADS, n_sub, V_ROWS, w), lambda b, i: (b, 0, i, 0, 0)),
                  pl.BlockSpec((1, B_HEADS * w), lambda b, i: (0, 0))],
        out_specs=pl.BlockSpec((n_sub * w, qw), cur),
        out_shape=jax.ShapeDtypeStruct((t, qw), BF16),
        compiler_params=_params("parallel", "arbitrary"),
        name="swa",
    )(bound, qb, kb, kb, vtb, vtb, sink_row)


def _diff_kernel(bound_ref, q_ref, k_ref, vt_ref, lam_ref, g_ref, o_ref, acc_scr, s_scr, *, blk, lam_init):
    qi = pl.program_id(1)
    n_heads = q_ref.shape[1] // HEAD_DIM
    per_block = LANES // C_HALF
    qt = q_ref[...].T
    feat = lax.broadcasted_iota(jnp.int32, (LANES, blk), 0)
    krow = lax.broadcasted_iota(jnp.int32, (blk, blk), 0)
    qcol = lax.broadcasted_iota(jnp.int32, (blk, blk), 1)

    lq = lam_ref[...]
    lam = (jnp.exp(jnp.sum(lq[0:1] * lq[1:2], axis=1, keepdims=True))
           - jnp.exp(jnp.sum(lq[2:3] * lq[3:4], axis=1, keepdims=True)) + lam_init)

    n_streams = 2 * n_heads
    qts = []
    for st in range(n_streams):
        b, r = divmod(st, per_block)
        qts.append(jnp.where((feat >= r * C_HALF) & (feat < (r + 1) * C_HALF), qt[b * LANES:(b + 1) * LANES],
                             0.0).astype(BF16))
    acc_scr[...] = jnp.zeros_like(acc_scr)

    def keys(j):
        return k_ref[pl.ds(pl.multiple_of(j * blk, blk), blk), :]

    def stream_scores(kblk):
        def scores(st, rows):
            b = st // per_block
            return _dot(kblk[rows][:, b * LANES:(b + 1) * LANES], qts[st])
        return scores

    _prime_streams(s_scr, stream_scores(keys(0)))

    def attend(bound):
        def body(j, ms):
            return _rolling_streams(
                n_streams, blk, s_scr, stream_scores(keys(j)), stream_scores(keys(j + 1)),
                lambda st, s: _flash_update(acc_scr, st, s, [[vt_ref[0, st // 2, j]]], ms[st], bound))

        ms = lax.fori_loop(0, qi, body, tuple(jnp.full((1, blk), NEG_INF, F32) for _ in range(n_streams)))
        _rolling_streams(
            n_streams, blk, s_scr, stream_scores(keys(qi)), None,
            lambda st, s: _flash_update(acc_scr, st, jnp.where(krow <= qcol, s, NEG_INF),
                                        [[vt_ref[0, st // 2, qi]]], ms[st], bound))

    _with_score_bound(bound_ref, attend)

    outs = []
    for h in range(n_heads):
        maps = [acc_scr[2 * h + c, 0:HEAD_DIM, :] / acc_scr[2 * h + c, HEAD_DIM:HEAD_DIM + 1, :] for c in range(2)]
        o = maps[0] - lam * maps[1]
        ms_o = jnp.mean(o * o, axis=0, keepdims=True)
        outs.append((o * lax.rsqrt(ms_o + EPS) * g_ref[...]) * (1.0 - lam_init))
    o_ref[...] = jnp.concatenate(outs, axis=0).T.astype(o_ref.dtype)


def _diff(bound, qc, kc, vtc, lam_rows, subln, *, batch, seq, lam_init, blk):
    t, width = qc.shape
    n_heads = width // HEAD_DIM
    n_q = seq // blk
    return pl.pallas_call(
        functools.partial(_diff_kernel, blk=blk, lam_init=lam_init),
        grid=(batch, n_q),
        in_specs=[
            pl.BlockSpec(memory_space=pltpu.SMEM),
            pl.BlockSpec((blk, width), lambda b, i: (b * n_q + i, 0)),
            pl.BlockSpec((seq, width), lambda b, i: (b, 0)),
            pl.BlockSpec((1, n_heads, n_q, V_ROWS, blk), lambda b, i: (b, 0, 0, 0, 0)),
            pl.BlockSpec((4, C_HALF), lambda b, i: (0, 0)),
            pl.BlockSpec((HEAD_DIM, 1), lambda b, i: (0, 0)),
        ],
        out_specs=pl.BlockSpec((blk, width), lambda b, i: (b * n_q + i, 0)),
        out_shape=jax.ShapeDtypeStruct((t, width), BF16),
        scratch_shapes=[pltpu.VMEM((2 * n_heads, V_ROWS, blk), F32),
                        pltpu.VMEM((DIFF_PARKED_KEYS, blk), F32)],
        compiler_params=_params("parallel", "arbitrary"),
        name="diff",
    )(bound, qc, kc, vtc, lam_rows, subln.reshape(HEAD_DIM, 1))


def _merge_kernel(x_ref, ya_ref, yb_ref, yc_ref, ga_ref, gb_ref, gc_ref, wa_ref, wb_ref, wc_ref, wo_ref, o_ref):
    merged = (jax.nn.sigmoid(ga_ref[...].astype(F32)) * _dot(ya_ref[...], wa_ref[...].astype(BF16))
              + jax.nn.sigmoid(gb_ref[...].astype(F32)) * _dot(yb_ref[...], wb_ref[...].astype(BF16))
              + jax.nn.sigmoid(gc_ref[...].astype(F32)) * _dot(yc_ref[...], wc_ref[...].astype(BF16)))
    o_ref[...] = x_ref[...] + _dot(merged.astype(BF16), wo_ref[...].astype(BF16))


def _merge(x2, ya, yb, yc, gates, w_pa, w_pb, w_pc, w_out, layer, *, tm=512):
    t, d = x2.shape
    rows = lambda i: (i, 0)
    return pl.pallas_call(
        _merge_kernel,
        grid=(t // tm,),
        in_specs=[
            pl.BlockSpec((tm, d), rows),
            pl.BlockSpec((tm, ya.shape[1]), rows),
            pl.BlockSpec((tm, yb.shape[1]), rows),
            pl.BlockSpec((tm, yc.shape[1]), rows),
            pl.BlockSpec((tm, d), lambda i: (i, 0)),
            pl.BlockSpec((tm, d), lambda i: (i, 1)),
            pl.BlockSpec((tm, d), lambda i: (i, 2)),
            _resident_layer(w_pa, layer),
            _resident_layer(w_pb, layer),
            _resident_layer(w_pc, layer),
            _resident_layer(w_out, layer),
        ],
        out_specs=pl.BlockSpec((tm, d), rows),
        out_shape=jax.ShapeDtypeStruct((t, d), F32),
        compiler_params=_params("parallel"),
        name="merge",
    )(x2, ya, yb, yc, gates, gates, gates, w_pa, w_pb, w_pc, w_out)


def _convglu_kernel(x_ref, xh_ref, g_ref, wu_ref, cw_ref, cb_ref, wd_ref, o_ref, h_scr, u_scr, act_scr,
                    *, tm, tf, tiles_per_seq):
    i = pl.program_id(0)
    halo = CONV_HALO
    ff = wd_ref.shape[0]
    n_chunks = ff // tf

    def normed(x):
        ms = jnp.mean(x * x, axis=-1, keepdims=True)
        return x * lax.rsqrt(ms + EPS) * g_ref[...]

    keep = jnp.where(i % tiles_per_seq == 0, 0.0, 1.0)
    h_scr[0:halo, :] = (normed(xh_ref[...]) * keep).astype(BF16)

    def up(c, rows=slice(None)):
        h = h_scr[rows]
        u_scr[c % 2, 0, rows] = _dot(h, wu_ref[:, c * tf:(c + 1) * tf])
        u_scr[c % 2, 1, rows] = _dot(h, wu_ref[:, ff + c * tf:ff + (c + 1) * tf])

    split = halo + tm // 2
    h_scr[halo:split, :] = normed(x_ref[0:tm // 2, :]).astype(BF16)
    up(0, slice(0, split))
    h_scr[split:, :] = normed(x_ref[tm // 2:, :]).astype(BF16)
    up(0, slice(split, None))

    def conv(c, half):
        col0 = half * ff + c * tf
        y = cb_ref[:, col0:col0 + tf]
        u = u_scr[c % 2, half]
        for j in range(CONV_W):
            back = CONV_W - 1 - j
            shifted = u if back == 0 else pltpu.roll(u, back, 0)
            y = y + shifted[halo:, :] * cw_ref[j:j + 1, col0:col0 + tf]
        return y

    for c in range(n_chunks):
        if c + 1 < n_chunks:
            up(c + 1)
        gate_u = conv(c, 0)
        val_u = conv(c, 1)
        act_scr[:, c * tf:(c + 1) * tf] = ((gate_u * jax.nn.sigmoid(gate_u)) * val_u).astype(BF16)

    o_ref[...] = x_ref[...] + _dot(act_scr[...], wd_ref[...].astype(BF16))


def _convglu(x2, gain, w_up, conv_w, conv_b, w_down, layer, *, seq, tm=512, tf=256):
    t, d = x2.shape
    ff = w_down.shape[1]
    halo = CONV_HALO
    tiles_per_seq = seq // tm
    halo_blocks = tm // halo
    return pl.pallas_call(
        functools.partial(_convglu_kernel, tm=tm, tf=tf, tiles_per_seq=tiles_per_seq),
        grid=(t // tm,),
        in_specs=[
            pl.BlockSpec((tm, d), lambda i: (i, 0)),
            pl.BlockSpec((halo, d), lambda i: (jnp.maximum(i * halo_blocks - 1, 0), 0)),
            _resident((1, d)),
            _resident_layer(w_up, layer),
            _resident(conv_w.shape),
            _resident((1, 2 * ff)),
            _resident_layer(w_down, layer),
        ],
        out_specs=pl.BlockSpec((tm, d), lambda i: (i, 0)),
        out_shape=jax.ShapeDtypeStruct((t, d), F32),
        scratch_shapes=[
            pltpu.VMEM((tm + halo, d), BF16),
            pltpu.VMEM((2, 2, tm + halo, tf), F32),
            pltpu.VMEM((tm, ff), BF16),
        ],
        compiler_params=_params("parallel"),
        name="convglu",
    )(x2, x2, gain.reshape(1, d), w_up, conv_w, conv_b.reshape(1, -1), w_down)


def _rope_tables(seq, dim):
    inv = 1.0 / (ROPE_THETA ** (jnp.arange(0, dim, 2, dtype=F32) / dim))
    ang = jnp.arange(seq, dtype=F32)[:, None] * inv[None, :]
    return jnp.cos(ang), jnp.sin(ang)


def kernel(x, attn_norm, w_in, qn_a, kn_a, qn_b, kn_b, sinks, qn_c, kn_c, lam_q1, lam_k1, lam_q2, lam_k2, subln,
           w_pa, w_pb, w_pc, w_out, mlp_norm, w_up, conv_w, conv_b, w_down):
    batch, seq, d = x.shape
    depth = w_in.shape[0]
    x2 = x.reshape(batch * seq, d)

    def rope_block(dim):
        cos, sin = _rope_tables(seq, dim)
        reps = LANES // dim
        return (jnp.tile(jnp.concatenate([cos, cos], axis=-1), (1, reps)),
                jnp.tile(jnp.concatenate([-sin, sin], axis=-1), (1, reps)))

    tables = rope_block(HEAD_DIM) + rope_block(C_HALF)

    segments = (("qa", A_W), ("ka", A_W), ("va", A_W), ("qb", B_QW), ("kb", B_KVW), ("vb", B_KVW),
                ("qc", C_W), ("kc", C_W), ("vc", C_W), ("gates", N_BRANCH * d))

    w_up_bf = w_up.astype(BF16)

    for i in range(depth):
        lam_init = 0.8 - 0.6 * float(np.exp(-0.3 * i))
        qk_gains = jnp.stack([jnp.tile(g, LANES // g.shape[0])
                              for g in (qn_a[i], kn_a[i], qn_b[i], kn_b[i], qn_c[i], kn_c[i])]).astype(F32)
        qa, ka, kmean, vta, qb, kb, vtb, qc, kc, vtc, gates = _inproj(
            x2, attn_norm[i], w_in, i, qk_gains, tables, batch=batch, seq=seq, segments=segments)

        kmean = kmean.reshape(batch, seq // MOBA_BLOCK, A_W)
        ya = _moba(_score_bound(qn_a[i], kn_a[i], HEAD_DIM), qa, ka, vta, kmean, batch=batch, seq=seq)
        yb = _swa(_score_bound(qn_b[i], kn_b[i], HEAD_DIM), qb, kb, vtb, sinks[i], batch=batch, seq=seq)
        lam_rows = jnp.stack([lam_q1[i], lam_k1[i], lam_q2[i], lam_k2[i]]).astype(F32)
        yc = _diff(_score_bound(qn_c[i], kn_c[i], C_HALF), qc, kc, vtc, lam_rows, subln[i], batch=batch, seq=seq, lam_init=lam_init, blk=DIFF_BLOCK)

        x2 = _merge(x2, ya, yb, yc, gates, w_pa, w_pb, w_pc, w_out, i)
        x2 = _convglu(x2, mlp_norm[i], w_up_bf, conv_w[i], conv_b[i], w_down, i, seq=seq)

    return x2.reshape(batch, seq, d)
```
